```python
import jax, jax.numpy as jnp
from jax import lax
import numpy as np

D_MODEL = 4096
BATCH = 1
SEQ = 8192
DEPTH = 1

MIX_WIDTH = D_MODEL
MLSTM_WIDTH = D_MODEL // 2
ATTN_WIDTH = MIX_WIDTH - MLSTM_WIDTH
MLSTM_HEADS = 4
MLSTM_DV = MLSTM_WIDTH // MLSTM_HEADS
MLSTM_DQK = MLSTM_DV // 2
MLSTM_QK_WIDTH = MLSTM_HEADS * MLSTM_DQK
MLSTM_CHUNK = 64
CONV_WIDTH = 4
GATE_SOFTCAP = 15.0
ATTN_HEAD_DIM = 128
ATTN_HEADS = ATTN_WIDTH // ATTN_HEAD_DIM
DILATED_PATTERNS = ((128, 1), (512, 4), (2048, 16))
ATTN_BLOCK = 128
D_FF = 4 * D_MODEL
EPS = 1e-6
PROJ_SIZES = (MLSTM_QK_WIDTH, MLSTM_QK_WIDTH, MLSTM_WIDTH, MLSTM_WIDTH, MLSTM_HEADS, MLSTM_HEADS, ATTN_WIDTH, ATTN_WIDTH, ATTN_WIDTH)
PROJ_WIDTH = sum(PROJ_SIZES)

kernel_name = 'hymba_mlstm_dilated_attn_hybrid'


def rms_norm(x, g):
    xf = x.astype(jnp.float32)
    y = xf * lax.rsqrt(jnp.mean(xf * xf, axis=-1, keepdims=True) + EPS)
    return (y * g.astype(jnp.float32)).astype(x.dtype)


def softcap(z):
    return GATE_SOFTCAP * jnp.tanh(z / GATE_SOFTCAP)


def causal_dwconv(u, w, b):
    k = w.shape[0]
    up = jnp.pad(u, ((0, 0), (k - 1, 0), (0, 0)))
    y = lax.conv_general_dilated(up, w[:, None, :].astype(u.dtype), window_strides=(1,), padding='VALID',
                                 dimension_numbers=('NWC', 'WIO', 'NWC'), feature_group_count=u.shape[-1])
    return y + b.astype(u.dtype)


def mlstm_chunkwise(q, k, v, i_pre, f_pre):
    f32 = jnp.float32
    B, S, H, DK = q.shape
    DV = v.shape[-1]
    L = MLSTM_CHUNK
    NC = S // L
    qc = q.astype(f32).reshape(B, NC, L, H, DK).transpose(0, 1, 3, 2, 4) * (DK ** -0.5)
    kc = k.astype(f32).reshape(B, NC, L, H, DK).transpose(0, 1, 3, 2, 4)
    vc = v.astype(f32).reshape(B, NC, L, H, DV).transpose(0, 1, 3, 2, 4)
    log_i = i_pre.astype(f32).reshape(B, NC, L, H).transpose(0, 1, 3, 2)
    log_f = jax.nn.log_sigmoid(f_pre.astype(f32)).reshape(B, NC, L, H).transpose(0, 1, 3, 2)
    bcum = jnp.cumsum(log_f, axis=-1)
    b_last = bcum[..., -1]
    g = b_last[..., None] - bcum + log_i
    m_loc = jnp.max(g, axis=-1)
    w_loc = jnp.exp(g - m_loc[..., None])
    C_loc = jnp.einsum('bchlv,bchlk->bchvk', w_loc[..., None] * vc, kc)
    n_loc = jnp.einsum('bchl,bchlk->bchk', w_loc, kc)

    def step(carry, inp):
        C, n, m = carry
        C_l, n_l, m_l, bL = inp
        m_new = jnp.maximum(bL + m, m_l)
        a = jnp.exp(bL + m - m_new)
        s = jnp.exp(m_l - m_new)
        C_new = a[..., None, None] * C + s[..., None, None] * C_l
        n_new = a[..., None] * n + s[..., None] * n_l
        return (C_new, n_new, m_new), (C, n, m)

    init = (jnp.zeros((B, H, DV, DK), f32), jnp.zeros((B, H, DK), f32), jnp.zeros((B, H), f32))
    xs = (jnp.moveaxis(C_loc, 1, 0), jnp.moveaxis(n_loc, 1, 0), jnp.moveaxis(m_loc, 1, 0), jnp.moveaxis(b_last, 1, 0))
    _, (C_prev, n_prev, m_prev) = lax.scan(step, init, xs)
    C_prev = jnp.moveaxis(C_prev, 0, 1)
    n_prev = jnp.moveaxis(n_prev, 0, 1)
    m_prev = jnp.moveaxis(m_prev, 0, 1)

    causal = jnp.tril(jnp.ones((L, L), dtype=bool))
    D = bcum[..., :, None] - bcum[..., None, :] + log_i[..., None, :]
    D = jnp.where(causal, D, -jnp.inf)
    inter_log = bcum + m_prev[..., None]
    m = jnp.maximum(inter_log, jnp.max(D, axis=-1))
    s_qk = jnp.einsum('bchik,bchjk->bchij', qc, kc) * jnp.exp(D - m[..., None])
    inter_w = jnp.exp(inter_log - m)
    num = jnp.einsum('bchij,bchjv->bchiv', s_qk, vc) + inter_w[..., None] * jnp.einsum('bchik,bchvk->bchiv', qc, C_prev)
    den = jnp.sum(s_qk, axis=-1) + inter_w * jnp.einsum('bchik,bchk->bchi', qc, n_prev)
    h = num / jnp.maximum(jnp.abs(den), jnp.exp(-m))[..., None]
    return h.transpose(0, 1, 3, 2, 4).reshape(B, S, H, DV)


def strided_band_attention(q, k, v, n_back, dilation):
    f32 = jnp.float32
    B, S, H, Dh = q.shape
    Bk = ATTN_BLOCK
    seg = dilation * Bk
    S_pad = ((S + seg - 1) // seg) * seg
    Lsub = S_pad // dilation
    NB = Lsub // Bk
    pad = ((0, 0), (0, S_pad - S), (0, 0), (0, 0))

    def to_sub(a):
        a = jnp.pad(a.astype(f32), pad).reshape(B, Lsub, dilation, H, Dh).transpose(0, 2, 3, 1, 4)
        return a.reshape(B * dilation, H, NB, Bk, Dh)

    qs, ks, vs = to_sub(q), to_sub(k), to_sub(v)
    k_prev = jnp.concatenate([jnp.zeros_like(ks[:, :, :1]), ks[:, :, :-1]], axis=2)
    v_prev = jnp.concatenate([jnp.zeros_like(vs[:, :, :1]), vs[:, :, :-1]], axis=2)
    kk = jnp.concatenate([k_prev, ks], axis=3)
    vv = jnp.concatenate([v_prev, vs], axis=3)
    blk = jnp.arange(NB)[:, None, None]
    q_idx = blk * Bk + jnp.arange(Bk)[None, :, None]
    k_idx = blk * Bk - Bk + jnp.arange(2 * Bk)[None, None, :]
    dist = q_idx - k_idx
    valid = (dist >= 0) & (dist <= n_back) & (k_idx >= 0)
    scores = jnp.einsum('xhnqd,xhnkd->xhnqk', qs, kk) * (Dh ** -0.5)
    scores = jnp.where(valid, scores, -jnp.inf)
    mx = jnp.max(scores, axis=-1, keepdims=True)
    p = jnp.exp(scores - mx)
    l = jnp.sum(p, axis=-1, keepdims=True)
    o = jnp.einsum('xhnqk,xhnkd->xhnqd', p, vv) / l
    lse = (mx + jnp.log(l))[..., 0]
    o = o.reshape(B, dilation, H, Lsub, Dh).transpose(0, 3, 1, 2, 4).reshape(B, S_pad, H, Dh)[:, :S]
    lse = lse.reshape(B, dilation, H, Lsub).transpose(0, 3, 1, 2).reshape(B, S_pad, H)[:, :S]
    return o, lse


def dilated_causal_attention(q, k, v):
    outs, lses = [], []
    for window, dilation in DILATED_PATTERNS:
        o, lse = strided_band_attention(q, k, v, window // dilation, dilation)
        outs.append(o)
        lses.append(lse)
    wts = jax.nn.softmax(jnp.stack(lses, axis=0), axis=0)
    out = jnp.sum(wts[..., None] * jnp.stack(outs, axis=0), axis=0)
    return out.astype(q.dtype)


def hybrid_mixer(h, w_in, conv_w, conv_b, i_bias, f_bias, mlstm_norm, w_out):
    B, S, _ = h.shape
    proj = jnp.einsum('bsd,de->bse', h, w_in)
    cuts = list(np.cumsum(PROJ_SIZES)[:-1])
    q_m, k_m, v_m, o_m, i_m, f_m, q_a, k_a, v_a = jnp.split(proj, cuts, axis=-1)
    qk_m = jax.nn.silu(causal_dwconv(jnp.concatenate([q_m, k_m], axis=-1), conv_w, conv_b))
    q_m, k_m = jnp.split(qk_m, 2, axis=-1)
    i_pre = softcap(i_m + i_bias)
    f_pre = softcap(f_m + f_bias)
    h_m = mlstm_chunkwise(q_m.reshape(B, S, MLSTM_HEADS, MLSTM_DQK), k_m.reshape(B, S, MLSTM_HEADS, MLSTM_DQK),
                          v_m.reshape(B, S, MLSTM_HEADS, MLSTM_DV), i_pre, f_pre)
    h_m = rms_norm(h_m, mlstm_norm.reshape(MLSTM_HEADS, MLSTM_DV)).astype(h.dtype)
    y_m = jax.nn.sigmoid(o_m) * h_m.reshape(B, S, MLSTM_WIDTH)
    y_a = dilated_causal_attention(q_a.reshape(B, S, ATTN_HEADS, ATTN_HEAD_DIM), k_a.reshape(B, S, ATTN_HEADS, ATTN_HEAD_DIM),
                                   v_a.reshape(B, S, ATTN_HEADS, ATTN_HEAD_DIM)).reshape(B, S, ATTN_WIDTH)
    mix = jnp.concatenate([y_m, y_a], axis=-1)
    return jnp.einsum('bse,ed->bsd', mix, w_out)


def sq_relu_mlp(h, w_up, w_down):
    u = jax.nn.relu(jnp.einsum('bsd,df->bsf', h, w_up))
    return jnp.einsum('bsf,fd->bsd', u * u, w_down)


def setup_inputs(seed: int = 0) -> dict:
    key = jax.random.key(seed)
    ks = jax.random.split(key, 16)
    f32 = jnp.float32
    x = jax.random.normal(ks[0], (BATCH, SEQ, D_MODEL), f32)
    norm_mix = 1.0 + 0.02 * jax.random.normal(ks[1], (DEPTH, D_MODEL), f32)
    w_in = jax.random.normal(ks[2], (DEPTH, D_MODEL, PROJ_WIDTH), f32) * (D_MODEL ** -0.5)
    conv_w = jax.random.normal(ks[3], (DEPTH, CONV_WIDTH, 2 * MLSTM_QK_WIDTH), f32) * (CONV_WIDTH ** -0.5)
    conv_b = 0.02 * jax.random.normal(ks[4], (DEPTH, 2 * MLSTM_QK_WIDTH), f32)
    i_bias = 0.1 * jax.random.normal(ks[5], (DEPTH, MLSTM_HEADS), f32)
    f_bias = jnp.linspace(3.0, 6.0, MLSTM_HEADS, dtype=f32)[None, :] + 0.1 * jax.random.normal(ks[6], (DEPTH, MLSTM_HEADS), f32)
    mlstm_norm = 1.0 + 0.02 * jax.random.normal(ks[7], (DEPTH, MLSTM_WIDTH), f32)
    w_out = jax.random.normal(ks[8], (DEPTH, MIX_WIDTH, D_MODEL), f32) * (MIX_WIDTH ** -0.5)
    norm_mlp = 1.0 + 0.02 * jax.random.normal(ks[9], (DEPTH, D_MODEL), f32)
    w_up = jax.random.normal(ks[10], (DEPTH, D_MODEL, D_FF), f32) * (D_MODEL ** -0.5)
    w_down = jax.random.normal(ks[11], (DEPTH, D_FF, D_MODEL), f32) * (D_FF ** -0.5)
    norm_final = 1.0 + 0.02 * jax.random.normal(ks[12], (D_MODEL,), f32)
    return {'x': x, 'norm_mix': norm_mix, 'w_in': w_in, 'conv_w': conv_w, 'conv_b': conv_b, 'i_bias': i_bias,
            'f_bias': f_bias, 'mlstm_norm': mlstm_norm, 'w_out': w_out, 'norm_mlp': norm_mlp, 'w_up': w_up,
            'w_down': w_down, 'norm_final': norm_final}


def reference(x, norm_mix, w_in, conv_w, conv_b, i_bias, f_bias, mlstm_norm, w_out, norm_mlp, w_up, w_down, norm_final):
    for layer in range(DEPTH):
        h = rms_norm(x, norm_mix[layer])
        x = x + hybrid_mixer(h, w_in[layer], conv_w[layer], conv_b[layer], i_bias[layer], f_bias[layer],
                             mlstm_norm[layer], w_out[layer])
        h = rms_norm(x, norm_mlp[layer])
        x = x + sq_relu_mlp(h, w_up[layer], w_down[layer])
    return rms_norm(x, norm_final)
```

```python
import functools

import jax
import jax.numpy as jnp
from jax import lax
from jax.experimental import pallas as pl
from jax.experimental.pallas import tpu as pltpu

F32 = jnp.float32
BF16 = jnp.bfloat16

D_MODEL = 4096
MLSTM_HEADS = 4
MLSTM_DV = 512
MLSTM_DQK = 256
MLSTM_QK_WIDTH = MLSTM_HEADS * MLSTM_DQK
MLSTM_WIDTH = MLSTM_HEADS * MLSTM_DV
CONV_WIDTH = 4
GATE_SOFTCAP = 15.0
ATTN_HEAD_DIM = 128
ATTN_HEADS = 16
ATTN_WIDTH = ATTN_HEADS * ATTN_HEAD_DIM
ATTN_BLOCK = 128
DILATIONS = (1, 4, 16)
D_FF = 4 * D_MODEL
EPS = 1e-6

LANES = 128
SUBLANES = 8
VMEM_LIMIT_BYTES = 56 * 1024 * 1024

MM_TILE = 1024
MM_K_TILE = 2048
NORM_ROWS = 256
MLSTM_CHUNK = 256
ATTN_SUPER = DILATIONS[-1] * ATTN_BLOCK


def _params(*sem):
    return pltpu.CompilerParams(dimension_semantics=sem, vmem_limit_bytes=VMEM_LIMIT_BYTES)


def _rmsnorm_kernel(x_ref, g_ref, o_ref):
    x = x_ref[...]
    ms = jnp.mean(x * x, axis=-1, keepdims=True)
    o_ref[...] = (x * lax.rsqrt(ms + EPS) * g_ref[...]).astype(o_ref.dtype)


def _rmsnorm(x, g, out_dtype):
    s, d = x.shape
    return pl.pallas_call(
        _rmsnorm_kernel,
        grid=(s // NORM_ROWS,),
        in_specs=[pl.BlockSpec((NORM_ROWS, d), lambda i: (i, 0)),
                  pl.BlockSpec((1, d), lambda i: (0, 0))],
        out_specs=pl.BlockSpec((NORM_ROWS, d), lambda i: (i, 0)),
        out_shape=jax.ShapeDtypeStruct((s, d), out_dtype),
        compiler_params=_params("parallel"),
        name="rmsnorm",
    )(x, g.reshape(1, d))


def _mm_kernel(a_ref, b_ref, o_ref, *, relu2):
    acc = jnp.dot(a_ref[...], b_ref[...], preferred_element_type=F32)
    if relu2:
        acc = jnp.maximum(acc, 0.0)
        acc = acc * acc
    o_ref[...] = acc.astype(o_ref.dtype)


def _matmul(a, b, out_dtype, *, relu2=False, tn=MM_TILE, name="matmul"):
    m, k = a.shape
    n = b.shape[1]
    tm = MM_TILE
    return pl.pallas_call(
        functools.partial(_mm_kernel, relu2=relu2),
        grid=(m // tm, n // tn),
        in_specs=[pl.BlockSpec((tm, k), lambda i, j: (i, 0)),
                  pl.BlockSpec((k, tn), lambda i, j: (0, j))],
        out_specs=pl.BlockSpec((tm, tn), lambda i, j: (i, j)),
        out_shape=jax.ShapeDtypeStruct((m, n), out_dtype),
        compiler_params=_params("parallel", "parallel"),
        name=name,
    )(a, b)


def _mm2_res_kernel(a0_ref, a1_ref, b0_ref, b1_ref, r_ref, o_ref):
    acc = jnp.dot(a0_ref[...], b0_ref[...], preferred_element_type=F32)
    acc = acc + jnp.dot(a1_ref[...], b1_ref[...], preferred_element_type=F32)
    o_ref[...] = r_ref[...] + acc


def _out_proj(y_m, y_a, w, res):
    m, k0 = y_m.shape
    k1 = y_a.shape[1]
    assert k0 == k1
    n = w.shape[1]
    tm = tn = MM_TILE
    return pl.pallas_call(
        _mm2_res_kernel,
        grid=(m // tm, n // tn),
        in_specs=[pl.BlockSpec((tm, k0), lambda i, j: (i, 0)),
                  pl.BlockSpec((tm, k1), lambda i, j: (i, 0)),
                  pl.BlockSpec((k0, tn), lambda i, j: (0, j)),
                  pl.BlockSpec((k1, tn), lambda i, j: (1, j)),
                  pl.BlockSpec((tm, tn), lambda i, j: (i, j))],
        out_specs=pl.BlockSpec((tm, tn), lambda i, j: (i, j)),
        out_shape=jax.ShapeDtypeStruct((m, n), F32),
        compiler_params=_params("parallel", "parallel"),
        name="out_proj",
    )(y_m, y_a, w, w, res)


def _mmk_res_kernel(a_ref, b_ref, r_ref, o_ref):
    @pl.when(pl.program_id(2) == 0)
    def _():
        o_ref[...] = r_ref[...]

    o_ref[...] += jnp.dot(a_ref[...], b_ref[...], preferred_element_type=F32)


def _matmul_ktiled_res(a, b, res):
    m, k = a.shape
    n = b.shape[1]
    tm = tn = MM_TILE
    tk = MM_K_TILE
    return pl.pallas_call(
        _mmk_res_kernel,
        grid=(m // tm, n // tn, k // tk),
        in_specs=[pl.BlockSpec((tm, tk), lambda i, j, kk: (i, kk)),
                  pl.BlockSpec((tk, tn), lambda i, j, kk: (kk, j)),
                  pl.BlockSpec((tm, tn), lambda i, j, kk: (i, j))],
        out_specs=pl.BlockSpec((tm, tn), lambda i, j, kk: (i, j)),
        out_shape=jax.ShapeDtypeStruct((m, n), F32),
        compiler_params=_params("parallel", "parallel", "arbitrary"),
        name="down_proj",
    )(a, b, res)


def _split3(a):
    hi = a.astype(BF16)
    r1 = a - hi.astype(F32)
    mid = r1.astype(BF16)
    lo = (r1 - mid.astype(F32)).astype(BF16)
    return hi, mid, lo


def _conv_silu(u_ref, tail_ref, ext_ref, w_ref, b_ref):
    chunk = u_ref.shape[0]
    u = u_ref[...]
    ext_ref[0:SUBLANES, :] = tail_ref[...]
    ext_ref[SUBLANES:SUBLANES + chunk, :] = u
    tail_ref[...] = u[chunk - SUBLANES:, :]
    w = w_ref[...]
    y = b_ref[...] + w[CONV_WIDTH - 1:CONV_WIDTH, :] * u
    for j in range(CONV_WIDTH - 1):
        start = SUBLANES - (CONV_WIDTH - 1) + j
        y = y + w[j:j + 1, :] * ext_ref[start:start + chunk, :]
    return y * (1.0 / (1.0 + jnp.exp(-y)))


def _mlstm_kernel(q_ref, k_ref, v_ref, o_ref, g_ref, gb_ref, cwq_ref, cwk_ref, cbq_ref, cbk_ref, nrm_ref,
                  y_ref, c_s, n_s, qtail_s, ktail_s, ext_s):
    chunk = q_ref.shape[0]

    @pl.when(pl.program_id(1) == 0)
    def _():
        c_s[...] = jnp.zeros_like(c_s)
        n_s[...] = jnp.zeros_like(n_s)
        qtail_s[...] = jnp.zeros_like(qtail_s)
        ktail_s[...] = jnp.zeros_like(ktail_s)

    q = _conv_silu(q_ref, qtail_s, ext_s, cwq_ref, cbq_ref) * (MLSTM_DQK ** -0.5)
    k = _conv_silu(k_ref, ktail_s, ext_s, cwk_ref, cbk_ref)
    qb = q.astype(BF16)
    v = v_ref[...]

    pre = GATE_SOFTCAP * jnp.tanh((g_ref[...] + gb_ref[...]) * (1.0 / GATE_SOFTCAP))
    logsig = jnp.minimum(pre, 0.0) - jnp.log(1.0 + jnp.exp(-jnp.abs(pre)))
    row = lax.broadcasted_iota(jnp.int32, (chunk, chunk), 0)
    col = lax.broadcasted_iota(jnp.int32, (chunk, chunk), 1)
    causal = row >= col
    tri = causal.astype(BF16)
    cum = sum(jnp.dot(tri, piece, preferred_element_type=F32) for piece in _split3(logsig))
    li_col = pre[:, 0:1]
    b_col = cum[:, 1:2]
    b_last = cum[chunk - 1:chunk, 1:2]
    r_col = li_col - b_col
    r_row = pre.T[0:1, :] - cum.T[1:2, :]

    dmat = jnp.where(causal, b_col + r_row, -jnp.inf)
    s_qk = lax.dot_general(qb, k.astype(BF16), (((1,), (1,)), ((), ())), preferred_element_type=F32)
    p = s_qk * jnp.exp(dmat)
    inter_w = jnp.exp(b_col)
    num = jnp.dot(p.astype(BF16), v, preferred_element_type=F32)
    num = num + inter_w * jnp.dot(qb, c_s[...].astype(BF16), preferred_element_type=F32)
    den = jnp.sum(p, axis=-1, keepdims=True) + inter_w * jnp.sum(q * n_s[...], axis=-1, keepdims=True)
    h = num / jnp.maximum(jnp.abs(den), 1.0)

    kw = k * jnp.exp(b_last + r_col)
    decay = jnp.exp(b_last)
    c_s[...] = decay * c_s[...] + lax.dot_general(kw.astype(BF16), v, (((0,), (0,)), ((), ())),
                                                  preferred_element_type=F32)
    n_s[...] = decay * n_s[...] + jnp.sum(kw, axis=0, keepdims=True)

    hn = h * lax.rsqrt(jnp.mean(h * h, axis=-1, keepdims=True) + EPS) * nrm_ref[...]
    gate = 1.0 / (1.0 + jnp.exp(-o_ref[...].astype(F32)))
    y_ref[...] = (gate * hn).astype(y_ref.dtype)


def _mlstm(pf, pb, gates, gate_bias, conv_w, conv_b, mlstm_norm):
    s = pf.shape[0]
    L = MLSTM_CHUNK
    H = MLSTM_HEADS
    dk, dv = MLSTM_DQK, MLSTM_DV
    return pl.pallas_call(
        _mlstm_kernel,
        grid=(H, s // L),
        in_specs=[pl.BlockSpec((L, dk), lambda h, c: (c, h)),
                  pl.BlockSpec((L, dk), lambda h, c: (c, H + h)),
                  pl.BlockSpec((L, dv), lambda h, c: (c, h)),
                  pl.BlockSpec((L, dv), lambda h, c: (c, H + h)),
                  pl.BlockSpec((L, LANES), lambda h, c: (c, h)),
                  pl.BlockSpec((1, LANES), lambda h, c: (0, h)),
                  pl.BlockSpec((CONV_WIDTH, dk), lambda h, c: (0, h)),
                  pl.BlockSpec((CONV_WIDTH, dk), lambda h, c: (0, H + h)),
                  pl.BlockSpec((1, dk), lambda h, c: (0, h)),
                  pl.BlockSpec((1, dk), lambda h, c: (0, H + h)),
                  pl.BlockSpec((1, dv), lambda h, c: (0, h))],
        out_specs=pl.BlockSpec((L, dv), lambda h, c: (c, h)),
        out_shape=jax.ShapeDtypeStruct((s, MLSTM_WIDTH), BF16),
        scratch_shapes=[pltpu.VMEM((dk, dv), F32),
                        pltpu.VMEM((1, dk), F32),
                        pltpu.VMEM((SUBLANES, dk), F32),
                        pltpu.VMEM((SUBLANES, dk), F32),
                        pltpu.VMEM((SUBLANES + L, dk), F32)],
        compiler_params=_params("parallel", "arbitrary"),
        name="mlstm",
    )(pf, pf, pb, pb, gates, gate_bias, conv_w, conv_w, conv_b, conv_b, mlstm_norm)


def _band_softmax(q, k, v, bias):
    s = lax.dot_general(q, k, (((1,), (1,)), ((), ())), preferred_element_type=F32) + bias
    m = jnp.max(s, axis=-1, keepdims=True)
    p = jnp.exp(s - m)
    l = jnp.sum(p, axis=-1, keepdims=True)
    o = jnp.dot(p.astype(BF16), v, preferred_element_type=F32) / l
    return o, m + jnp.log(l)


def _attn_kernel(q_ref, kp_ref, kc_ref, vp_ref, vc_ref, y_ref, kk_s, vv_s, o_s, lse_s):
    blk = ATTN_BLOCK
    sup = q_ref.shape[0]
    nblk = sup // blk
    sb = pl.program_id(1)

    kk_s[0:sup, :] = kp_ref[...]
    kk_s[sup:2 * sup, :] = kc_ref[...]
    vv_s[0:sup, :] = vp_ref[...]
    vv_s[sup:2 * sup, :] = vc_ref[...]

    qi = lax.broadcasted_iota(jnp.int32, (blk, 2 * blk), 0)
    kj = lax.broadcasted_iota(jnp.int32, (blk, 2 * blk), 1)
    band = (kj >= qi) & (kj <= qi + blk)
    bias_any = jnp.where(band, 0.0, -jnp.inf).astype(F32)
    bias_first = jnp.where(band & (kj >= blk), 0.0, -jnp.inf).astype(F32)
    scale = ATTN_HEAD_DIM ** -0.5
    seg = DILATIONS[1] * blk

    def body(t, carry):
        def run(p_idx, q_rows, k_rows, first):
            q = (q_ref[q_rows, :] * scale).astype(BF16)
            k = kk_s[k_rows, :].astype(BF16)
            v = vv_s[k_rows, :].astype(BF16)
            bias = jnp.where(first, bias_first, bias_any)
            o, lse = _band_softmax(q, k, v, bias)
            o_s[p_idx, q_rows, :] = o
            lse_s[p_idx, q_rows, :] = jnp.broadcast_to(lse, (blk, LANES))

        q0 = pl.multiple_of(t * blk, blk)
        run(0, pl.ds(q0, blk), pl.ds(sup + q0 - blk, 2 * blk), (sb == 0) & (t == 0))
        d1 = DILATIONS[1]
        sg = lax.shift_right_logical(t, d1.bit_length() - 1)
        r1 = t & (d1 - 1)
        run(1, pl.ds(sg * seg + r1, blk, stride=d1), pl.ds(sup + (sg - 1) * seg + r1, 2 * blk, stride=d1),
            (sb == 0) & (sg == 0))
        d2 = DILATIONS[2]
        run(2, pl.ds(t, blk, stride=d2), pl.ds(t, 2 * blk, stride=d2), sb == 0)
        return carry

    lax.fori_loop(0, nblk, body, 0)

    def combine(t, carry):
        rows = pl.ds(pl.multiple_of(t * blk, blk), blk)
        l0, l1, l2 = lse_s[0, rows, :], lse_s[1, rows, :], lse_s[2, rows, :]
        mx = jnp.maximum(jnp.maximum(l0, l1), l2)
        w0, w1, w2 = jnp.exp(l0 - mx), jnp.exp(l1 - mx), jnp.exp(l2 - mx)
        out = (w0 * o_s[0, rows, :] + w1 * o_s[1, rows, :] + w2 * o_s[2, rows, :]) / (w0 + w1 + w2)
        y_ref[rows, :] = out.astype(y_ref.dtype)
        return carry

    lax.fori_loop(0, nblk, combine, 0)


def _dilated_attention(pf, col0):
    s = pf.shape[0]
    sup = ATTN_SUPER
    hb = ATTN_HEADS
    qb, kb, vb = (col0 // LANES + i * hb for i in range(3))
    prev = lambda b: jnp.maximum(b - 1, 0)
    return pl.pallas_call(
        _attn_kernel,
        grid=(hb, s // sup),
        in_specs=[pl.BlockSpec((sup, LANES), lambda h, b: (b, qb + h)),
                  pl.BlockSpec((sup, LANES), lambda h, b: (prev(b), kb + h)),
                  pl.BlockSpec((sup, LANES), lambda h, b: (b, kb + h)),
                  pl.BlockSpec((sup, LANES), lambda h, b: (prev(b), vb + h)),
                  pl.BlockSpec((sup, LANES), lambda h, b: (b, vb + h))],
        out_specs=pl.BlockSpec((sup, LANES), lambda h, b: (b, h)),
        out_shape=jax.ShapeDtypeStruct((s, ATTN_WIDTH), BF16),
        scratch_shapes=[pltpu.VMEM((2 * sup, LANES), F32),
                        pltpu.VMEM((2 * sup, LANES), F32),
                        pltpu.VMEM((len(DILATIONS), sup, LANES), F32),
                        pltpu.VMEM((len(DILATIONS), sup, LANES), F32)],
        compiler_params=_params("parallel", "parallel"),
        name="dilated_attn",
    )(pf, pf, pf, pf, pf)


def _gate_weights(w_in_l, i_bias_l, f_bias_l):
    g0 = 2 * MLSTM_QK_WIDTH + 2 * MLSTM_WIDTH
    H = MLSTM_HEADS
    wi = w_in_l[:, g0:g0 + H]
    wf = w_in_l[:, g0 + H:g0 + 2 * H]
    wg = jnp.stack([wi, wf], axis=-1)
    wg = jnp.pad(wg, ((0, 0), (0, 0), (0, LANES - 2))).reshape(w_in_l.shape[0], H * LANES)
    bg = jnp.stack([i_bias_l, f_bias_l], axis=-1)
    bg = jnp.pad(bg, ((0, 0), (0, LANES - 2))).reshape(1, H * LANES)
    return wg.astype(BF16), bg.astype(F32)


def _layer(x, norm_mix, w_in, conv_w, conv_b, i_bias, f_bias, mlstm_norm, w_out, norm_mlp, w_up, w_down):
    qk_w = 2 * MLSTM_QK_WIDTH
    g0 = qk_w + 2 * MLSTM_WIDTH
    a0 = g0 + 2 * MLSTM_HEADS
    w_f = jnp.concatenate([w_in[:, :qk_w], w_in[:, a0:]], axis=1).astype(BF16)
    w_b = w_in[:, qk_w:g0].astype(BF16)
    w_g, b_g = _gate_weights(w_in, i_bias, f_bias)

    h = _rmsnorm(x, norm_mix, BF16)
    pf = _matmul(h, w_f, F32, name="in_proj_f32")
    pb = _matmul(h, w_b, BF16, name="in_proj_bf16")
    gates = _matmul(h, w_g, F32, tn=w_g.shape[1], name="in_proj_gates")
    y_m = _mlstm(pf, pb, gates, b_g, conv_w, conv_b.reshape(1, -1), mlstm_norm.reshape(1, -1))
    y_a = _dilated_attention(pf, qk_w)
    x1 = _out_proj(y_m, y_a, w_out.astype(BF16), x)
    h2 = _rmsnorm(x1, norm_mlp, BF16)
    u = _matmul(h2, w_up.astype(BF16), BF16, relu2=True, name="up_proj")
    return _matmul_ktiled_res(u, w_down.astype(BF16), x1)


def kernel(x, norm_mix, w_in, conv_w, conv_b, i_bias, f_bias, mlstm_norm, w_out, norm_mlp, w_up, w_down, norm_final):
    b, s, d = x.shape
    outs = []
    for bi in range(b):
        xb = x[bi]
        for layer in range(norm_mix.shape[0]):
            xb = _layer(xb, norm_mix[layer], w_in[layer], conv_w[layer], conv_b[layer], i_bias[layer], f_bias[layer],
                        mlstm_norm[layer], w_out[layer], norm_mlp[layer], w_up[layer], w_down[layer])
        outs.append(_rmsnorm(xb, norm_final, x.dtype))
    return jnp.stack(outs, axis=0)
```

```python
import functools

import jax
import jax.numpy as jnp
from jax import lax
from jax.experimental import pallas as pl
from jax.experimental.pallas import tpu as pltpu

F32 = jnp.float32
BF16 = jnp.bfloat16

D_MODEL = 4096
MLSTM_HEADS = 4
MLSTM_DV = 512
MLSTM_DQK = 256
MLSTM_QK_WIDTH = MLSTM_HEADS * MLSTM_DQK
MLSTM_WIDTH = MLSTM_HEADS * MLSTM_DV
CONV_WIDTH = 4
GATE_SOFTCAP = 15.0
ATTN_HEAD_DIM = 128
ATTN_HEADS = 16
ATTN_WIDTH = ATTN_HEADS * ATTN_HEAD_DIM
ATTN_BLOCK = 128
DILATIONS = (1, 4, 16)
D_FF = 4 * D_MODEL
EPS = 1e-6

LANES = 128
SUBLANES = 8
VMEM_LIMIT_BYTES = 56 * 1024 * 1024

MM_TILE = 1024
MM_K_TILE = 4096
NORM_ROWS = 256
MLSTM_CHUNK = 256
ATTN_SUPER = DILATIONS[-1] * ATTN_BLOCK
ATTN_UNROLL = 4


def _params(*sem):
    return pltpu.CompilerParams(dimension_semantics=sem, vmem_limit_bytes=VMEM_LIMIT_BYTES)


def _rmsnorm_kernel(x_ref, g_ref, o_ref):
    x = x_ref[...]
    ms = jnp.mean(x * x, axis=-1, keepdims=True)
    o_ref[...] = (x * lax.rsqrt(ms + EPS) * g_ref[...]).astype(o_ref.dtype)


def _rmsnorm(x, g, out_dtype):
    s, d = x.shape
    return pl.pallas_call(
        _rmsnorm_kernel,
        grid=(s // NORM_ROWS,),
        in_specs=[pl.BlockSpec((NORM_ROWS, d), lambda i: (i, 0)),
                  pl.BlockSpec((1, d), lambda i: (0, 0))],
        out_specs=pl.BlockSpec((NORM_ROWS, d), lambda i: (i, 0)),
        out_shape=jax.ShapeDtypeStruct((s, d), out_dtype),
        compiler_params=_params("parallel"),
        name="rmsnorm",
    )(x, g.reshape(1, d))


def _cast_kernel(w_ref, o_ref):
    o_ref[...] = w_ref[...].astype(o_ref.dtype)


def _cast_cols_bf16(w, ncols):
    k = w.shape[0]
    rows = NORM_ROWS
    return pl.pallas_call(
        _cast_kernel,
        grid=(k // rows,),
        in_specs=[pl.BlockSpec((rows, ncols), lambda i: (i, 0))],
        out_specs=pl.BlockSpec((rows, ncols), lambda i: (i, 0)),
        out_shape=jax.ShapeDtypeStruct((k, ncols), BF16),
        compiler_params=_params("parallel"),
        name="cast_w_in",
    )(w)


def _side_cast(w, grid):
    steps = 1
    for g in grid:
        steps *= g
    rows = w.shape[0] // steps
    assert rows * steps == w.shape[0] and rows % (2 * SUBLANES) == 0

    def index_map(*ids):
        lin = ids[0]
        for g, i in zip(grid[1:], ids[1:]):
            lin = lin * g + i
        return (lin, 0)

    spec = pl.BlockSpec((rows, w.shape[1]), index_map)
    return spec, jax.ShapeDtypeStruct(w.shape, BF16)


def _mm_kernel(a_ref, b_ref, *refs, relu2, side_cast):
    if side_cast:
        wsrc_ref, o_ref, wdst_ref = refs
        wdst_ref[...] = wsrc_ref[...].astype(wdst_ref.dtype)
    else:
        (o_ref,) = refs
    acc = jnp.dot(a_ref[...], b_ref[...], preferred_element_type=F32)
    if relu2:
        acc = jnp.maximum(acc, 0.0)
        acc = acc * acc
    o_ref[...] = acc.astype(o_ref.dtype)


def _matmul(a, b, out_dtype, *, n=None, col0=0, relu2=False, tn=MM_TILE, cast_w=None, name="matmul"):
    m, k = a.shape
    n = b.shape[1] if n is None else n
    tm = MM_TILE
    jb = col0 // tn
    assert jb * tn == col0
    grid = (m // tm, n // tn)
    in_specs = [pl.BlockSpec((tm, k), lambda i, j: (i, 0)),
                pl.BlockSpec((k, tn), lambda i, j: (0, jb + j))]
    out_specs = pl.BlockSpec((tm, tn), lambda i, j: (i, j))
    out_shape = jax.ShapeDtypeStruct((m, n), out_dtype)
    args = [a, b]
    if cast_w is not None:
        cspec, cshape = _side_cast(cast_w, grid)
        in_specs.append(cspec)
        out_specs, out_shape = [out_specs, cspec], [out_shape, cshape]
        args.append(cast_w)
    return pl.pallas_call(
        functools.partial(_mm_kernel, relu2=relu2, side_cast=cast_w is not None),
        grid=grid, in_specs=in_specs, out_specs=out_specs, out_shape=out_shape,
        compiler_params=_params("parallel", "parallel"),
        name=name,
    )(*args)


def _mm2_res_kernel(a0_ref, a1_ref, b0_ref, b1_ref, r_ref, o_ref):
    acc = jnp.dot(a0_ref[...], b0_ref[...], preferred_element_type=F32)
    acc = acc + jnp.dot(a1_ref[...], b1_ref[...], preferred_element_type=F32)
    o_ref[...] = r_ref[...] + acc


def _out_proj(y_m, y_a, w, res):
    m, k0 = y_m.shape
    k1 = y_a.shape[1]
    assert k0 == k1
    n = w.shape[1]
    tm = tn = MM_TILE
    return pl.pallas_call(
        _mm2_res_kernel,
        grid=(m // tm, n // tn),
        in_specs=[pl.BlockSpec((tm, k0), lambda i, j: (i, 0)),
                  pl.BlockSpec((tm, k1), lambda i, j: (i, 0)),
                  pl.BlockSpec((k0, tn), lambda i, j: (0, j)),
                  pl.BlockSpec((k1, tn), lambda i, j: (1, j)),
                  pl.BlockSpec((tm, tn), lambda i, j: (i, j))],
        out_specs=pl.BlockSpec((tm, tn), lambda i, j: (i, j)),
        out_shape=jax.ShapeDtypeStruct((m, n), F32),
        compiler_params=_params("parallel", "parallel"),
        name="out_proj",
    )(y_m, y_a, w, w, res)


def _mmk_res_kernel(a_ref, b_ref, r_ref, o_ref):
    @pl.when(pl.program_id(2) == 0)
    def _():
        o_ref[...] = r_ref[...]

    o_ref[...] += jnp.dot(a_ref[...], b_ref[...], preferred_element_type=F32)


def _matmul_ktiled_res(a, b, res):
    m, k = a.shape
    n = b.shape[1]
    tm = tn = MM_TILE
    tk = MM_K_TILE
    return pl.pallas_call(
        _mmk_res_kernel,
        grid=(m // tm, n // tn, k // tk),
        in_specs=[pl.BlockSpec((tm, tk), lambda i, j, kk: (i, kk)),
                  pl.BlockSpec((tk, tn), lambda i, j, kk: (kk, j)),
                  pl.BlockSpec((tm, tn), lambda i, j, kk: (i, j))],
        out_specs=pl.BlockSpec((tm, tn), lambda i, j, kk: (i, j)),
        out_shape=jax.ShapeDtypeStruct((m, n), F32),
        compiler_params=_params("parallel", "parallel", "arbitrary"),
        name="down_proj",
    )(a, b, res)


def _split3(a):
    hi = a.astype(BF16)
    r1 = a - hi.astype(F32)
    mid = r1.astype(BF16)
    lo = (r1 - mid.astype(F32)).astype(BF16)
    return hi, mid, lo


def _conv_silu(u_ref, tail_ref, ext_ref, w_ref, b_ref):
    chunk = u_ref.shape[0]
    u = u_ref[...]
    ext_ref[0:SUBLANES, :] = tail_ref[...]
    ext_ref[SUBLANES:SUBLANES + chunk, :] = u
    tail_ref[...] = u[chunk - SUBLANES:, :]
    w = w_ref[...]
    y = b_ref[...] + w[CONV_WIDTH - 1:CONV_WIDTH, :] * u
    for j in range(CONV_WIDTH - 1):
        start = SUBLANES - (CONV_WIDTH - 1) + j
        y = y + w[j:j + 1, :] * ext_ref[start:start + chunk, :]
    return y * (1.0 / (1.0 + jnp.exp(-y)))


def _mlstm_kernel(q_ref, k_ref, v_ref, o_ref, g_ref, gb_ref, cwq_ref, cwk_ref, cbq_ref, cbk_ref, nrm_ref, wsrc_ref,
                  y_ref, wdst_ref, c_s, n_s, qtail_s, ktail_s, ext_s):
    chunk = q_ref.shape[0]
    wdst_ref[...] = wsrc_ref[...].astype(wdst_ref.dtype)

    @pl.when(pl.program_id(1) == 0)
    def _():
        c_s[...] = jnp.zeros_like(c_s)
        n_s[...] = jnp.zeros_like(n_s)
        qtail_s[...] = jnp.zeros_like(qtail_s)
        ktail_s[...] = jnp.zeros_like(ktail_s)

    q = _conv_silu(q_ref, qtail_s, ext_s, cwq_ref, cbq_ref) * (MLSTM_DQK ** -0.5)
    k = _conv_silu(k_ref, ktail_s, ext_s, cwk_ref, cbk_ref)
    qb = q.astype(BF16)
    v = v_ref[...]

    pre = GATE_SOFTCAP * jnp.tanh((g_ref[...] + gb_ref[...]) * (1.0 / GATE_SOFTCAP))
    logsig = jnp.minimum(pre, 0.0) - jnp.log(1.0 + jnp.exp(-jnp.abs(pre)))
    row = lax.broadcasted_iota(jnp.int32, (chunk, chunk), 0)
    col = lax.broadcasted_iota(jnp.int32, (chunk, chunk), 1)
    causal = row >= col
    tri = causal.astype(BF16)
    cum = sum(jnp.dot(tri, piece, preferred_element_type=F32) for piece in _split3(logsig))
    li_col = pre[:, 0:1]
    b_col = cum[:, 1:2]
    b_last = cum[chunk - 1:chunk, 1:2]
    r_col = li_col - b_col
    r_row = pre.T[0:1, :] - cum.T[1:2, :]

    dmat = jnp.where(causal, b_col + r_row, -jnp.inf)
    s_qk = lax.dot_general(qb, k.astype(BF16), (((1,), (1,)), ((), ())), preferred_element_type=F32)
    p = s_qk * jnp.exp(dmat)
    inter_w = jnp.exp(b_col)
    num = jnp.dot(p.astype(BF16), v, preferred_element_type=F32)
    num = num + inter_w * jnp.dot(qb, c_s[...].astype(BF16), preferred_element_type=F32)
    den = jnp.sum(p, axis=-1, keepdims=True) + inter_w * jnp.sum(q * n_s[...], axis=-1, keepdims=True)
    h = num / jnp.maximum(jnp.abs(den), 1.0)

    kw = k * jnp.exp(b_last + r_col)
    decay = jnp.exp(b_last)
    c_s[...] = decay * c_s[...] + lax.dot_general(kw.astype(BF16), v, (((0,), (0,)), ((), ())),
                                                  preferred_element_type=F32)
    n_s[...] = decay * n_s[...] + jnp.sum(kw, axis=0, keepdims=True)

    hn = h * lax.rsqrt(jnp.mean(h * h, axis=-1, keepdims=True) + EPS) * nrm_ref[...]
    gate = 1.0 / (1.0 + jnp.exp(-o_ref[...].astype(F32)))
    y_ref[...] = (gate * hn).astype(y_ref.dtype)


def _mlstm(pf, pb, gates, gate_bias, conv_w, conv_b, mlstm_norm, cast_w):
    s = pf.shape[0]
    L = MLSTM_CHUNK
    H = MLSTM_HEADS
    dk, dv = MLSTM_DQK, MLSTM_DV
    grid = (H, s // L)
    cspec, cshape = _side_cast(cast_w, grid)
    return pl.pallas_call(
        _mlstm_kernel,
        grid=grid,
        in_specs=[pl.BlockSpec((L, dk), lambda h, c: (c, h)),
                  pl.BlockSpec((L, dk), lambda h, c: (c, H + h)),
                  pl.BlockSpec((L, dv), lambda h, c: (c, h)),
                  pl.BlockSpec((L, dv), lambda h, c: (c, H + h)),
                  pl.BlockSpec((L, LANES), lambda h, c: (c, h)),
                  pl.BlockSpec((1, LANES), lambda h, c: (0, h)),
                  pl.BlockSpec((CONV_WIDTH, dk), lambda h, c: (0, h)),
                  pl.BlockSpec((CONV_WIDTH, dk), lambda h, c: (0, H + h)),
                  pl.BlockSpec((1, dk), lambda h, c: (0, h)),
                  pl.BlockSpec((1, dk), lambda h, c: (0, H + h)),
                  pl.BlockSpec((1, dv), lambda h, c: (0, h)),
                  cspec],
        out_specs=[pl.BlockSpec((L, dv), lambda h, c: (c, h)), cspec],
        out_shape=[jax.ShapeDtypeStruct((s, MLSTM_WIDTH), BF16), cshape],
        scratch_shapes=[pltpu.VMEM((dk, dv), F32),
                        pltpu.VMEM((1, dk), F32),
                        pltpu.VMEM((SUBLANES, dk), F32),
                        pltpu.VMEM((SUBLANES, dk), F32),
                        pltpu.VMEM((SUBLANES + L, dk), F32)],
        compiler_params=_params("parallel", "arbitrary"),
        name="mlstm",
    )(pf, pf, pb, pb, gates, gate_bias, conv_w, conv_w, conv_b, conv_b, mlstm_norm, cast_w)


def _band_softmax(q, k, v, bias):
    s = lax.dot_general(q, k, (((1,), (1,)), ((), ())), preferred_element_type=F32) + bias
    m = jnp.max(s, axis=-1, keepdims=True)
    p = jnp.exp(s - m)
    l = jnp.sum(p, axis=-1, keepdims=True)
    o = jnp.dot(p.astype(BF16), v, preferred_element_type=F32) / l
    return o, m + jnp.log(l)


def _attn_kernel(q_ref, kp_ref, kc_ref, vp_ref, vc_ref, wsrc_ref, y_ref, wdst_ref, kk_s, vv_s, o_s, lse_s):
    wdst_ref[...] = wsrc_ref[...].astype(wdst_ref.dtype)
    blk = ATTN_BLOCK
    sup = q_ref.shape[0]
    nblk = sup // blk
    sb = pl.program_id(1)

    kk_s[0:sup, :] = kp_ref[...]
    kk_s[sup:2 * sup, :] = kc_ref[...]
    vv_s[0:sup, :] = vp_ref[...]
    vv_s[sup:2 * sup, :] = vc_ref[...]

    qi = lax.broadcasted_iota(jnp.int32, (blk, 2 * blk), 0)
    kj = lax.broadcasted_iota(jnp.int32, (blk, 2 * blk), 1)
    band = (kj >= qi) & (kj <= qi + blk)
    bias_any = jnp.where(band, 0.0, -jnp.inf).astype(F32)
    bias_first = jnp.where(band & (kj >= blk), 0.0, -jnp.inf).astype(F32)
    scale = ATTN_HEAD_DIM ** -0.5
    seg = DILATIONS[1] * blk

    def body(t, carry):
        def run(p_idx, q_rows, k_rows, first):
            q = (q_ref[q_rows, :] * scale).astype(BF16)
            k = kk_s[k_rows, :].astype(BF16)
            v = vv_s[k_rows, :].astype(BF16)
            bias = jnp.where(first, bias_first, bias_any)
            o, lse = _band_softmax(q, k, v, bias)
            o_s[p_idx, q_rows, :] = o
            lse_s[p_idx, q_rows, :] = jnp.broadcast_to(lse, (blk, LANES))

        q0 = pl.multiple_of(t * blk, blk)
        run(0, pl.ds(q0, blk), pl.ds(sup + q0 - blk, 2 * blk), (sb == 0) & (t == 0))
        d1 = DILATIONS[1]
        sg = lax.shift_right_logical(t, d1.bit_length() - 1)
        r1 = t & (d1 - 1)
        run(1, pl.ds(sg * seg + r1, blk, stride=d1), pl.ds(sup + (sg - 1) * seg + r1, 2 * blk, stride=d1),
            (sb == 0) & (sg == 0))
        d2 = DILATIONS[2]
        run(2, pl.ds(t, blk, stride=d2), pl.ds(t, 2 * blk, stride=d2), sb == 0)
        return carry

    lax.fori_loop(0, nblk, body, 0, unroll=ATTN_UNROLL)

    def combine(t, carry):
        rows = pl.ds(pl.multiple_of(t * blk, blk), blk)
        l0, l1, l2 = lse_s[0, rows, :], lse_s[1, rows, :], lse_s[2, rows, :]
        mx = jnp.maximum(jnp.maximum(l0, l1), l2)
        w0, w1, w2 = jnp.exp(l0 - mx), jnp.exp(l1 - mx), jnp.exp(l2 - mx)
        out = (w0 * o_s[0, rows, :] + w1 * o_s[1, rows, :] + w2 * o_s[2, rows, :]) / (w0 + w1 + w2)
        y_ref[rows, :] = out.astype(y_ref.dtype)
        return carry

    lax.fori_loop(0, nblk, combine, 0)


def _dilated_attention(pf, col0, cast_w):
    s = pf.shape[0]
    sup = ATTN_SUPER
    hb = ATTN_HEADS
    qb, kb, vb = (col0 // LANES + i * hb for i in range(3))
    prev = lambda b: jnp.maximum(b - 1, 0)
    grid = (hb, s // sup)
    cspec, cshape = _side_cast(cast_w, grid)
    return pl.pallas_call(
        _attn_kernel,
        grid=grid,
        in_specs=[pl.BlockSpec((sup, LANES), lambda h, b: (b, qb + h)),
                  pl.BlockSpec((sup, LANES), lambda h, b: (prev(b), kb + h)),
                  pl.BlockSpec((sup, LANES), lambda h, b: (b, kb + h)),
                  pl.BlockSpec((sup, LANES), lambda h, b: (prev(b), vb + h)),
                  pl.BlockSpec((sup, LANES), lambda h, b: (b, vb + h)),
                  cspec],
        out_specs=[pl.BlockSpec((sup, LANES), lambda h, b: (b, h)), cspec],
        out_shape=[jax.ShapeDtypeStruct((s, ATTN_WIDTH), BF16), cshape],
        scratch_shapes=[pltpu.VMEM((2 * sup, LANES), F32),
                        pltpu.VMEM((2 * sup, LANES), F32),
                        pltpu.VMEM((len(DILATIONS), sup, LANES), F32),
                        pltpu.VMEM((len(DILATIONS), sup, LANES), F32)],
        compiler_params=_params("parallel", "parallel"),
        name="dilated_attn",
    )(pf, pf, pf, pf, pf, cast_w)


def _gate_weights(w_in_l, i_bias_l, f_bias_l):
    g0 = 2 * MLSTM_QK_WIDTH + 2 * MLSTM_WIDTH
    H = MLSTM_HEADS
    wi = w_in_l[:, g0:g0 + H]
    wf = w_in_l[:, g0 + H:g0 + 2 * H]
    wg = jnp.stack([wi, wf], axis=-1)
    wg = jnp.pad(wg, ((0, 0), (0, 0), (0, LANES - 2))).reshape(w_in_l.shape[0], H * LANES)
    bg = jnp.stack([i_bias_l, f_bias_l], axis=-1)
    bg = jnp.pad(bg, ((0, 0), (0, LANES - 2))).reshape(1, H * LANES)
    return wg.astype(BF16), bg.astype(F32)


def _layer(x, norm_mix, w_in, conv_w, conv_b, i_bias, f_bias, mlstm_norm, w_out, norm_mlp, w_up, w_down):
    qk_w = 2 * MLSTM_QK_WIDTH
    g0 = qk_w + 2 * MLSTM_WIDTH
    a0 = g0 + 2 * MLSTM_HEADS
    w_m = _cast_cols_bf16(w_in, g0)
    w_a = w_in[:, a0:].astype(BF16)
    w_g, b_g = _gate_weights(w_in, i_bias, f_bias)

    h = _rmsnorm(x, norm_mix, BF16)
    pqk = _matmul(h, w_m, F32, n=qk_w, name="in_proj_qk")
    pvo = _matmul(h, w_m, BF16, n=g0 - qk_w, col0=qk_w, name="in_proj_vo")
    pa = _matmul(h, w_a, F32, name="in_proj_attn")
    gates = _matmul(h, w_g, F32, tn=w_g.shape[1], name="in_proj_gates")
    y_m, w_out_b = _mlstm(pqk, pvo, gates, b_g, conv_w, conv_b.reshape(1, -1), mlstm_norm.reshape(1, -1), w_out)
    y_a, w_up_b = _dilated_attention(pa, 0, w_up)
    x1 = _out_proj(y_m, y_a, w_out_b, x)
    h2 = _rmsnorm(x1, norm_mlp, BF16)
    u, w_down_b = _matmul(h2, w_up_b, BF16, relu2=True, cast_w=w_down, name="up_proj")
    return _matmul_ktiled_res(u, w_down_b, x1)


def kernel(x, norm_mix, w_in, conv_w, conv_b, i_bias, f_bias, mlstm_norm, w_out, norm_mlp, w_up, w_down, norm_final):
    b, s, d = x.shape
    outs = []
    for bi in range(b):
        xb = x[bi]
        for layer in range(norm_mix.shape[0]):
            xb = _layer(xb, norm_mix[layer], w_in[layer], conv_w[layer], conv_b[layer], i_bias[layer], f_bias[layer],
                        mlstm_norm[layer], w_out[layer], norm_mlp[layer], w_up[layer], w_down[layer])
        outs.append(_rmsnorm(xb, norm_final, x.dtype))
    return jnp.stack(outs, axis=0)
```

```python
import functools

import jax
import jax.numpy as jnp
from jax import lax
from jax.experimental import pallas as pl
from jax.experimental.pallas import tpu as pltpu

F32 = jnp.float32
BF16 = jnp.bfloat16

D_MODEL = 4096
MLSTM_HEADS = 4
MLSTM_DV = 512
MLSTM_DQK = 256
MLSTM_QK_WIDTH = MLSTM_HEADS * MLSTM_DQK
MLSTM_WIDTH = MLSTM_HEADS * MLSTM_DV
CONV_WIDTH = 4
GATE_SOFTCAP = 15.0
ATTN_HEAD_DIM = 128
ATTN_HEADS = 16
ATTN_WIDTH = ATTN_HEADS * ATTN_HEAD_DIM
ATTN_BLOCK = 128
DILATIONS = (1, 4, 16)
D_FF = 4 * D_MODEL
EPS = 1e-6

LANES = 128
SUBLANES = 8
VMEM_LIMIT_BYTES = 56 * 1024 * 1024

MM_TILE = 1024
MM_K_TILE = 4096
NORM_ROWS = 256
MLSTM_CHUNK = 256
ATTN_SUPER = DILATIONS[-1] * ATTN_BLOCK
ATTN_UNROLL = 4


def _params(*sem):
    return pltpu.CompilerParams(dimension_semantics=sem, vmem_limit_bytes=VMEM_LIMIT_BYTES)


def _rmsnorm_kernel(x_ref, g_ref, o_ref):
    x = x_ref[...]
    ms = jnp.mean(x * x, axis=-1, keepdims=True)
    o_ref[...] = (x * lax.rsqrt(ms + EPS) * g_ref[...]).astype(o_ref.dtype)


def _rmsnorm(x, g, out_dtype):
    s, d = x.shape
    return pl.pallas_call(
        _rmsnorm_kernel,
        grid=(s // NORM_ROWS,),
        in_specs=[pl.BlockSpec((NORM_ROWS, d), lambda i: (i, 0)),
                  pl.BlockSpec((1, d), lambda i: (0, 0))],
        out_specs=pl.BlockSpec((NORM_ROWS, d), lambda i: (i, 0)),
        out_shape=jax.ShapeDtypeStruct((s, d), out_dtype),
        compiler_params=_params("parallel"),
        name="rmsnorm",
    )(x, g.reshape(1, d))


def _cast_rows_kernel(w_ref, nxt_ref, o_ref, *, first_shifted):
    i = pl.program_id(0)

    @pl.when(i < first_shifted)
    def _():
        o_ref[...] = w_ref[...].astype(o_ref.dtype)

    @pl.when(i >= first_shifted)
    def _():
        rows = w_ref.shape[0]
        shifted = jnp.concatenate([w_ref[SUBLANES:rows, :], nxt_ref[...]], axis=0)
        o_ref[...] = shifted.astype(o_ref.dtype)


def _cast_rows_drop_gap(wt, gap_row, n_out):
    k = wt.shape[1]
    rows = NORM_ROWS
    assert gap_row % rows == 0 and n_out % rows == 0 and wt.shape[0] == n_out + SUBLANES
    per = rows // SUBLANES
    return pl.pallas_call(
        functools.partial(_cast_rows_kernel, first_shifted=gap_row // rows),
        grid=(n_out // rows,),
        in_specs=[pl.BlockSpec((rows, k), lambda i: (i, 0)),
                  pl.BlockSpec((SUBLANES, k), lambda i: ((i + 1) * per, 0))],
        out_specs=pl.BlockSpec((rows, k), lambda i: (i, 0)),
        out_shape=jax.ShapeDtypeStruct((n_out, k), BF16),
        compiler_params=_params("parallel"),
        name="cast_w_in",
    )(wt, wt)


def _side_cast(w, grid):
    steps = 1
    for g in grid:
        steps *= g
    rows = w.shape[0] // steps
    assert rows * steps == w.shape[0] and rows % (2 * SUBLANES) == 0

    def index_map(*ids):
        lin = ids[0]
        for g, i in zip(grid[1:], ids[1:]):
            lin = lin * g + i
        return (lin, 0)

    spec = pl.BlockSpec((rows, w.shape[1]), index_map)
    return spec, jax.ShapeDtypeStruct(w.shape, BF16)


def _mm_nt_kernel(a_ref, bt_ref, o_ref):
    acc = lax.dot_general(a_ref[...], bt_ref[...].astype(BF16), (((1,), (1,)), ((), ())),
                          preferred_element_type=F32)
    o_ref[...] = acc.astype(o_ref.dtype)


def _matmul_nt(a, bt, out_dtype, *, n=None, row0=0, tn=MM_TILE, name="matmul"):
    m, k = a.shape
    n = bt.shape[0] if n is None else n
    tm = MM_TILE
    jb = row0 // tn
    assert jb * tn == row0
    return pl.pallas_call(
        _mm_nt_kernel,
        grid=(m // tm, n // tn),
        in_specs=[pl.BlockSpec((tm, k), lambda i, j: (i, 0)),
                  pl.BlockSpec((tn, k), lambda i, j: (jb + j, 0))],
        out_specs=pl.BlockSpec((tm, tn), lambda i, j: (i, j)),
        out_shape=jax.ShapeDtypeStruct((m, n), out_dtype),
        compiler_params=_params("parallel", "parallel"),
        name=name,
    )(a, bt)


def _up_proj_kernel(a_ref, ssq_ref, b_ref, wsrc_ref, o_ref, wdst_ref, *, d_norm):
    wdst_ref[...] = wsrc_ref[...].astype(wdst_ref.dtype)
    acc = jnp.dot(a_ref[...], b_ref[...], preferred_element_type=F32)
    acc = acc * lax.rsqrt(ssq_ref[:, 0:1] * (1.0 / d_norm) + EPS)
    acc = jnp.maximum(acc, 0.0)
    o_ref[...] = (acc * acc).astype(o_ref.dtype)


def _up_proj(xg, ssq, w, cast_w):
    m, k = xg.shape
    n = w.shape[1]
    tm = tn = MM_TILE
    grid = (m // tm, n // tn)
    cspec, cshape = _side_cast(cast_w, grid)
    return pl.pallas_call(
        functools.partial(_up_proj_kernel, d_norm=k),
        grid=grid,
        in_specs=[pl.BlockSpec((tm, k), lambda i, j: (i, 0)),
                  pl.BlockSpec((tm, LANES), lambda i, j: (i, 0)),
                  pl.BlockSpec((k, tn), lambda i, j: (0, j)),
                  cspec],
        out_specs=[pl.BlockSpec((tm, tn), lambda i, j: (i, j)), cspec],
        out_shape=[jax.ShapeDtypeStruct((m, n), BF16), cshape],
        compiler_params=_params("parallel", "parallel"),
        name="up_proj",
    )(xg, ssq, w, cast_w)


def _out_proj_kernel(a0_ref, a1_ref, b0_ref, b1_ref, r_ref, g_ref, o_ref, xg_ref, ssq_ref):
    acc = jnp.dot(a0_ref[...], b0_ref[...], preferred_element_type=F32)
    acc = acc + jnp.dot(a1_ref[...], b1_ref[...], preferred_element_type=F32)
    x1 = r_ref[...] + acc
    o_ref[...] = x1
    xg_ref[...] = (x1 * g_ref[...]).astype(xg_ref.dtype)
    part = jnp.broadcast_to(jnp.sum(x1 * x1, axis=-1, keepdims=True), ssq_ref.shape)
    j = pl.program_id(1)

    @pl.when(j == 0)
    def _():
        ssq_ref[...] = part

    @pl.when(j > 0)
    def _():
        ssq_ref[...] += part


def _out_proj(y_m, y_a, w, res, gain):
    m, k0 = y_m.shape
    k1 = y_a.shape[1]
    assert k0 == k1
    n = w.shape[1]
    tm, tn = MM_TILE, MM_TILE // 2
    return pl.pallas_call(
        _out_proj_kernel,
        grid=(m // tm, n // tn),
        in_specs=[pl.BlockSpec((tm, k0), lambda i, j: (i, 0)),
                  pl.BlockSpec((tm, k1), lambda i, j: (i, 0)),
                  pl.BlockSpec((k0, tn), lambda i, j: (0, j)),
                  pl.BlockSpec((k1, tn), lambda i, j: (1, j)),
                  pl.BlockSpec((tm, tn), lambda i, j: (i, j)),
                  pl.BlockSpec((1, tn), lambda i, j: (0, j))],
        out_specs=[pl.BlockSpec((tm, tn), lambda i, j: (i, j)),
                   pl.BlockSpec((tm, tn), lambda i, j: (i, j)),
                   pl.BlockSpec((tm, LANES), lambda i, j: (i, 0))],
        out_shape=[jax.ShapeDtypeStruct((m, n), F32),
                   jax.ShapeDtypeStruct((m, n), BF16),
                   jax.ShapeDtypeStruct((m, LANES), F32)],
        compiler_params=_params("parallel", "arbitrary"),
        name="out_proj",
    )(y_m, y_a, w, w, res, gain.reshape(1, n))


def _mmk_res_kernel(a_ref, b_ref, r_ref, o_ref):
    @pl.when(pl.program_id(2) == 0)
    def _():
        o_ref[...] = r_ref[...]

    o_ref[...] += jnp.dot(a_ref[...], b_ref[...], preferred_element_type=F32)


def _matmul_ktiled_res(a, b, res):
    m, k = a.shape
    n = b.shape[1]
    tm = tn = MM_TILE
    tk = MM_K_TILE
    return pl.pallas_call(
        _mmk_res_kernel,
        grid=(m // tm, n // tn, k // tk),
        in_specs=[pl.BlockSpec((tm, tk), lambda i, j, kk: (i, kk)),
                  pl.BlockSpec((tk, tn), lambda i, j, kk: (kk, j)),
                  pl.BlockSpec((tm, tn), lambda i, j, kk: (i, j))],
        out_specs=pl.BlockSpec((tm, tn), lambda i, j, kk: (i, j)),
        out_shape=jax.ShapeDtypeStruct((m, n), F32),
        compiler_params=_params("parallel", "parallel", "arbitrary"),
        name="down_proj",
    )(a, b, res)


def _split3(a):
    hi = a.astype(BF16)
    r1 = a - hi.astype(F32)
    mid = r1.astype(BF16)
    lo = (r1 - mid.astype(F32)).astype(BF16)
    return hi, mid, lo


def _conv_silu(u_ref, tail_ref, ext_ref, w_ref, b_ref):
    chunk = u_ref.shape[0]
    u = u_ref[...]
    ext_ref[0:SUBLANES, :] = tail_ref[...]
    ext_ref[SUBLANES:SUBLANES + chunk, :] = u
    tail_ref[...] = u[chunk - SUBLANES:, :]
    w = w_ref[...]
    y = b_ref[...] + w[CONV_WIDTH - 1:CONV_WIDTH, :] * u
    for j in range(CONV_WIDTH - 1):
        start = SUBLANES - (CONV_WIDTH - 1) + j
        y = y + w[j:j + 1, :] * ext_ref[start:start + chunk, :]
    return y * (1.0 / (1.0 + jnp.exp(-y)))


def _mlstm_kernel(q_ref, k_ref, v_ref, o_ref, g_ref, gb_ref, cwq_ref, cwk_ref, cbq_ref, cbk_ref, nrm_ref, wsrc_ref,
                  y_ref, wdst_ref, c_s, n_s, qtail_s, ktail_s, ext_s):
    chunk = q_ref.shape[0]
    wdst_ref[...] = wsrc_ref[...].astype(wdst_ref.dtype)

    @pl.when(pl.program_id(1) == 0)
    def _():
        c_s[...] = jnp.zeros_like(c_s)
        n_s[...] = jnp.zeros_like(n_s)
        qtail_s[...] = jnp.zeros_like(qtail_s)
        ktail_s[...] = jnp.zeros_like(ktail_s)

    q = _conv_silu(q_ref, qtail_s, ext_s, cwq_ref, cbq_ref) * (MLSTM_DQK ** -0.5)
    k = _conv_silu(k_ref, ktail_s, ext_s, cwk_ref, cbk_ref)
    qb = q.astype(BF16)
    v = v_ref[...]

    pre = GATE_SOFTCAP * jnp.tanh((g_ref[...] + gb_ref[...]) * (1.0 / GATE_SOFTCAP))
    logsig = jnp.minimum(pre, 0.0) - jnp.log(1.0 + jnp.exp(-jnp.abs(pre)))
    row = lax.broadcasted_iota(jnp.int32, (chunk, chunk), 0)
    col = lax.broadcasted_iota(jnp.int32, (chunk, chunk), 1)
    causal = row >= col
    tri = causal.astype(BF16)
    cum = sum(jnp.dot(tri, piece, preferred_element_type=F32) for piece in _split3(logsig))
    li_col = pre[:, 0:1]
    b_col = cum[:, 1:2]
    b_last = cum[chunk - 1:chunk, 1:2]
    r_col = li_col - b_col
    r_row = pre.T[0:1, :] - cum.T[1:2, :]

    dmat = jnp.where(causal, b_col + r_row, -jnp.inf)
    s_qk = lax.dot_general(qb, k.astype(BF16), (((1,), (1,)), ((), ())), preferred_element_type=F32)
    p = s_qk * jnp.exp(dmat)
    inter_w = jnp.exp(b_col)
    num = jnp.dot(p.astype(BF16), v, preferred_element_type=F32)
    num = num + inter_w * jnp.dot(qb, c_s[...].astype(BF16), preferred_element_type=F32)
    den = jnp.sum(p, axis=-1, keepdims=True) + inter_w * jnp.sum(q * n_s[...], axis=-1, keepdims=True)
    h = num / jnp.maximum(jnp.abs(den), 1.0)

    kw = k * jnp.exp(b_last + r_col)
    decay = jnp.exp(b_last)
    c_s[...] = decay * c_s[...] + lax.dot_general(kw.astype(BF16), v, (((0,), (0,)), ((), ())),
                                                  preferred_element_type=F32)
    n_s[...] = decay * n_s[...] + jnp.sum(kw, axis=0, keepdims=True)

    hn = h * lax.rsqrt(jnp.mean(h * h, axis=-1, keepdims=True) + EPS) * nrm_ref[...]
    gate = 1.0 / (1.0 + jnp.exp(-o_ref[...].astype(F32)))
    y_ref[...] = (gate * hn).astype(y_ref.dtype)


def _mlstm(pf, pb, gates, gate_bias, conv_w, conv_b, mlstm_norm, cast_w):
    s = pf.shape[0]
    L = MLSTM_CHUNK
    H = MLSTM_HEADS
    dk, dv = MLSTM_DQK, MLSTM_DV
    grid = (H, s // L)
    cspec, cshape = _side_cast(cast_w, grid)
    return pl.pallas_call(
        _mlstm_kernel,
        grid=grid,
        in_specs=[pl.BlockSpec((L, dk), lambda h, c: (c, h)),
                  pl.BlockSpec((L, dk), lambda h, c: (c, H + h)),
                  pl.BlockSpec((L, dv), lambda h, c: (c, h)),
                  pl.BlockSpec((L, dv), lambda h, c: (c, H + h)),
                  pl.BlockSpec((L, LANES), lambda h, c: (c, h)),
                  pl.BlockSpec((1, LANES), lambda h, c: (0, h)),
                  pl.BlockSpec((CONV_WIDTH, dk), lambda h, c: (0, h)),
                  pl.BlockSpec((CONV_WIDTH, dk), lambda h, c: (0, H + h)),
                  pl.BlockSpec((1, dk), lambda h, c: (0, h)),
                  pl.BlockSpec((1, dk), lambda h, c: (0, H + h)),
                  pl.BlockSpec((1, dv), lambda h, c: (0, h)),
                  cspec],
        out_specs=[pl.BlockSpec((L, dv), lambda h, c: (c, h)), cspec],
        out_shape=[jax.ShapeDtypeStruct((s, MLSTM_WIDTH), BF16), cshape],
        scratch_shapes=[pltpu.VMEM((dk, dv), F32),
                        pltpu.VMEM((1, dk), F32),
                        pltpu.VMEM((SUBLANES, dk), F32),
                        pltpu.VMEM((SUBLANES, dk), F32),
                        pltpu.VMEM((SUBLANES + L, dk), F32)],
        compiler_params=_params("parallel", "arbitrary"),
        name="mlstm",
    )(pf, pf, pb, pb, gates, gate_bias, conv_w, conv_w, conv_b, conv_b, mlstm_norm, cast_w)


def _band_softmax(q, k, v, bias):
    s = lax.dot_general(q, k, (((1,), (1,)), ((), ())), preferred_element_type=F32) + bias
    m = jnp.max(s, axis=-1, keepdims=True)
    p = jnp.exp(s - m)
    l = jnp.sum(p, axis=-1, keepdims=True)
    o = jnp.dot(p.astype(BF16), v, preferred_element_type=F32) / l
    return o, m + jnp.log(l)


def _attn_kernel(q_ref, kp_ref, kc_ref, vp_ref, vc_ref, wsrc_ref, y_ref, wdst_ref, kk_s, vv_s, o_s, lse_s):
    wdst_ref[...] = wsrc_ref[...].astype(wdst_ref.dtype)
    blk = ATTN_BLOCK
    sup = q_ref.shape[0]
    nblk = sup // blk
    sb = pl.program_id(1)

    kk_s[0:sup, :] = kp_ref[...]
    kk_s[sup:2 * sup, :] = kc_ref[...]
    vv_s[0:sup, :] = vp_ref[...]
    vv_s[sup:2 * sup, :] = vc_ref[...]

    qi = lax.broadcasted_iota(jnp.int32, (blk, 2 * blk), 0)
    kj = lax.broadcasted_iota(jnp.int32, (blk, 2 * blk), 1)
    band = (kj >= qi) & (kj <= qi + blk)
    bias_any = jnp.where(band, 0.0, -jnp.inf).astype(F32)
    bias_first = jnp.where(band & (kj >= blk), 0.0, -jnp.inf).astype(F32)
    scale = ATTN_HEAD_DIM ** -0.5
    seg = DILATIONS[1] * blk

    def body(t, carry):
        def run(p_idx, q_rows, k_rows, first):
            q = (q_ref[q_rows, :] * scale).astype(BF16)
            k = kk_s[k_rows, :].astype(BF16)
            v = vv_s[k_rows, :].astype(BF16)
            bias = jnp.where(first, bias_first, bias_any)
            o, lse = _band_softmax(q, k, v, bias)
            o_s[p_idx, q_rows, :] = o
            lse_s[p_idx, q_rows, :] = jnp.broadcast_to(lse, (blk, LANES))

        q0 = pl.multiple_of(t * blk, blk)
        run(0, pl.ds(q0, blk), pl.ds(sup + q0 - blk, 2 * blk), (sb == 0) & (t == 0))
        d1 = DILATIONS[1]
        sg = lax.shift_right_logical(t, d1.bit_length() - 1)
        r1 = t & (d1 - 1)
        run(1, pl.ds(sg * seg + r1, blk, stride=d1), pl.ds(sup + (sg - 1) * seg + r1, 2 * blk, stride=d1),
            (sb == 0) & (sg == 0))
        d2 = DILATIONS[2]
        run(2, pl.ds(t, blk, stride=d2), pl.ds(t, 2 * blk, stride=d2), sb == 0)
        return carry

    lax.fori_loop(0, nblk, body, 0, unroll=ATTN_UNROLL)

    def combine(t, carry):
        rows = pl.ds(pl.multiple_of(t * blk, blk), blk)
        l0, l1, l2 = lse_s[0, rows, :], lse_s[1, rows, :], lse_s[2, rows, :]
        mx = jnp.maximum(jnp.maximum(l0, l1), l2)
        w0, w1, w2 = jnp.exp(l0 - mx), jnp.exp(l1 - mx), jnp.exp(l2 - mx)
        out = (w0 * o_s[0, rows, :] + w1 * o_s[1, rows, :] + w2 * o_s[2, rows, :]) / (w0 + w1 + w2)
        y_ref[rows, :] = out.astype(y_ref.dtype)
        return carry

    lax.fori_loop(0, nblk, combine, 0)


def _dilated_attention(pf, col0, cast_w):
    s = pf.shape[0]
    sup = ATTN_SUPER
    hb = ATTN_HEADS
    qb, kb, vb = (col0 // LANES + i * hb for i in range(3))
    prev = lambda b: jnp.maximum(b - 1, 0)
    grid = (hb, s // sup)
    cspec, cshape = _side_cast(cast_w, grid)
    return pl.pallas_call(
        _attn_kernel,
        grid=grid,
        in_specs=[pl.BlockSpec((sup, LANES), lambda h, b: (b, qb + h)),
                  pl.BlockSpec((sup, LANES), lambda h, b: (prev(b), kb + h)),
                  pl.BlockSpec((sup, LANES), lambda h, b: (b, kb + h)),
                  pl.BlockSpec((sup, LANES), lambda h, b: (prev(b), vb + h)),
                  pl.BlockSpec((sup, LANES), lambda h, b: (b, vb + h)),
                  cspec],
        out_specs=[pl.BlockSpec((sup, LANES), lambda h, b: (b, h)), cspec],
        out_shape=[jax.ShapeDtypeStruct((s, ATTN_WIDTH), BF16), cshape],
        scratch_shapes=[pltpu.VMEM((2 * sup, LANES), F32),
                        pltpu.VMEM((2 * sup, LANES), F32),
                        pltpu.VMEM((len(DILATIONS), sup, LANES), F32),
                        pltpu.VMEM((len(DILATIONS), sup, LANES), F32)],
        compiler_params=_params("parallel", "parallel"),
        name="dilated_attn",
    )(pf, pf, pf, pf, pf, cast_w)


def _gate_weights(w_in_t, i_bias_l, f_bias_l):
    g0 = 2 * MLSTM_QK_WIDTH + 2 * MLSTM_WIDTH
    H = MLSTM_HEADS
    wg = jnp.stack([w_in_t[g0:g0 + H], w_in_t[g0 + H:g0 + 2 * H]], axis=1)
    wg = jnp.pad(wg, ((0, 0), (0, LANES - 2), (0, 0))).reshape(H * LANES, w_in_t.shape[1])
    bg = jnp.stack([i_bias_l, f_bias_l], axis=-1)
    bg = jnp.pad(bg, ((0, 0), (0, LANES - 2))).reshape(1, H * LANES)
    return wg, bg.astype(F32)


def _layer(x, norm_mix, w_in, conv_w, conv_b, i_bias, f_bias, mlstm_norm, w_out, norm_mlp, w_up, w_down):
    qk_w = 2 * MLSTM_QK_WIDTH
    g0 = qk_w + 2 * MLSTM_WIDTH
    a0 = g0 + 2 * MLSTM_HEADS
    n_proj = w_in.shape[1] - 2 * MLSTM_HEADS
    assert a0 - g0 == SUBLANES
    w_in_t = w_in.T
    w_t = _cast_rows_drop_gap(w_in_t, g0, n_proj)
    w_g, b_g = _gate_weights(w_in_t, i_bias, f_bias)

    h = _rmsnorm(x, norm_mix, BF16)
    pqk = _matmul_nt(h, w_t, F32, n=qk_w, name="in_proj_qk")
    pvo = _matmul_nt(h, w_t, BF16, n=g0 - qk_w, row0=qk_w, name="in_proj_vo")
    pa = _matmul_nt(h, w_t, F32, n=n_proj - g0, row0=g0, name="in_proj_attn")
    gates = _matmul_nt(h, w_g, F32, tn=w_g.shape[0], name="in_proj_gates")
    y_m, w_out_b = _mlstm(pqk, pvo, gates, b_g, conv_w, conv_b.reshape(1, -1), mlstm_norm.reshape(1, -1), w_out)
    y_a, w_up_b = _dilated_attention(pa, 0, w_up)
    x1, x1g, ssq = _out_proj(y_m, y_a, w_out_b, x, norm_mlp)
    u, w_down_b = _up_proj(x1g, ssq, w_up_b, w_down)
    return _matmul_ktiled_res(u, w_down_b, x1)


def kernel(x, norm_mix, w_in, conv_w, conv_b, i_bias, f_bias, mlstm_norm, w_out, norm_mlp, w_up, w_down, norm_final):
    b, s, d = x.shape
    outs = []
    for bi in range(b):
        xb = x[bi]
        for layer in range(norm_mix.shape[0]):
            xb = _layer(xb, norm_mix[layer], w_in[layer], conv_w[layer], conv_b[layer], i_bias[layer], f_bias[layer],
                        mlstm_norm[layer], w_out[layer], norm_mlp[layer], w_up[layer], w_down[layer])
        outs.append(_rmsnorm(xb, norm_final, x.dtype))
    return jnp.stack(outs, axis=0)
```

```python
import functools

import jax
import jax.numpy as jnp
from jax import lax
from jax.experimental import pallas as pl
from jax.experimental.pallas import tpu as pltpu

F32 = jnp.float32
BF16 = jnp.bfloat16

D_MODEL = 4096
MLSTM_HEADS = 4
MLSTM_DV = 512
MLSTM_DQK = 256
MLSTM_QK_WIDTH = MLSTM_HEADS * MLSTM_DQK
MLSTM_WIDTH = MLSTM_HEADS * MLSTM_DV
CONV_WIDTH = 4
GATE_SOFTCAP = 15.0
ATTN_HEAD_DIM = 128
ATTN_HEADS = 16
ATTN_WIDTH = ATTN_HEADS * ATTN_HEAD_DIM
ATTN_BLOCK = 128
DILATIONS = (1, 4, 16)
D_FF = 4 * D_MODEL
EPS = 1e-6

LANES = 128
SUBLANES = 8
VMEM_LIMIT_BYTES = 56 * 1024 * 1024

MM_TILE = 1024
MM_K_TILE = 4096
NORM_ROWS = 256
MLSTM_CHUNK = 256
ATTN_SUPER = DILATIONS[-1] * ATTN_BLOCK
ATTN_UNROLL = 8


def _params(*sem):
    return pltpu.CompilerParams(dimension_semantics=sem, vmem_limit_bytes=VMEM_LIMIT_BYTES)


def _rmsnorm_kernel(x_ref, g_ref, o_ref):
    x = x_ref[...]
    ms = jnp.mean(x * x, axis=-1, keepdims=True)
    o_ref[...] = (x * lax.rsqrt(ms + EPS) * g_ref[...]).astype(o_ref.dtype)


def _rmsnorm(x, g, out_dtype):
    s, d = x.shape
    return pl.pallas_call(
        _rmsnorm_kernel,
        grid=(s // NORM_ROWS,),
        in_specs=[pl.BlockSpec((NORM_ROWS, d), lambda i: (i, 0)),
                  pl.BlockSpec((1, d), lambda i: (0, 0))],
        out_specs=pl.BlockSpec((NORM_ROWS, d), lambda i: (i, 0)),
        out_shape=jax.ShapeDtypeStruct((s, d), out_dtype),
        compiler_params=_params("parallel"),
        name="rmsnorm",
    )(x, g.reshape(1, d))


def _cast_rows_kernel(w_ref, nxt_ref, o_ref, *, first_shifted):
    i = pl.program_id(0)

    @pl.when(i < first_shifted)
    def _():
        o_ref[...] = w_ref[...].astype(o_ref.dtype)

    @pl.when(i >= first_shifted)
    def _():
        rows = w_ref.shape[0]
        shifted = jnp.concatenate([w_ref[SUBLANES:rows, :], nxt_ref[...]], axis=0)
        o_ref[...] = shifted.astype(o_ref.dtype)


def _cast_rows_drop_gap(wt, gap_row, n_out):
    k = wt.shape[1]
    rows = NORM_ROWS
    assert gap_row % rows == 0 and n_out % rows == 0 and wt.shape[0] == n_out + SUBLANES
    per = rows // SUBLANES
    return pl.pallas_call(
        functools.partial(_cast_rows_kernel, first_shifted=gap_row // rows),
        grid=(n_out // rows,),
        in_specs=[pl.BlockSpec((rows, k), lambda i: (i, 0)),
                  pl.BlockSpec((SUBLANES, k), lambda i: ((i + 1) * per, 0))],
        out_specs=pl.BlockSpec((rows, k), lambda i: (i, 0)),
        out_shape=jax.ShapeDtypeStruct((n_out, k), BF16),
        compiler_params=_params("parallel"),
        name="cast_w_in",
    )(wt, wt)


def _side_cast(w, grid):
    steps = 1
    for g in grid:
        steps *= g
    rows = w.shape[0] // steps
    assert rows * steps == w.shape[0] and rows % (2 * SUBLANES) == 0

    def index_map(*ids):
        lin = ids[0]
        for g, i in zip(grid[1:], ids[1:]):
            lin = lin * g + i
        return (lin, 0)

    spec = pl.BlockSpec((rows, w.shape[1]), index_map)
    return spec, jax.ShapeDtypeStruct(w.shape, BF16)


def _mm_nt_kernel(a_ref, bt_ref, *refs, side_cast):
    if side_cast:
        wsrc_ref, o_ref, wdst_ref = refs
        wdst_ref[...] = wsrc_ref[...].astype(wdst_ref.dtype)
    else:
        (o_ref,) = refs
    acc = lax.dot_general(a_ref[...], bt_ref[...].astype(BF16), (((1,), (1,)), ((), ())),
                          preferred_element_type=F32)
    o_ref[...] = acc.astype(o_ref.dtype)


def _matmul_nt(a, bt, out_dtype, *, n=None, row0=0, tn=MM_TILE, cast_w=None, name="matmul"):
    m, k = a.shape
    n = bt.shape[0] if n is None else n
    tm = MM_TILE
    jb = row0 // tn
    assert jb * tn == row0
    grid = (m // tm, n // tn)
    in_specs = [pl.BlockSpec((tm, k), lambda i, j: (i, 0)),
                pl.BlockSpec((tn, k), lambda i, j: (jb + j, 0))]
    out_specs = pl.BlockSpec((tm, tn), lambda i, j: (i, j))
    out_shape = jax.ShapeDtypeStruct((m, n), out_dtype)
    args = [a, bt]
    if cast_w is not None:
        cspec, cshape = _side_cast(cast_w, grid)
        in_specs.append(cspec)
        out_specs, out_shape = [out_specs, cspec], [out_shape, cshape]
        args.append(cast_w)
    return pl.pallas_call(
        functools.partial(_mm_nt_kernel, side_cast=cast_w is not None),
        grid=grid, in_specs=in_specs, out_specs=out_specs, out_shape=out_shape,
        compiler_params=_params("parallel", "parallel"),
        name=name,
    )(*args)


def _up_proj_kernel(a_ref, ssq_ref, b_ref, wsrc_ref, o_ref, wdst_ref, *, d_norm):
    wdst_ref[...] = wsrc_ref[...].astype(wdst_ref.dtype)
    acc = jnp.dot(a_ref[...], b_ref[...], preferred_element_type=F32)
    acc = acc * lax.rsqrt(ssq_ref[:, 0:1] * (1.0 / d_norm) + EPS)
    acc = jnp.maximum(acc, 0.0)
    o_ref[...] = (acc * acc).astype(o_ref.dtype)


def _up_proj(xg, ssq, w, cast_w):
    m, k = xg.shape
    n = w.shape[1]
    tm = tn = MM_TILE
    grid = (m // tm, n // tn)
    cspec, cshape = _side_cast(cast_w, grid)
    return pl.pallas_call(
        functools.partial(_up_proj_kernel, d_norm=k),
        grid=grid,
        in_specs=[pl.BlockSpec((tm, k), lambda i, j: (i, 0)),
                  pl.BlockSpec((tm, LANES), lambda i, j: (i, 0)),
                  pl.BlockSpec((k, tn), lambda i, j: (0, j)),
                  cspec],
        out_specs=[pl.BlockSpec((tm, tn), lambda i, j: (i, j)), cspec],
        out_shape=[jax.ShapeDtypeStruct((m, n), BF16), cshape],
        compiler_params=_params("parallel", "parallel"),
        name="up_proj",
    )(xg, ssq, w, cast_w)


def _out_proj_kernel(a0_ref, a1_ref, b0_ref, b1_ref, r_ref, g_ref, wsrc_ref, o_ref, xg_ref, ssq_ref, wdst_ref):
    wdst_ref[...] = wsrc_ref[...].astype(wdst_ref.dtype)
    acc = jnp.dot(a0_ref[...], b0_ref[...], preferred_element_type=F32)
    acc = acc + jnp.dot(a1_ref[...], b1_ref[...], preferred_element_type=F32)
    x1 = r_ref[...] + acc
    o_ref[...] = x1
    xg_ref[...] = (x1 * g_ref[...]).astype(xg_ref.dtype)
    part = jnp.broadcast_to(jnp.sum(x1 * x1, axis=-1, keepdims=True), ssq_ref.shape)
    j = pl.program_id(1)

    @pl.when(j == 0)
    def _():
        ssq_ref[...] = part

    @pl.when(j > 0)
    def _():
        ssq_ref[...] += part


def _out_proj(y_m, y_a, w, res, gain, cast_w):
    m, k0 = y_m.shape
    k1 = y_a.shape[1]
    assert k0 == k1
    n = w.shape[1]
    tm, tn = MM_TILE, MM_TILE // 2
    grid = (m // tm, n // tn)
    cspec, cshape = _side_cast(cast_w, grid)
    return pl.pallas_call(
        _out_proj_kernel,
        grid=grid,
        in_specs=[pl.BlockSpec((tm, k0), lambda i, j: (i, 0)),
                  pl.BlockSpec((tm, k1), lambda i, j: (i, 0)),
                  pl.BlockSpec((k0, tn), lambda i, j: (0, j)),
                  pl.BlockSpec((k1, tn), lambda i, j: (1, j)),
                  pl.BlockSpec((tm, tn), lambda i, j: (i, j)),
                  pl.BlockSpec((1, tn), lambda i, j: (0, j)),
                  cspec],
        out_specs=[pl.BlockSpec((tm, tn), lambda i, j: (i, j)),
                   pl.BlockSpec((tm, tn), lambda i, j: (i, j)),
                   pl.BlockSpec((tm, LANES), lambda i, j: (i, 0)),
                   cspec],
        out_shape=[jax.ShapeDtypeStruct((m, n), F32),
                   jax.ShapeDtypeStruct((m, n), BF16),
                   jax.ShapeDtypeStruct((m, LANES), F32),
                   cshape],
        compiler_params=_params("parallel", "arbitrary"),
        name="out_proj",
    )(y_m, y_a, w, w, res, gain.reshape(1, n), cast_w)


def _mmk_res_kernel(a_ref, b_ref, r_ref, o_ref):
    @pl.when(pl.program_id(2) == 0)
    def _():
        o_ref[...] = r_ref[...]

    o_ref[...] += jnp.dot(a_ref[...], b_ref[...], preferred_element_type=F32)


def _matmul_ktiled_res(a, b, res):
    m, k = a.shape
    n = b.shape[1]
    tm = tn = MM_TILE
    tk = MM_K_TILE
    return pl.pallas_call(
        _mmk_res_kernel,
        grid=(m // tm, n // tn, k // tk),
        in_specs=[pl.BlockSpec((tm, tk), lambda i, j, kk: (i, kk)),
                  pl.BlockSpec((tk, tn), lambda i, j, kk: (kk, j)),
                  pl.BlockSpec((tm, tn), lambda i, j, kk: (i, j))],
        out_specs=pl.BlockSpec((tm, tn), lambda i, j, kk: (i, j)),
        out_shape=jax.ShapeDtypeStruct((m, n), F32),
        compiler_params=_params("parallel", "parallel", "arbitrary"),
        name="down_proj",
    )(a, b, res)


def _split3(a):
    hi = a.astype(BF16)
    r1 = a - hi.astype(F32)
    mid = r1.astype(BF16)
    lo = (r1 - mid.astype(F32)).astype(BF16)
    return hi, mid, lo


def _conv_silu(u_ref, tail_ref, ext_ref, w_ref, b_ref):
    chunk = u_ref.shape[0]
    u = u_ref[...]
    ext_ref[0:SUBLANES, :] = tail_ref[...]
    ext_ref[SUBLANES:SUBLANES + chunk, :] = u
    tail_ref[...] = u[chunk - SUBLANES:, :]
    w = w_ref[...]
    y = b_ref[...] + w[CONV_WIDTH - 1:CONV_WIDTH, :] * u
    for j in range(CONV_WIDTH - 1):
        start = SUBLANES - (CONV_WIDTH - 1) + j
        y = y + w[j:j + 1, :] * ext_ref[start:start + chunk, :]
    return y * (1.0 / (1.0 + jnp.exp(-y)))


def _mlstm_kernel(qk_ref, v_ref, o_ref, g_ref, gb_ref, cw_ref, cb_ref, nrm_ref, y_ref, c_s, n_s, tail_s, ext_s):
    chunk = qk_ref.shape[0]
    dk, dv = MLSTM_DQK, MLSTM_DV

    @pl.when(pl.program_id(0) == 0)
    def _():
        c_s[...] = jnp.zeros_like(c_s)
        n_s[...] = jnp.zeros_like(n_s)
        tail_s[...] = jnp.zeros_like(tail_s)

    qk = _conv_silu(qk_ref, tail_s, ext_s, cw_ref, cb_ref)

    pre = GATE_SOFTCAP * jnp.tanh((g_ref[...] + gb_ref[...]) * (1.0 / GATE_SOFTCAP))
    logsig = jnp.minimum(pre, 0.0) - jnp.log(1.0 + jnp.exp(-jnp.abs(pre)))
    row = lax.broadcasted_iota(jnp.int32, (chunk, chunk), 0)
    col = lax.broadcasted_iota(jnp.int32, (chunk, chunk), 1)
    causal = row >= col
    tri = causal.astype(BF16)
    cum = sum(jnp.dot(tri, piece, preferred_element_type=F32) for piece in _split3(logsig))
    pre_t = pre.T
    cum_t = cum.T

    for hd in range(MLSTM_HEADS):
        g0 = 2 * hd
        q = qk[:, hd * dk:(hd + 1) * dk] * (dk ** -0.5)
        k = qk[:, (MLSTM_HEADS + hd) * dk:(MLSTM_HEADS + hd + 1) * dk]
        qb = q.astype(BF16)
        v = v_ref[:, hd * dv:(hd + 1) * dv]
        li_col = pre[:, g0:g0 + 1]
        b_col = cum[:, g0 + 1:g0 + 2]
        b_last = cum[chunk - 1:chunk, g0 + 1:g0 + 2]
        r_col = li_col - b_col
        r_row = pre_t[g0:g0 + 1, :] - cum_t[g0 + 1:g0 + 2, :]

        dmat = jnp.where(causal, b_col + r_row, -jnp.inf)
        s_qk = lax.dot_general(qb, k.astype(BF16), (((1,), (1,)), ((), ())), preferred_element_type=F32)
        p = s_qk * jnp.exp(dmat)
        inter_w = jnp.exp(b_col)
        num = jnp.dot(p.astype(BF16), v, preferred_element_type=F32)
        num = num + inter_w * jnp.dot(qb, c_s[hd].astype(BF16), preferred_element_type=F32)
        den = jnp.sum(p, axis=-1, keepdims=True) + inter_w * jnp.sum(q * n_s[hd], axis=-1, keepdims=True)
        h = num / jnp.maximum(jnp.abs(den), 1.0)

        kw = k * jnp.exp(b_last + r_col)
        decay = jnp.exp(b_last)
        c_s[hd] = decay * c_s[hd] + lax.dot_general(kw.astype(BF16), v, (((0,), (0,)), ((), ())),
                                                    preferred_element_type=F32)
        n_s[hd] = decay * n_s[hd] + jnp.sum(kw, axis=0, keepdims=True)

        hn = h * lax.rsqrt(jnp.mean(h * h, axis=-1, keepdims=True) + EPS) * nrm_ref[:, hd * dv:(hd + 1) * dv]
        gate = 1.0 / (1.0 + jnp.exp(-o_ref[:, hd * dv:(hd + 1) * dv].astype(F32)))
        y_ref[:, hd * dv:(hd + 1) * dv] = (gate * hn).astype(y_ref.dtype)


def _mlstm(pqk, pvo, gates, gate_bias, conv_w, conv_b, mlstm_norm):
    s = pqk.shape[0]
    L = MLSTM_CHUNK
    H = MLSTM_HEADS
    dk, dv = MLSTM_DQK, MLSTM_DV
    qkw = 2 * MLSTM_QK_WIDTH
    return pl.pallas_call(
        _mlstm_kernel,
        grid=(s // L,),
        in_specs=[pl.BlockSpec((L, qkw), lambda c: (c, 0)),
                  pl.BlockSpec((L, MLSTM_WIDTH), lambda c: (c, 0)),
                  pl.BlockSpec((L, MLSTM_WIDTH), lambda c: (c, 1)),
                  pl.BlockSpec((L, LANES), lambda c: (c, 0)),
                  pl.BlockSpec((1, LANES), lambda c: (0, 0)),
                  pl.BlockSpec((CONV_WIDTH, qkw), lambda c: (0, 0)),
                  pl.BlockSpec((1, qkw), lambda c: (0, 0)),
                  pl.BlockSpec((1, MLSTM_WIDTH), lambda c: (0, 0))],
        out_specs=pl.BlockSpec((L, MLSTM_WIDTH), lambda c: (c, 0)),
        out_shape=jax.ShapeDtypeStruct((s, MLSTM_WIDTH), BF16),
        scratch_shapes=[pltpu.VMEM((H, dk, dv), F32),
                        pltpu.VMEM((H, 1, dk), F32),
                        pltpu.VMEM((SUBLANES, qkw), F32),
                        pltpu.VMEM((SUBLANES + L, qkw), F32)],
        compiler_params=_params("arbitrary"),
        name="mlstm",
    )(pqk, pvo, pvo, gates, gate_bias, conv_w, conv_b, mlstm_norm)


def _band_softmax(q, k, v, bias):
    s = lax.dot_general(q, k, (((1,), (1,)), ((), ())), preferred_element_type=F32) + bias
    m = jnp.max(s, axis=-1, keepdims=True)
    p = jnp.exp2(s - m)
    l = jnp.sum(p, axis=-1, keepdims=True)
    o = jnp.dot(p.astype(BF16), v, preferred_element_type=F32) / l
    return o, m + jnp.log2(l)


def _attn_kernel(q_ref, kp_ref, kc_ref, vp_ref, vc_ref, y_ref, q4_s, k4_s, v4_s, o_s, lse_s):
    blk = ATTN_BLOCK
    sup = q_ref.shape[0]
    nblk = sup // blk
    d1, d2 = DILATIONS[1], DILATIONS[2]
    sub = sup // d1
    sb = pl.program_id(1)
    qscale = 1.4426950408889634 * ATTN_HEAD_DIM ** -0.5

    for r in range(d1):
        q4_s[r] = q_ref[pl.ds(r, sub, stride=d1), :] * qscale
        k4_s[r, 0:sub, :] = kp_ref[pl.ds(r, sub, stride=d1), :]
        k4_s[r, sub:2 * sub, :] = kc_ref[pl.ds(r, sub, stride=d1), :]
        v4_s[r, 0:sub, :] = vp_ref[pl.ds(r, sub, stride=d1), :]
        v4_s[r, sub:2 * sub, :] = vc_ref[pl.ds(r, sub, stride=d1), :]

    qi = lax.broadcasted_iota(jnp.int32, (blk, 2 * blk), 0)
    kj = lax.broadcasted_iota(jnp.int32, (blk, 2 * blk), 1)
    band = (kj >= qi) & (kj <= qi + blk)
    bias_any = jnp.where(band, 0.0, -jnp.inf).astype(F32)
    bias_first = jnp.where(band & (kj >= blk), 0.0, -jnp.inf).astype(F32)

    def body(t, carry):
        def run(p_idx, q, k, v, out_rows, first):
            bias = jnp.where(first, bias_first, bias_any)
            o, lse = _band_softmax(q.astype(BF16), k.astype(BF16), v.astype(BF16), bias)
            o_s[p_idx, out_rows, :] = o
            lse_s[p_idx, out_rows, :] = jnp.broadcast_to(lse, (blk, LANES))

        q0 = pl.multiple_of(t * blk, blk)
        qp = pl.multiple_of(jnp.maximum(t - 1, 0) * blk, blk)
        at_start = t == 0
        k_prev = jnp.where(at_start, kp_ref[sup - blk:sup, :], kc_ref[pl.ds(qp, blk), :])
        v_prev = jnp.where(at_start, vp_ref[sup - blk:sup, :], vc_ref[pl.ds(qp, blk), :])
        run(0, q_ref[pl.ds(q0, blk), :] * qscale,
            jnp.concatenate([k_prev, kc_ref[pl.ds(q0, blk), :]], axis=0),
            jnp.concatenate([v_prev, vc_ref[pl.ds(q0, blk), :]], axis=0),
            pl.ds(q0, blk), (sb == 0) & at_start)

        sg = lax.shift_right_logical(t, d1.bit_length() - 1)
        r = t & (d1 - 1)
        s0 = pl.multiple_of(sg * blk, blk)
        kwin = pl.ds(pl.multiple_of(sub - blk + sg * blk, blk), 2 * blk)
        run(1, q4_s[r, pl.ds(s0, blk), :], k4_s[r, kwin, :], v4_s[r, kwin, :],
            pl.ds(sg * (d1 * blk) + r, blk, stride=d1), (sb == 0) & (sg == 0))

        run(2, q4_s[r, pl.ds(sg, blk, stride=d2 // d1), :], k4_s[r, pl.ds(sg, 2 * blk, stride=d2 // d1), :],
            v4_s[r, pl.ds(sg, 2 * blk, stride=d2 // d1), :], pl.ds(t, blk, stride=d2), sb == 0)
        return carry

    lax.fori_loop(0, nblk, body, 0, unroll=ATTN_UNROLL)

    def combine(t, carry):
        rows = pl.ds(pl.multiple_of(t * blk, blk), blk)
        l0, l1, l2 = lse_s[0, rows, :], lse_s[1, rows, :], lse_s[2, rows, :]
        mx = jnp.maximum(jnp.maximum(l0, l1), l2)
        w0, w1, w2 = jnp.exp2(l0 - mx), jnp.exp2(l1 - mx), jnp.exp2(l2 - mx)
        out = (w0 * o_s[0, rows, :] + w1 * o_s[1, rows, :] + w2 * o_s[2, rows, :]) / (w0 + w1 + w2)
        y_ref[rows, :] = out.astype(y_ref.dtype)
        return carry

    lax.fori_loop(0, nblk, combine, 0)


def _dilated_attention(pf, col0):
    s = pf.shape[0]
    sup = ATTN_SUPER
    hb = ATTN_HEADS
    d1 = DILATIONS[1]
    qb, kb, vb = (col0 // LANES + i * hb for i in range(3))
    prev = lambda b: jnp.maximum(b - 1, 0)
    return pl.pallas_call(
        _attn_kernel,
        grid=(hb, s // sup),
        in_specs=[pl.BlockSpec((sup, LANES), lambda h, b: (b, qb + h)),
                  pl.BlockSpec((sup, LANES), lambda h, b: (prev(b), kb + h)),
                  pl.BlockSpec((sup, LANES), lambda h, b: (b, kb + h)),
                  pl.BlockSpec((sup, LANES), lambda h, b: (prev(b), vb + h)),
                  pl.BlockSpec((sup, LANES), lambda h, b: (b, vb + h))],
        out_specs=pl.BlockSpec((sup, LANES), lambda h, b: (b, h)),
        out_shape=jax.ShapeDtypeStruct((s, ATTN_WIDTH), BF16),
        scratch_shapes=[pltpu.VMEM((d1, sup // d1, LANES), F32),
                        pltpu.VMEM((d1, 2 * sup // d1, LANES), F32),
                        pltpu.VMEM((d1, 2 * sup // d1, LANES), F32),
                        pltpu.VMEM((len(DILATIONS), sup, LANES), F32),
                        pltpu.VMEM((len(DILATIONS), sup, LANES), F32)],
        compiler_params=_params("parallel", "parallel"),
        name="dilated_attn",
    )(pf, pf, pf, pf, pf)


def _gate_weights(w_in_t, i_bias_l, f_bias_l):
    g0 = 2 * MLSTM_QK_WIDTH + 2 * MLSTM_WIDTH
    H = MLSTM_HEADS
    wg = jnp.stack([w_in_t[g0:g0 + H], w_in_t[g0 + H:g0 + 2 * H]], axis=1).reshape(2 * H, w_in_t.shape[1])
    wg = jnp.pad(wg, ((0, LANES - 2 * H), (0, 0)))
    bg = jnp.stack([i_bias_l, f_bias_l], axis=-1).reshape(1, 2 * H)
    bg = jnp.pad(bg, ((0, 0), (0, LANES - 2 * H)))
    return wg, bg.astype(F32)


def _layer(x, norm_mix, w_in, conv_w, conv_b, i_bias, f_bias, mlstm_norm, w_out, norm_mlp, w_up, w_down):
    qk_w = 2 * MLSTM_QK_WIDTH
    g0 = qk_w + 2 * MLSTM_WIDTH
    a0 = g0 + 2 * MLSTM_HEADS
    n_proj = w_in.shape[1] - 2 * MLSTM_HEADS
    assert a0 - g0 == SUBLANES
    w_in_t = w_in.T
    w_t = _cast_rows_drop_gap(w_in_t, g0, n_proj)
    w_g, b_g = _gate_weights(w_in_t, i_bias, f_bias)

    h = _rmsnorm(x, norm_mix, BF16)
    pqk = _matmul_nt(h, w_t, F32, n=qk_w, name="in_proj_qk")
    pvo, w_out_b = _matmul_nt(h, w_t, BF16, n=g0 - qk_w, row0=qk_w, cast_w=w_out, name="in_proj_vo")
    pa = _matmul_nt(h, w_t, F32, n=n_proj - g0, row0=g0, name="in_proj_attn")
    gates = _matmul_nt(h, w_g, F32, tn=w_g.shape[0], name="in_proj_gates")
    y_m = _mlstm(pqk, pvo, gates, b_g, conv_w, conv_b.reshape(1, -1), mlstm_norm.reshape(1, -1))
    y_a = _dilated_attention(pa, 0)
    x1, x1g, ssq, w_up_b = _out_proj(y_m, y_a, w_out_b, x, norm_mlp, w_up)
    u, w_down_b = _up_proj(x1g, ssq, w_up_b, w_down)
    return _matmul_ktiled_res(u, w_down_b, x1)


def kernel(x, norm_mix, w_in, conv_w, conv_b, i_bias, f_bias, mlstm_norm, w_out, norm_mlp, w_up, w_down, norm_final):
    b, s, d = x.shape
    outs = []
    for bi in range(b):
        xb = x[bi]
        for layer in range(norm_mix.shape[0]):
            xb = _layer(xb, norm_mix[layer], w_in[layer], conv_w[layer], conv_b[layer], i_bias[layer], f_bias[layer],
                        mlstm_norm[layer], w_out[layer], norm_mlp[layer], w_up[layer], w_down[layer])
        outs.append(_rmsnorm(xb, norm_final, x.dtype))
    return jnp.stack(outs, axis=0)
```

```python
import functools

import jax
import jax.numpy as jnp
from jax import lax
from jax.experimental import pallas as pl
from jax.experimental.pallas import tpu as pltpu

F32 = jnp.float32
BF16 = jnp.bfloat16

D_MODEL = 4096
MLSTM_HEADS = 4
MLSTM_DV = 512
MLSTM_DQK = 256
MLSTM_QK_WIDTH = MLSTM_HEADS * MLSTM_DQK
MLSTM_WIDTH = MLSTM_HEADS * MLSTM_DV
CONV_WIDTH = 4
GATE_SOFTCAP = 15.0
ATTN_HEAD_DIM = 128
ATTN_HEADS = 16
ATTN_WIDTH = ATTN_HEADS * ATTN_HEAD_DIM
ATTN_BLOCK = 128
DILATIONS = (1, 4, 16)
D_FF = 4 * D_MODEL
EPS = 1e-6

LANES = 128
SUBLANES = 8
VMEM_LIMIT_BYTES = 56 * 1024 * 1024

MM_TILE = 1024
MM_TILE_3Q = 768
MM_K_TILE = 4096
NORM_ROWS = 256
MLSTM_CHUNK = 256
ATTN_SUPER = DILATIONS[-1] * ATTN_BLOCK
ATTN_UNROLL = 8


def _params(*sem):
    return pltpu.CompilerParams(dimension_semantics=sem, vmem_limit_bytes=VMEM_LIMIT_BYTES)


def _rmsnorm_kernel(x_ref, g_ref, o_ref):
    x = x_ref[...]
    ms = jnp.mean(x * x, axis=-1, keepdims=True)
    o_ref[...] = (x * lax.rsqrt(ms + EPS) * g_ref[...]).astype(o_ref.dtype)


def _rmsnorm(x, g, out_dtype):
    s, d = x.shape
    return pl.pallas_call(
        _rmsnorm_kernel,
        grid=(s // NORM_ROWS,),
        in_specs=[pl.BlockSpec((NORM_ROWS, d), lambda i: (i, 0)),
                  pl.BlockSpec((1, d), lambda i: (0, 0))],
        out_specs=pl.BlockSpec((NORM_ROWS, d), lambda i: (i, 0)),
        out_shape=jax.ShapeDtypeStruct((s, d), out_dtype),
        compiler_params=_params("parallel"),
        name="rmsnorm",
    )(x, g.reshape(1, d))


def _cast_rows_kernel(w_ref, nxt_ref, o_ref, *, first_shifted):
    i = pl.program_id(0)

    @pl.when(i < first_shifted)
    def _():
        o_ref[...] = w_ref[...].astype(o_ref.dtype)

    @pl.when(i >= first_shifted)
    def _():
        rows = w_ref.shape[0]
        shifted = jnp.concatenate([w_ref[SUBLANES:rows, :], nxt_ref[...]], axis=0)
        o_ref[...] = shifted.astype(o_ref.dtype)


def _cast_rows_drop_gap(wt, gap_row, n_out):
    k = wt.shape[1]
    rows = NORM_ROWS
    assert gap_row % rows == 0 and n_out % rows == 0 and wt.shape[0] == n_out + SUBLANES
    per = rows // SUBLANES
    return pl.pallas_call(
        functools.partial(_cast_rows_kernel, first_shifted=gap_row // rows),
        grid=(n_out // rows,),
        in_specs=[pl.BlockSpec((rows, k), lambda i: (i, 0)),
                  pl.BlockSpec((SUBLANES, k), lambda i: ((i + 1) * per, 0))],
        out_specs=pl.BlockSpec((rows, k), lambda i: (i, 0)),
        out_shape=jax.ShapeDtypeStruct((n_out, k), BF16),
        compiler_params=_params("parallel"),
        name="cast_w_in",
    )(wt, wt)


def _side_cast(w, grid):
    steps = 1
    for g in grid:
        steps *= g
    rows = w.shape[0] // steps
    assert rows * steps == w.shape[0] and rows % (2 * SUBLANES) == 0

    def index_map(*ids):
        lin = ids[0]
        for g, i in zip(grid[1:], ids[1:]):
            lin = lin * g + i
        return (lin, 0)

    spec = pl.BlockSpec((rows, w.shape[1]), index_map)
    return spec, jax.ShapeDtypeStruct(w.shape, BF16)


def _mm_nt_kernel(a_ref, bt_ref, *refs, side_cast):
    if side_cast:
        wsrc_ref, o_ref, wdst_ref = refs
        wdst_ref[...] = wsrc_ref[...].astype(wdst_ref.dtype)
    else:
        (o_ref,) = refs
    acc = lax.dot_general(a_ref[...], bt_ref[...].astype(BF16), (((1,), (1,)), ((), ())),
                          preferred_element_type=F32)
    o_ref[...] = acc.astype(o_ref.dtype)


def _matmul_nt(a, bt, out_dtype, *, n=None, row0=0, tn=MM_TILE, cast_w=None, name="matmul"):
    m, k = a.shape
    n = bt.shape[0] if n is None else n
    tm = MM_TILE
    jb = row0 // tn
    assert jb * tn == row0
    grid = (m // tm, n // tn)
    in_specs = [pl.BlockSpec((tm, k), lambda i, j: (i, 0)),
                pl.BlockSpec((tn, k), lambda i, j: (jb + j, 0))]
    out_specs = pl.BlockSpec((tm, tn), lambda i, j: (i, j))
    out_shape = jax.ShapeDtypeStruct((m, n), out_dtype)
    args = [a, bt]
    if cast_w is not None:
        cspec, cshape = _side_cast(cast_w, grid)
        in_specs.append(cspec)
        out_specs, out_shape = [out_specs, cspec], [out_shape, cshape]
        args.append(cast_w)
    return pl.pallas_call(
        functools.partial(_mm_nt_kernel, side_cast=cast_w is not None),
        grid=grid, in_specs=in_specs, out_specs=out_specs, out_shape=out_shape,
        compiler_params=_params("parallel", "parallel"),
        name=name,
    )(*args)


def _up_proj_kernel(a_ref, ssq_ref, b_ref, wsrc_ref, o_ref, wdst_ref, *, d_norm):
    wdst_ref[...] = wsrc_ref[...].astype(wdst_ref.dtype)
    acc = jnp.dot(a_ref[...], b_ref[...], preferred_element_type=F32)
    acc = acc * lax.rsqrt(ssq_ref[:, 0:1] * (1.0 / d_norm) + EPS)
    acc = jnp.maximum(acc, 0.0)
    o_ref[...] = (acc * acc).astype(o_ref.dtype)


def _up_proj(xg, ssq, w, cast_w):
    m, k = xg.shape
    n = w.shape[1]
    tm = tn = MM_TILE
    grid = (m // tm, n // tn)
    cspec, cshape = _side_cast(cast_w, grid)
    return pl.pallas_call(
        functools.partial(_up_proj_kernel, d_norm=k),
        grid=grid,
        in_specs=[pl.BlockSpec((tm, k), lambda i, j: (i, 0)),
                  pl.BlockSpec((tm, LANES), lambda i, j: (i, 0)),
                  pl.BlockSpec((k, tn), lambda i, j: (0, j)),
                  cspec],
        out_specs=[pl.BlockSpec((tm, tn), lambda i, j: (i, j)), cspec],
        out_shape=[jax.ShapeDtypeStruct((m, n), BF16), cshape],
        compiler_params=_params("parallel", "parallel"),
        name="up_proj",
    )(xg, ssq, w, cast_w)


def _out_proj_kernel(a0_ref, a1_ref, b0_ref, b1_ref, r_ref, g_ref, o_ref, xg_ref, ssq_ref):
    acc = jnp.dot(a0_ref[...], b0_ref[...], preferred_element_type=F32)
    acc = acc + jnp.dot(a1_ref[...], b1_ref[...], preferred_element_type=F32)
    x1 = r_ref[...] + acc
    o_ref[...] = x1
    xg_ref[...] = (x1 * g_ref[...]).astype(xg_ref.dtype)
    part = jnp.broadcast_to(jnp.sum(x1 * x1, axis=-1, keepdims=True), ssq_ref.shape)
    j = pl.program_id(1)

    @pl.when(j == 0)
    def _():
        ssq_ref[...] = part

    @pl.when(j > 0)
    def _():
        ssq_ref[...] += part


def _out_proj(y_m, y_a, w, res, gain):
    m, k0 = y_m.shape
    k1 = y_a.shape[1]
    assert k0 == k1
    n = w.shape[1]
    tm, tn = MM_TILE, MM_TILE // 2
    return pl.pallas_call(
        _out_proj_kernel,
        grid=(m // tm, n // tn),
        in_specs=[pl.BlockSpec((tm, k0), lambda i, j: (i, 0)),
                  pl.BlockSpec((tm, k1), lambda i, j: (i, 0)),
                  pl.BlockSpec((k0, tn), lambda i, j: (0, j)),
                  pl.BlockSpec((k1, tn), lambda i, j: (1, j)),
                  pl.BlockSpec((tm, tn), lambda i, j: (i, j)),
                  pl.BlockSpec((1, tn), lambda i, j: (0, j))],
        out_specs=[pl.BlockSpec((tm, tn), lambda i, j: (i, j)),
                   pl.BlockSpec((tm, tn), lambda i, j: (i, j)),
                   pl.BlockSpec((tm, LANES), lambda i, j: (i, 0))],
        out_shape=[jax.ShapeDtypeStruct((m, n), F32),
                   jax.ShapeDtypeStruct((m, n), BF16),
                   jax.ShapeDtypeStruct((m, LANES), F32)],
        compiler_params=_params("parallel", "arbitrary"),
        name="out_proj",
    )(y_m, y_a, w, w, res, gain.reshape(1, n))


def _mmk_res_kernel(a_ref, b_ref, r_ref, o_ref):
    @pl.when(pl.program_id(2) == 0)
    def _():
        o_ref[...] = r_ref[...]

    o_ref[...] += jnp.dot(a_ref[...], b_ref[...], preferred_element_type=F32)


def _matmul_ktiled_res(a, b, res):
    m, k = a.shape
    n = b.shape[1]
    tm = tn = MM_TILE
    tk = MM_K_TILE
    return pl.pallas_call(
        _mmk_res_kernel,
        grid=(m // tm, n // tn, k // tk),
        in_specs=[pl.BlockSpec((tm, tk), lambda i, j, kk: (i, kk)),
                  pl.BlockSpec((tk, tn), lambda i, j, kk: (kk, j)),
                  pl.BlockSpec((tm, tn), lambda i, j, kk: (i, j))],
        out_specs=pl.BlockSpec((tm, tn), lambda i, j, kk: (i, j)),
        out_shape=jax.ShapeDtypeStruct((m, n), F32),
        compiler_params=_params("parallel", "parallel", "arbitrary"),
        name="down_proj",
    )(a, b, res)


def _split3(a):
    hi = a.astype(BF16)
    r1 = a - hi.astype(F32)
    mid = r1.astype(BF16)
    lo = (r1 - mid.astype(F32)).astype(BF16)
    return hi, mid, lo


def _conv_silu(u_ref, tail_ref, ext_ref, w_ref, b_ref):
    chunk = u_ref.shape[0]
    u = u_ref[...]
    ext_ref[0:SUBLANES, :] = tail_ref[...]
    ext_ref[SUBLANES:SUBLANES + chunk, :] = u
    tail_ref[...] = u[chunk - SUBLANES:, :]
    w = w_ref[...]
    y = b_ref[...] + w[CONV_WIDTH - 1:CONV_WIDTH, :] * u
    for j in range(CONV_WIDTH - 1):
        start = SUBLANES - (CONV_WIDTH - 1) + j
        y = y + w[j:j + 1, :] * ext_ref[start:start + chunk, :]
    return y * (1.0 / (1.0 + jnp.exp(-y)))


def _mlstm_kernel(qk_ref, v_ref, o_ref, g_ref, gb_ref, cw_ref, cb_ref, nrm_ref, y_ref, c_s, n_s, tail_s, ext_s):
    chunk = qk_ref.shape[0]
    dk, dv = MLSTM_DQK, MLSTM_DV

    @pl.when(pl.program_id(0) == 0)
    def _():
        c_s[...] = jnp.zeros_like(c_s)
        n_s[...] = jnp.zeros_like(n_s)
        tail_s[...] = jnp.zeros_like(tail_s)

    qk = _conv_silu(qk_ref, tail_s, ext_s, cw_ref, cb_ref)

    pre = GATE_SOFTCAP * jnp.tanh((g_ref[...] + gb_ref[...]) * (1.0 / GATE_SOFTCAP))
    logsig = jnp.minimum(pre, 0.0) - jnp.log(1.0 + jnp.exp(-jnp.abs(pre)))
    row = lax.broadcasted_iota(jnp.int32, (chunk, chunk), 0)
    col = lax.broadcasted_iota(jnp.int32, (chunk, chunk), 1)
    causal = row >= col
    tri = causal.astype(BF16)
    cum = sum(jnp.dot(tri, piece, preferred_element_type=F32) for piece in _split3(logsig))
    pre_t = pre.T
    cum_t = cum.T

    for hd in range(MLSTM_HEADS):
        g0 = 2 * hd
        q = qk[:, hd * dk:(hd + 1) * dk] * (dk ** -0.5)
        k = qk[:, (MLSTM_HEADS + hd) * dk:(MLSTM_HEADS + hd + 1) * dk]
        qb = q.astype(BF16)
        v = v_ref[:, hd * dv:(hd + 1) * dv].astype(BF16)
        li_col = pre[:, g0:g0 + 1]
        b_col = cum[:, g0 + 1:g0 + 2]
        b_last = cum[chunk - 1:chunk, g0 + 1:g0 + 2]
        r_col = li_col - b_col
        r_row = pre_t[g0:g0 + 1, :] - cum_t[g0 + 1:g0 + 2, :]

        dmat = jnp.where(causal, b_col + r_row, -jnp.inf)
        s_qk = lax.dot_general(qb, k.astype(BF16), (((1,), (1,)), ((), ())), preferred_element_type=F32)
        p = s_qk * jnp.exp(dmat)
        inter_w = jnp.exp(b_col)
        num = jnp.dot(p.astype(BF16), v, preferred_element_type=F32)
        num = num + inter_w * jnp.dot(qb, c_s[hd].astype(BF16), preferred_element_type=F32)
        den = jnp.sum(p, axis=-1, keepdims=True) + inter_w * jnp.sum(q * n_s[hd], axis=-1, keepdims=True)
        h = num / jnp.maximum(jnp.abs(den), 1.0)

        kw = k * jnp.exp(b_last + r_col)
        decay = jnp.exp(b_last)
        c_s[hd] = decay * c_s[hd] + lax.dot_general(kw.astype(BF16), v, (((0,), (0,)), ((), ())),
                                                    preferred_element_type=F32)
        n_s[hd] = decay * n_s[hd] + jnp.sum(kw, axis=0, keepdims=True)

        hn = h * lax.rsqrt(jnp.mean(h * h, axis=-1, keepdims=True) + EPS) * nrm_ref[:, hd * dv:(hd + 1) * dv]
        gate = 1.0 / (1.0 + jnp.exp(-o_ref[:, hd * dv:(hd + 1) * dv].astype(F32)))
        y_ref[:, hd * dv:(hd + 1) * dv] = (gate * hn).astype(y_ref.dtype)


def _mlstm(pm, gates, gate_bias, conv_w, conv_b, mlstm_norm):
    s = pm.shape[0]
    L = MLSTM_CHUNK
    H = MLSTM_HEADS
    dk, dv = MLSTM_DQK, MLSTM_DV
    qkw = 2 * MLSTM_QK_WIDTH
    assert qkw == MLSTM_WIDTH
    return pl.pallas_call(
        _mlstm_kernel,
        grid=(s // L,),
        in_specs=[pl.BlockSpec((L, qkw), lambda c: (c, 0)),
                  pl.BlockSpec((L, MLSTM_WIDTH), lambda c: (c, 1)),
                  pl.BlockSpec((L, MLSTM_WIDTH), lambda c: (c, 2)),
                  pl.BlockSpec((L, LANES), lambda c: (c, 0)),
                  pl.BlockSpec((1, LANES), lambda c: (0, 0)),
                  pl.BlockSpec((CONV_WIDTH, qkw), lambda c: (0, 0)),
                  pl.BlockSpec((1, qkw), lambda c: (0, 0)),
                  pl.BlockSpec((1, MLSTM_WIDTH), lambda c: (0, 0))],
        out_specs=pl.BlockSpec((L, MLSTM_WIDTH), lambda c: (c, 0)),
        out_shape=jax.ShapeDtypeStruct((s, MLSTM_WIDTH), BF16),
        scratch_shapes=[pltpu.VMEM((H, dk, dv), F32),
                        pltpu.VMEM((H, 1, dk), F32),
                        pltpu.VMEM((SUBLANES, qkw), F32),
                        pltpu.VMEM((SUBLANES + L, qkw), F32)],
        compiler_params=_params("arbitrary"),
        name="mlstm",
    )(pm, pm, pm, gates, gate_bias, conv_w, conv_b, mlstm_norm)


def _band_softmax(q, k, v, bias):
    s = lax.dot_general(q, k, (((1,), (1,)), ((), ())), preferred_element_type=F32) + bias
    m = jnp.max(s, axis=-1, keepdims=True)
    p = jnp.exp2(s - m)
    l = jnp.sum(p, axis=-1, keepdims=True)
    o = jnp.dot(p.astype(BF16), v, preferred_element_type=F32) / l
    return o, m + jnp.log2(l)


def _attn_kernel(q_ref, kp_ref, kc_ref, vp_ref, vc_ref, y_ref, q4_s, k4_s, v4_s, o_s, lse_s):
    blk = ATTN_BLOCK
    sup = q_ref.shape[0]
    nblk = sup // blk
    d1, d2 = DILATIONS[1], DILATIONS[2]
    sub = sup // d1
    sb = pl.program_id(1)
    qscale = 1.4426950408889634 * ATTN_HEAD_DIM ** -0.5

    for r in range(d1):
        q4_s[r] = q_ref[pl.ds(r, sub, stride=d1), :] * qscale
        k4_s[r, 0:sub, :] = kp_ref[pl.ds(r, sub, stride=d1), :]
        k4_s[r, sub:2 * sub, :] = kc_ref[pl.ds(r, sub, stride=d1), :]
        v4_s[r, 0:sub, :] = vp_ref[pl.ds(r, sub, stride=d1), :]
        v4_s[r, sub:2 * sub, :] = vc_ref[pl.ds(r, sub, stride=d1), :]

    qi = lax.broadcasted_iota(jnp.int32, (blk, 2 * blk), 0)
    kj = lax.broadcasted_iota(jnp.int32, (blk, 2 * blk), 1)
    band = (kj >= qi) & (kj <= qi + blk)
    bias_any = jnp.where(band, 0.0, -jnp.inf).astype(F32)
    bias_first = jnp.where(band & (kj >= blk), 0.0, -jnp.inf).astype(F32)

    def body(t, carry):
        def run(p_idx, q, k, v, out_rows, first):
            bias = jnp.where(first, bias_first, bias_any)
            o, lse = _band_softmax(q.astype(BF16), k.astype(BF16), v.astype(BF16), bias)
            o_s[p_idx, out_rows, :] = o
            lse_s[p_idx, out_rows, :] = jnp.broadcast_to(lse, (blk, LANES))

        q0 = pl.multiple_of(t * blk, blk)
        qp = pl.multiple_of(jnp.maximum(t - 1, 0) * blk, blk)
        at_start = t == 0
        k_prev = jnp.where(at_start, kp_ref[sup - blk:sup, :], kc_ref[pl.ds(qp, blk), :])
        v_prev = jnp.where(at_start, vp_ref[sup - blk:sup, :], vc_ref[pl.ds(qp, blk), :])
        run(0, q_ref[pl.ds(q0, blk), :] * qscale,
            jnp.concatenate([k_prev, kc_ref[pl.ds(q0, blk), :]], axis=0),
            jnp.concatenate([v_prev, vc_ref[pl.ds(q0, blk), :]], axis=0),
            pl.ds(q0, blk), (sb == 0) & at_start)

        sg = lax.shift_right_logical(t, d1.bit_length() - 1)
        r = t & (d1 - 1)
        s0 = pl.multiple_of(sg * blk, blk)
        kwin = pl.ds(pl.multiple_of(sub - blk + sg * blk, blk), 2 * blk)
        run(1, q4_s[r, pl.ds(s0, blk), :], k4_s[r, kwin, :], v4_s[r, kwin, :],
            pl.ds(sg * (d1 * blk) + r, blk, stride=d1), (sb == 0) & (sg == 0))

        run(2, q4_s[r, pl.ds(sg, blk, stride=d2 // d1), :], k4_s[r, pl.ds(sg, 2 * blk, stride=d2 // d1), :],
            v4_s[r, pl.ds(sg, 2 * blk, stride=d2 // d1), :], pl.ds(t, blk, stride=d2), sb == 0)
        return carry

    lax.fori_loop(0, nblk, body, 0, unroll=ATTN_UNROLL)

    def combine(t, carry):
        rows = pl.ds(pl.multiple_of(t * blk, blk), blk)
        l0, l1, l2 = lse_s[0, rows, :], lse_s[1, rows, :], lse_s[2, rows, :]
        mx = jnp.maximum(jnp.maximum(l0, l1), l2)
        w0, w1, w2 = jnp.exp2(l0 - mx), jnp.exp2(l1 - mx), jnp.exp2(l2 - mx)
        out = (w0 * o_s[0, rows, :] + w1 * o_s[1, rows, :] + w2 * o_s[2, rows, :]) / (w0 + w1 + w2)
        y_ref[rows, :] = out.astype(y_ref.dtype)
        return carry

    lax.fori_loop(0, nblk, combine, 0)


def _dilated_attention(pf, col0):
    s = pf.shape[0]
    sup = ATTN_SUPER
    hb = ATTN_HEADS
    d1 = DILATIONS[1]
    qb, kb, vb = (col0 // LANES + i * hb for i in range(3))
    prev = lambda b: jnp.maximum(b - 1, 0)
    return pl.pallas_call(
        _attn_kernel,
        grid=(hb, s // sup),
        in_specs=[pl.BlockSpec((sup, LANES), lambda h, b: (b, qb + h)),
                  pl.BlockSpec((sup, LANES), lambda h, b: (prev(b), kb + h)),
                  pl.BlockSpec((sup, LANES), lambda h, b: (b, kb + h)),
                  pl.BlockSpec((sup, LANES), lambda h, b: (prev(b), vb + h)),
                  pl.BlockSpec((sup, LANES), lambda h, b: (b, vb + h))],
        out_specs=pl.BlockSpec((sup, LANES), lambda h, b: (b, h)),
        out_shape=jax.ShapeDtypeStruct((s, ATTN_WIDTH), BF16),
        scratch_shapes=[pltpu.VMEM((d1, sup // d1, LANES), F32),
                        pltpu.VMEM((d1, 2 * sup // d1, LANES), F32),
                        pltpu.VMEM((d1, 2 * sup // d1, LANES), F32),
                        pltpu.VMEM((len(DILATIONS), sup, LANES), F32),
                        pltpu.VMEM((len(DILATIONS), sup, LANES), F32)],
        compiler_params=_params("parallel", "parallel"),
        name="dilated_attn",
    )(pf, pf, pf, pf, pf)


def _gate_weights(w_in_t, i_bias_l, f_bias_l):
    g0 = 2 * MLSTM_QK_WIDTH + 2 * MLSTM_WIDTH
    H = MLSTM_HEADS
    wg = jnp.stack([w_in_t[g0:g0 + H], w_in_t[g0 + H:g0 + 2 * H]], axis=1).reshape(2 * H, w_in_t.shape[1])
    wg = jnp.pad(wg, ((0, LANES - 2 * H), (0, 0)))
    bg = jnp.stack([i_bias_l, f_bias_l], axis=-1).reshape(1, 2 * H)
    bg = jnp.pad(bg, ((0, 0), (0, LANES - 2 * H)))
    return wg, bg.astype(F32)


def _layer(x, norm_mix, w_in, conv_w, conv_b, i_bias, f_bias, mlstm_norm, w_out, norm_mlp, w_up, w_down):
    qk_w = 2 * MLSTM_QK_WIDTH
    g0 = qk_w + 2 * MLSTM_WIDTH
    a0 = g0 + 2 * MLSTM_HEADS
    n_proj = w_in.shape[1] - 2 * MLSTM_HEADS
    assert a0 - g0 == SUBLANES
    w_in_t = w_in.T
    w_t = _cast_rows_drop_gap(w_in_t, g0, n_proj)
    w_g, b_g = _gate_weights(w_in_t, i_bias, f_bias)

    h = _rmsnorm(x, norm_mix, BF16)
    pa, w_out_b = _matmul_nt(h, w_t, F32, n=n_proj - g0, row0=g0, tn=MM_TILE_3Q, cast_w=w_out, name="in_proj_attn")
    pm, w_up_b = _matmul_nt(h, w_t, F32, n=g0, tn=MM_TILE_3Q, cast_w=w_up, name="in_proj_m")
    gates = _matmul_nt(h, w_g, F32, tn=w_g.shape[0], name="in_proj_gates")
    y_a = _dilated_attention(pa, 0)
    y_m = _mlstm(pm, gates, b_g, conv_w, conv_b.reshape(1, -1), mlstm_norm.reshape(1, -1))
    x1, x1g, ssq = _out_proj(y_m, y_a, w_out_b, x, norm_mlp)
    u, w_down_b = _up_proj(x1g, ssq, w_up_b, w_down)
    return _matmul_ktiled_res(u, w_down_b, x1)


def kernel(x, norm_mix, w_in, conv_w, conv_b, i_bias, f_bias, mlstm_norm, w_out, norm_mlp, w_up, w_down, norm_final):
    b, s, d = x.shape
    outs = []
    for bi in range(b):
        xb = x[bi]
        for layer in range(norm_mix.shape[0]):
            xb = _layer(xb, norm_mix[layer], w_in[layer], conv_w[layer], conv_b[layer], i_bias[layer], f_bias[layer],
                        mlstm_norm[layer], w_out[layer], norm_mlp[layer], w_up[layer], w_down[layer])
        outs.append(_rmsnorm(xb, norm_final, x.dtype))
    return jnp.stack(outs, axis=0)
```

```python
import functools

import jax
import jax.numpy as jnp
from jax import lax
from jax.experimental import pallas as pl
from jax.experimental.pallas import tpu as pltpu

F32 = jnp.float32
BF16 = jnp.bfloat16

D_MODEL = 4096
MLSTM_HEADS = 4
MLSTM_DV = 512
MLSTM_DQK = 256
MLSTM_QK_WIDTH = MLSTM_HEADS * MLSTM_DQK
MLSTM_WIDTH = MLSTM_HEADS * MLSTM_DV
CONV_WIDTH = 4
GATE_SOFTCAP = 15.0
ATTN_HEAD_DIM = 128
ATTN_HEADS = 16
ATTN_WIDTH = ATTN_HEADS * ATTN_HEAD_DIM
ATTN_BLOCK = 128
DILATIONS = (1, 4, 16)
D_FF = 4 * D_MODEL
EPS = 1e-6

LANES = 128
SUBLANES = 8
VMEM_LIMIT_BYTES = 60 * 1024 * 1024

MM_TILE = 1024
MM_TILE_3Q = 768
MM_K_TILE = 4096
NORM_ROWS = 512
MLSTM_CHUNK = 256
ATTN_SUPER = DILATIONS[-1] * ATTN_BLOCK
ATTN_UNROLL = 8


def _params(*sem):
    return pltpu.CompilerParams(dimension_semantics=sem, vmem_limit_bytes=VMEM_LIMIT_BYTES)


def _rmsnorm_kernel(x_ref, g_ref, o_ref):
    x = x_ref[...]
    ms = jnp.mean(x * x, axis=-1, keepdims=True)
    o_ref[...] = (x * lax.rsqrt(ms + EPS) * g_ref[...]).astype(o_ref.dtype)


def _rmsnorm(x, g, out_dtype):
    s, d = x.shape
    return pl.pallas_call(
        _rmsnorm_kernel,
        grid=(s // NORM_ROWS,),
        in_specs=[pl.BlockSpec((NORM_ROWS, d), lambda i: (i, 0)),
                  pl.BlockSpec((1, d), lambda i: (0, 0))],
        out_specs=pl.BlockSpec((NORM_ROWS, d), lambda i: (i, 0)),
        out_shape=jax.ShapeDtypeStruct((s, d), out_dtype),
        compiler_params=_params("parallel"),
        name="rmsnorm",
    )(x, g.reshape(1, d))


def _cast_rows_kernel(w_ref, nxt_ref, o_ref, *, first_shifted):
    i = pl.program_id(0)

    @pl.when(i < first_shifted)
    def _():
        o_ref[...] = w_ref[...].astype(o_ref.dtype)

    @pl.when(i >= first_shifted)
    def _():
        rows = w_ref.shape[0]
        shifted = jnp.concatenate([w_ref[SUBLANES:rows, :], nxt_ref[...]], axis=0)
        o_ref[...] = shifted.astype(o_ref.dtype)


def _cast_rows_drop_gap(wt, gap_row, n_out):
    k = wt.shape[1]
    rows = NORM_ROWS
    assert gap_row % rows == 0 and n_out % rows == 0 and wt.shape[0] == n_out + SUBLANES
    per = rows // SUBLANES
    return pl.pallas_call(
        functools.partial(_cast_rows_kernel, first_shifted=gap_row // rows),
        grid=(n_out // rows,),
        in_specs=[pl.BlockSpec((rows, k), lambda i: (i, 0)),
                  pl.BlockSpec((SUBLANES, k), lambda i: ((i + 1) * per, 0))],
        out_specs=pl.BlockSpec((rows, k), lambda i: (i, 0)),
        out_shape=jax.ShapeDtypeStruct((n_out, k), BF16),
        compiler_params=_params("parallel"),
        name="cast_w_in",
    )(wt, wt)


def _side_cast(w, grid):
    steps = 1
    for g in grid:
        steps *= g
    rows = w.shape[0] // steps
    assert rows * steps == w.shape[0] and rows % (2 * SUBLANES) == 0

    def index_map(*ids):
        lin = ids[0]
        for g, i in zip(grid[1:], ids[1:]):
            lin = lin * g + i
        return (lin, 0)

    spec = pl.BlockSpec((rows, w.shape[1]), index_map)
    return spec, jax.ShapeDtypeStruct(w.shape, BF16)


def _mm_nt_kernel(a_ref, bt_ref, *refs, n_casts):
    wsrc_refs, o_ref, wdst_refs = refs[:n_casts], refs[n_casts], refs[n_casts + 1:]
    for wsrc_ref, wdst_ref in zip(wsrc_refs, wdst_refs):
        wdst_ref[...] = wsrc_ref[...].astype(wdst_ref.dtype)
    acc = lax.dot_general(a_ref[...], bt_ref[...].astype(BF16), (((1,), (1,)), ((), ())),
                          preferred_element_type=F32)
    o_ref[...] = acc.astype(o_ref.dtype)


def _matmul_nt(a, bt, out_dtype, *, n=None, row0=0, tn=MM_TILE, cast_ws=(), name="matmul"):
    m, k = a.shape
    n = bt.shape[0] if n is None else n
    tm = MM_TILE
    jb = row0 // tn
    assert jb * tn == row0
    grid = (m // tm, n // tn)
    cspecs, cshapes = zip(*(_side_cast(w, grid) for w in cast_ws)) if cast_ws else ((), ())
    return pl.pallas_call(
        functools.partial(_mm_nt_kernel, n_casts=len(cast_ws)),
        grid=grid,
        in_specs=[pl.BlockSpec((tm, k), lambda i, j: (i, 0)),
                  pl.BlockSpec((tn, k), lambda i, j: (jb + j, 0)), *cspecs],
        out_specs=[pl.BlockSpec((tm, tn), lambda i, j: (i, j)), *cspecs],
        out_shape=[jax.ShapeDtypeStruct((m, n), out_dtype), *cshapes],
        compiler_params=_params("parallel", "parallel"),
        name=name,
    )(a, bt, *cast_ws)


def _up_proj_kernel(a_ref, ssq_ref, b_ref, wsrc_ref, o_ref, wdst_ref, *, d_norm):
    wdst_ref[...] = wsrc_ref[...].astype(wdst_ref.dtype)
    acc = jnp.dot(a_ref[...], b_ref[...], preferred_element_type=F32)
    acc = acc * lax.rsqrt(ssq_ref[:, 0:1] * (1.0 / d_norm) + EPS)
    acc = jnp.maximum(acc, 0.0)
    o_ref[...] = (acc * acc).astype(o_ref.dtype)


def _up_proj(xg, ssq, w, cast_w):
    m, k = xg.shape
    n = w.shape[1]
    tm = tn = MM_TILE
    grid = (m // tm, n // tn)
    cspec, cshape = _side_cast(cast_w, grid)
    return pl.pallas_call(
        functools.partial(_up_proj_kernel, d_norm=k),
        grid=grid,
        in_specs=[pl.BlockSpec((tm, k), lambda i, j: (i, 0)),
                  pl.BlockSpec((tm, LANES), lambda i, j: (i, 0)),
                  pl.BlockSpec((k, tn), lambda i, j: (0, j)),
                  cspec],
        out_specs=[pl.BlockSpec((tm, tn), lambda i, j: (i, j)), cspec],
        out_shape=[jax.ShapeDtypeStruct((m, n), BF16), cshape],
        compiler_params=_params("parallel", "parallel"),
        name="up_proj",
    )(xg, ssq, w, cast_w)


def _out_proj_kernel(a0_ref, a1_ref, b0_ref, b1_ref, r_ref, g_ref, o_ref, xg_ref, ssq_ref):
    acc = jnp.dot(a0_ref[...], b0_ref[...], preferred_element_type=F32)
    acc = acc + jnp.dot(a1_ref[...], b1_ref[...], preferred_element_type=F32)
    x1 = r_ref[...] + acc
    o_ref[...] = x1
    xg_ref[...] = (x1 * g_ref[...]).astype(xg_ref.dtype)
    part = jnp.broadcast_to(jnp.sum(x1 * x1, axis=-1, keepdims=True), ssq_ref.shape)
    j = pl.program_id(1)

    @pl.when(j == 0)
    def _():
        ssq_ref[...] = part

    @pl.when(j > 0)
    def _():
        ssq_ref[...] += part


def _out_proj(y_m, y_a, w, res, gain):
    m, k0 = y_m.shape
    k1 = y_a.shape[1]
    assert k0 == k1
    n = w.shape[1]
    tm, tn = MM_TILE, MM_TILE // 2
    return pl.pallas_call(
        _out_proj_kernel,
        grid=(m // tm, n // tn),
        in_specs=[pl.BlockSpec((tm, k0), lambda i, j: (i, 0)),
                  pl.BlockSpec((tm, k1), lambda i, j: (i, 0)),
                  pl.BlockSpec((k0, tn), lambda i, j: (0, j)),
                  pl.BlockSpec((k1, tn), lambda i, j: (1, j)),
                  pl.BlockSpec((tm, tn), lambda i, j: (i, j)),
                  pl.BlockSpec((1, tn), lambda i, j: (0, j))],
        out_specs=[pl.BlockSpec((tm, tn), lambda i, j: (i, j)),
                   pl.BlockSpec((tm, tn), lambda i, j: (i, j)),
                   pl.BlockSpec((tm, LANES), lambda i, j: (i, 0))],
        out_shape=[jax.ShapeDtypeStruct((m, n), F32),
                   jax.ShapeDtypeStruct((m, n), BF16),
                   jax.ShapeDtypeStruct((m, LANES), F32)],
        compiler_params=_params("parallel", "arbitrary"),
        name="out_proj",
    )(y_m, y_a, w, w, res, gain.reshape(1, n))


def _mmk_res_kernel(a_ref, b_ref, r_ref, o_ref):
    @pl.when(pl.program_id(2) == 0)
    def _():
        o_ref[...] = r_ref[...]

    o_ref[...] += jnp.dot(a_ref[...], b_ref[...], preferred_element_type=F32)


def _matmul_ktiled_res(a, b, res):
    m, k = a.shape
    n = b.shape[1]
    tm = tn = MM_TILE
    tk = MM_K_TILE
    return pl.pallas_call(
        _mmk_res_kernel,
        grid=(m // tm, n // tn, k // tk),
        in_specs=[pl.BlockSpec((tm, tk), lambda i, j, kk: (i, kk)),
                  pl.BlockSpec((tk, tn), lambda i, j, kk: (kk, j)),
                  pl.BlockSpec((tm, tn), lambda i, j, kk: (i, j))],
        out_specs=pl.BlockSpec((tm, tn), lambda i, j, kk: (i, j)),
        out_shape=jax.ShapeDtypeStruct((m, n), F32),
        compiler_params=_params("parallel", "parallel", "arbitrary"),
        name="down_proj",
    )(a, b, res)


def _split3(a):
    hi = a.astype(BF16)
    r1 = a - hi.astype(F32)
    mid = r1.astype(BF16)
    lo = (r1 - mid.astype(F32)).astype(BF16)
    return hi, mid, lo


def _conv_silu(u_ref, tail_ref, ext_ref, w_ref, b_ref):
    chunk = u_ref.shape[0]
    u = u_ref[...]
    ext_ref[0:SUBLANES, :] = tail_ref[...]
    ext_ref[SUBLANES:SUBLANES + chunk, :] = u
    tail_ref[...] = u[chunk - SUBLANES:, :]
    w = w_ref[...]
    y = b_ref[...] + w[CONV_WIDTH - 1:CONV_WIDTH, :] * u
    for j in range(CONV_WIDTH - 1):
        start = SUBLANES - (CONV_WIDTH - 1) + j
        y = y + w[j:j + 1, :] * ext_ref[start:start + chunk, :]
    return y * (1.0 / (1.0 + jnp.exp(-y)))


def _mlstm_kernel(qk_ref, v_ref, o_ref, g_ref, gb_ref, cw_ref, cb_ref, nrm_ref, y_ref, c_s, n_s, tail_s, ext_s):
    chunk = qk_ref.shape[0]
    dk, dv = MLSTM_DQK, MLSTM_DV

    @pl.when(pl.program_id(0) == 0)
    def _():
        c_s[...] = jnp.zeros_like(c_s)
        n_s[...] = jnp.zeros_like(n_s)
        tail_s[...] = jnp.zeros_like(tail_s)

    qk = _conv_silu(qk_ref, tail_s, ext_s, cw_ref, cb_ref)

    pre = GATE_SOFTCAP * jnp.tanh((g_ref[...] + gb_ref[...]) * (1.0 / GATE_SOFTCAP))
    logsig = jnp.minimum(pre, 0.0) - jnp.log(1.0 + jnp.exp(-jnp.abs(pre)))
    row = lax.broadcasted_iota(jnp.int32, (chunk, chunk), 0)
    col = lax.broadcasted_iota(jnp.int32, (chunk, chunk), 1)
    causal = row >= col
    tri = causal.astype(BF16)
    cum = sum(jnp.dot(tri, piece, preferred_element_type=F32) for piece in _split3(logsig))
    pre_t = pre.T
    cum_t = cum.T

    for hd in range(MLSTM_HEADS):
        g0 = 2 * hd
        q = qk[:, hd * dk:(hd + 1) * dk] * (dk ** -0.5)
        k = qk[:, (MLSTM_HEADS + hd) * dk:(MLSTM_HEADS + hd + 1) * dk]
        qb = q.astype(BF16)
        v = v_ref[:, hd * dv:(hd + 1) * dv]
        li_col = pre[:, g0:g0 + 1]
        b_col = cum[:, g0 + 1:g0 + 2]
        b_last = cum[chunk - 1:chunk, g0 + 1:g0 + 2]
        r_col = li_col - b_col
        r_row = pre_t[g0:g0 + 1, :] - cum_t[g0 + 1:g0 + 2, :]

        dmat = jnp.where(causal, b_col + r_row, -jnp.inf)
        s_qk = lax.dot_general(qb, k.astype(BF16), (((1,), (1,)), ((), ())), preferred_element_type=F32)
        p = s_qk * jnp.exp(dmat)
        inter_w = jnp.exp(b_col)
        num = jnp.dot(p.astype(BF16), v, preferred_element_type=F32)
        num = num + inter_w * jnp.dot(qb, c_s[hd].astype(BF16), preferred_element_type=F32)
        den = jnp.sum(p, axis=-1, keepdims=True) + inter_w * jnp.sum(q * n_s[hd], axis=-1, keepdims=True)
        h = num / jnp.maximum(jnp.abs(den), 1.0)

        kw = k * jnp.exp(b_last + r_col)
        decay = jnp.exp(b_last)
        c_s[hd] = decay * c_s[hd] + lax.dot_general(kw.astype(BF16), v, (((0,), (0,)), ((), ())),
                                                    preferred_element_type=F32)
        n_s[hd] = decay * n_s[hd] + jnp.sum(kw, axis=0, keepdims=True)

        hn = h * lax.rsqrt(jnp.mean(h * h, axis=-1, keepdims=True) + EPS) * nrm_ref[:, hd * dv:(hd + 1) * dv]
        gate = 1.0 / (1.0 + jnp.exp(-o_ref[:, hd * dv:(hd + 1) * dv].astype(F32)))
        y_ref[:, hd * dv:(hd + 1) * dv] = (gate * hn).astype(y_ref.dtype)


def _mlstm(pqk, pvo, gates, gate_bias, conv_w, conv_b, mlstm_norm):
    s = pqk.shape[0]
    L = MLSTM_CHUNK
    H = MLSTM_HEADS
    dk, dv = MLSTM_DQK, MLSTM_DV
    qkw = 2 * MLSTM_QK_WIDTH
    return pl.pallas_call(
        _mlstm_kernel,
        grid=(s // L,),
        in_specs=[pl.BlockSpec((L, qkw), lambda c: (c, 0)),
                  pl.BlockSpec((L, MLSTM_WIDTH), lambda c: (c, 0)),
                  pl.BlockSpec((L, MLSTM_WIDTH), lambda c: (c, 1)),
                  pl.BlockSpec((L, LANES), lambda c: (c, 0)),
                  pl.BlockSpec((1, LANES), lambda c: (0, 0)),
                  pl.BlockSpec((CONV_WIDTH, qkw), lambda c: (0, 0)),
                  pl.BlockSpec((1, qkw), lambda c: (0, 0)),
                  pl.BlockSpec((1, MLSTM_WIDTH), lambda c: (0, 0))],
        out_specs=pl.BlockSpec((L, MLSTM_WIDTH), lambda c: (c, 0)),
        out_shape=jax.ShapeDtypeStruct((s, MLSTM_WIDTH), BF16),
        scratch_shapes=[pltpu.VMEM((H, dk, dv), F32),
                        pltpu.VMEM((H, 1, dk), F32),
                        pltpu.VMEM((SUBLANES, qkw), F32),
                        pltpu.VMEM((SUBLANES + L, qkw), F32)],
        compiler_params=_params("arbitrary"),
        name="mlstm",
    )(pqk, pvo, pvo, gates, gate_bias, conv_w, conv_b, mlstm_norm)


def _band_softmax(q, k, v, bias):
    s = lax.dot_general(q, k, (((1,), (1,)), ((), ())), preferred_element_type=F32) + bias
    m = jnp.max(s, axis=-1, keepdims=True)
    p = jnp.exp2(s - m)
    l = jnp.sum(p, axis=-1, keepdims=True)
    o = jnp.dot(p.astype(BF16), v, preferred_element_type=F32) / l
    return o, m + jnp.log2(l)


def _attn_kernel(q_ref, kp_ref, kc_ref, vp_ref, vc_ref, y_ref, q4_s, k4_s, v4_s, o_s, lse_s, bias_s):
    blk = ATTN_BLOCK
    sup = q_ref.shape[0]
    nblk = sup // blk
    d1, d2 = DILATIONS[1], DILATIONS[2]
    sub = sup // d1
    sb = pl.program_id(1)
    qscale = 1.4426950408889634 * ATTN_HEAD_DIM ** -0.5

    for r in range(d1):
        q4_s[r] = q_ref[pl.ds(r, sub, stride=d1), :] * qscale
        k4_s[r, 0:sub, :] = kp_ref[pl.ds(r, sub, stride=d1), :]
        k4_s[r, sub:2 * sub, :] = kc_ref[pl.ds(r, sub, stride=d1), :]
        v4_s[r, 0:sub, :] = vp_ref[pl.ds(r, sub, stride=d1), :]
        v4_s[r, sub:2 * sub, :] = vc_ref[pl.ds(r, sub, stride=d1), :]

    qi = lax.broadcasted_iota(jnp.int32, (blk, 2 * blk), 0)
    kj = lax.broadcasted_iota(jnp.int32, (blk, 2 * blk), 1)
    band = (kj >= qi) & (kj <= qi + blk)
    bias_s[0] = jnp.where(band, 0.0, -jnp.inf).astype(F32)
    bias_s[1] = jnp.where(band & (kj >= blk), 0.0, -jnp.inf).astype(F32)

    def body(t, carry):
        def run(p_idx, q, k, v, out_rows, first):
            bias = bias_s[first.astype(jnp.int32)]
            o, lse = _band_softmax(q.astype(BF16), k.astype(BF16), v.astype(BF16), bias)
            o_s[p_idx, out_rows, :] = o
            lse_s[p_idx, out_rows, :] = jnp.broadcast_to(lse, (blk, LANES))

        q0 = pl.multiple_of(t * blk, blk)
        qp = pl.multiple_of(jnp.maximum(t - 1, 0) * blk, blk)
        at_start = t == 0
        k_prev = jnp.where(at_start, kp_ref[sup - blk:sup, :], kc_ref[pl.ds(qp, blk), :])
        v_prev = jnp.where(at_start, vp_ref[sup - blk:sup, :], vc_ref[pl.ds(qp, blk), :])
        run(0, q_ref[pl.ds(q0, blk), :] * qscale,
            jnp.concatenate([k_prev, kc_ref[pl.ds(q0, blk), :]], axis=0),
            jnp.concatenate([v_prev, vc_ref[pl.ds(q0, blk), :]], axis=0),
            pl.ds(q0, blk), (sb == 0) & at_start)

        sg = lax.shift_right_logical(t, d1.bit_length() - 1)
        r = t & (d1 - 1)
        s0 = pl.multiple_of(sg * blk, blk)
        kwin = pl.ds(pl.multiple_of(sub - blk + sg * blk, blk), 2 * blk)
        run(1, q4_s[r, pl.ds(s0, blk), :], k4_s[r, kwin, :], v4_s[r, kwin, :],
            pl.ds(sg * (d1 * blk) + r, blk, stride=d1), (sb == 0) & (sg == 0))

        run(2, q4_s[r, pl.ds(sg, blk, stride=d2 // d1), :], k4_s[r, pl.ds(sg, 2 * blk, stride=d2 // d1), :],
            v4_s[r, pl.ds(sg, 2 * blk, stride=d2 // d1), :], pl.ds(t, blk, stride=d2), sb == 0)
        return carry

    lax.fori_loop(0, nblk, body, 0, unroll=ATTN_UNROLL)

    def combine(t, carry):
        rows = pl.ds(pl.multiple_of(t * blk, blk), blk)
        l0, l1, l2 = lse_s[0, rows, :], lse_s[1, rows, :], lse_s[2, rows, :]
        mx = jnp.maximum(jnp.maximum(l0, l1), l2)
        w0, w1, w2 = jnp.exp2(l0 - mx), jnp.exp2(l1 - mx), jnp.exp2(l2 - mx)
        out = (w0 * o_s[0, rows, :] + w1 * o_s[1, rows, :] + w2 * o_s[2, rows, :]) / (w0 + w1 + w2)
        y_ref[rows, :] = out.astype(y_ref.dtype)
        return carry

    lax.fori_loop(0, nblk, combine, 0)


def _dilated_attention(pf, col0):
    s = pf.shape[0]
    sup = ATTN_SUPER
    hb = ATTN_HEADS
    d1 = DILATIONS[1]
    qb, kb, vb = (col0 // LANES + i * hb for i in range(3))
    prev = lambda b: jnp.maximum(b - 1, 0)
    return pl.pallas_call(
        _attn_kernel,
        grid=(hb, s // sup),
        in_specs=[pl.BlockSpec((sup, LANES), lambda h, b: (b, qb + h)),
                  pl.BlockSpec((sup, LANES), lambda h, b: (prev(b), kb + h)),
                  pl.BlockSpec((sup, LANES), lambda h, b: (b, kb + h)),
                  pl.BlockSpec((sup, LANES), lambda h, b: (prev(b), vb + h)),
                  pl.BlockSpec((sup, LANES), lambda h, b: (b, vb + h))],
        out_specs=pl.BlockSpec((sup, LANES), lambda h, b: (b, h)),
        out_shape=jax.ShapeDtypeStruct((s, ATTN_WIDTH), BF16),
        scratch_shapes=[pltpu.VMEM((d1, sup // d1, LANES), F32),
                        pltpu.VMEM((d1, 2 * sup // d1, LANES), F32),
                        pltpu.VMEM((d1, 2 * sup // d1, LANES), F32),
                        pltpu.VMEM((len(DILATIONS), sup, LANES), F32),
                        pltpu.VMEM((len(DILATIONS), sup, LANES), F32),
                        pltpu.VMEM((2, ATTN_BLOCK, 2 * ATTN_BLOCK), F32)],
        compiler_params=_params("parallel", "parallel"),
        name="dilated_attn",
    )(pf, pf, pf, pf, pf)


def _gate_weights(w_in_t, i_bias_l, f_bias_l):
    g0 = 2 * MLSTM_QK_WIDTH + 2 * MLSTM_WIDTH
    H = MLSTM_HEADS
    wg = jnp.stack([w_in_t[g0:g0 + H], w_in_t[g0 + H:g0 + 2 * H]], axis=1).reshape(2 * H, w_in_t.shape[1])
    wg = jnp.pad(wg, ((0, LANES - 2 * H), (0, 0)))
    bg = jnp.stack([i_bias_l, f_bias_l], axis=-1).reshape(1, 2 * H)
    bg = jnp.pad(bg, ((0, 0), (0, LANES - 2 * H)))
    return wg, bg.astype(F32)


def _layer(x, norm_mix, w_in, conv_w, conv_b, i_bias, f_bias, mlstm_norm, w_out, norm_mlp, w_up, w_down):
    qk_w = 2 * MLSTM_QK_WIDTH
    g0 = qk_w + 2 * MLSTM_WIDTH
    a0 = g0 + 2 * MLSTM_HEADS
    n_proj = w_in.shape[1] - 2 * MLSTM_HEADS
    assert a0 - g0 == SUBLANES
    w_in_t = w_in.T
    w_t = _cast_rows_drop_gap(w_in_t, g0, n_proj)
    w_g, b_g = _gate_weights(w_in_t, i_bias, f_bias)

    h = _rmsnorm(x, norm_mix, BF16)
    pa, w_out_b, w_up_b = _matmul_nt(h, w_t, F32, n=n_proj - g0, row0=g0, tn=MM_TILE_3Q, cast_ws=(w_out, w_up),
                                     name="in_proj_attn")
    pqk, = _matmul_nt(h, w_t, F32, n=qk_w, name="in_proj_qk")
    pvo, = _matmul_nt(h, w_t, BF16, n=g0 - qk_w, row0=qk_w, name="in_proj_vo")
    gates, = _matmul_nt(h, w_g, F32, tn=w_g.shape[0], name="in_proj_gates")
    y_a = _dilated_attention(pa, 0)
    y_m = _mlstm(pqk, pvo, gates, b_g, conv_w, conv_b.reshape(1, -1), mlstm_norm.reshape(1, -1))
    x1, x1g, ssq = _out_proj(y_m, y_a, w_out_b, x, norm_mlp)
    u, w_down_b = _up_proj(x1g, ssq, w_up_b, w_down)
    return _matmul_ktiled_res(u, w_down_b, x1)


def kernel(x, norm_mix, w_in, conv_w, conv_b, i_bias, f_bias, mlstm_norm, w_out, norm_mlp, w_up, w_down, norm_final):
    b, s, d = x.shape
    outs = []
    for bi in range(b):
        xb = x[bi]
        for layer in range(norm_mix.shape[0]):
            xb = _layer(xb, norm_mix[layer], w_in[layer], conv_w[layer], conv_b[layer], i_bias[layer], f_bias[layer],
                        mlstm_norm[layer], w_out[layer], norm_mlp[layer], w_up[layer], w_down[layer])
        outs.append(_rmsnorm(xb, norm_final, x.dtype))
    return jnp.stack(outs, axis=0)
```

```python
import functools

import jax
import jax.numpy as jnp
from jax import lax
from jax.experimental import pallas as pl
from jax.experimental.pallas import tpu as pltpu

F32 = jnp.float32
BF16 = jnp.bfloat16

D_MODEL = 4096
MLSTM_HEADS = 4
MLSTM_DV = 512
MLSTM_DQK = 256
MLSTM_QK_WIDTH = MLSTM_HEADS * MLSTM_DQK
MLSTM_WIDTH = MLSTM_HEADS * MLSTM_DV
CONV_WIDTH = 4
GATE_SOFTCAP = 15.0
ATTN_HEAD_DIM = 128
ATTN_HEADS = 16
ATTN_WIDTH = ATTN_HEADS * ATTN_HEAD_DIM
ATTN_BLOCK = 128
DILATIONS = (1, 4, 16)
D_FF = 4 * D_MODEL
EPS = 1e-6

LANES = 128
SUBLANES = 8
VMEM_LIMIT_BYTES = 58 * 1024 * 1024

MM_TILE = 1024
MM_K_TILE = 4096
NORM_ROWS = 512
MLSTM_CHUNK = 256
ATTN_SUPER = DILATIONS[-1] * ATTN_BLOCK
ATTN_UNROLL = 8


def _params(*sem):
    return pltpu.CompilerParams(dimension_semantics=sem, vmem_limit_bytes=VMEM_LIMIT_BYTES)


def _rmsnorm_kernel(x_ref, g_ref, o_ref):
    x = x_ref[...]
    ms = jnp.mean(x * x, axis=-1, keepdims=True)
    o_ref[...] = (x * lax.rsqrt(ms + EPS) * g_ref[...]).astype(o_ref.dtype)


def _rmsnorm(x, g, out_dtype):
    s, d = x.shape
    return pl.pallas_call(
        _rmsnorm_kernel,
        grid=(s // NORM_ROWS,),
        in_specs=[pl.BlockSpec((NORM_ROWS, d), lambda i: (i, 0)),
                  pl.BlockSpec((1, d), lambda i: (0, 0))],
        out_specs=pl.BlockSpec((NORM_ROWS, d), lambda i: (i, 0)),
        out_shape=jax.ShapeDtypeStruct((s, d), out_dtype),
        compiler_params=_params("parallel"),
        name="rmsnorm",
    )(x, g.reshape(1, d))


def _side_cast(w, grid):
    steps = 1
    for g in grid:
        steps *= g
    rows = w.shape[0] // steps
    assert rows * steps == w.shape[0] and rows % (2 * SUBLANES) == 0

    def index_map(*ids):
        lin = ids[0]
        for g, i in zip(grid[1:], ids[1:]):
            lin = lin * g + i
        return (lin, 0)

    spec = pl.BlockSpec((rows, w.shape[1]), index_map)
    return spec, jax.ShapeDtypeStruct(w.shape, BF16)


def _mm_nt_kernel(a_ref, bt_ref, *refs, n_casts):
    wsrc_refs, o_ref, wdst_refs = refs[:n_casts], refs[n_casts], refs[n_casts + 1:]
    for wsrc_ref, wdst_ref in zip(wsrc_refs, wdst_refs):
        wdst_ref[...] = wsrc_ref[...].astype(wdst_ref.dtype)
    acc = lax.dot_general(a_ref[...], bt_ref[...].astype(BF16), (((1,), (1,)), ((), ())),
                          preferred_element_type=F32)
    o_ref[...] = acc.astype(o_ref.dtype)


def _matmul_nt(a, bt, out_dtype, *, n=None, row0=0, tn=MM_TILE, cast_ws=(), name="matmul"):
    m, k = a.shape
    n = bt.shape[0] if n is None else n
    tm = MM_TILE
    jb = row0 // tn
    assert jb * tn == row0
    grid = (m // tm, n // tn)
    cspecs, cshapes = zip(*(_side_cast(w, grid) for w in cast_ws)) if cast_ws else ((), ())
    return pl.pallas_call(
        functools.partial(_mm_nt_kernel, n_casts=len(cast_ws)),
        grid=grid,
        in_specs=[pl.BlockSpec((tm, k), lambda i, j: (i, 0)),
                  pl.BlockSpec((tn, k), lambda i, j: (jb + j, 0)), *cspecs],
        out_specs=[pl.BlockSpec((tm, tn), lambda i, j: (i, j)), *cspecs],
        out_shape=[jax.ShapeDtypeStruct((m, n), out_dtype), *cshapes],
        compiler_params=_params("parallel", "parallel"),
        name=name,
    )(a, bt, *cast_ws)


def _in_proj_kernel(a_ref, w_ref, nxt_ref, o_ref, wb_s, *, first_shifted):
    j = pl.program_id(0)
    first_row_tile = pl.program_id(1) == 0

    @pl.when(first_row_tile & (j < first_shifted))
    def _():
        wb_s[...] = w_ref[...].astype(BF16)

    @pl.when(first_row_tile & (j >= first_shifted))
    def _():
        tn = w_ref.shape[0]
        body = tn - 2 * SUBLANES
        wb_s[0:body, :] = w_ref[SUBLANES:SUBLANES + body, :].astype(BF16)
        wb_s[body:tn, :] = jnp.concatenate([w_ref[tn - SUBLANES:tn, :], nxt_ref[...]], axis=0).astype(BF16)

    o_ref[...] = lax.dot_general(a_ref[...], wb_s[...], (((1,), (1,)), ((), ())), preferred_element_type=F32)


def _in_proj(a, wt, gap_row, n_out):
    m, k = a.shape
    tm, tn = MM_TILE, MM_TILE // 2
    assert gap_row % tn == 0 and n_out % tn == 0 and wt.shape[0] == n_out + SUBLANES
    per = tn // SUBLANES
    return pl.pallas_call(
        functools.partial(_in_proj_kernel, first_shifted=gap_row // tn),
        grid=(n_out // tn, m // tm),
        in_specs=[pl.BlockSpec((tm, k), lambda j, i: (i, 0)),
                  pl.BlockSpec((tn, k), lambda j, i: (j, 0)),
                  pl.BlockSpec((SUBLANES, k), lambda j, i: ((j + 1) * per, 0))],
        out_specs=pl.BlockSpec((tm, tn), lambda j, i: (i, j)),
        out_shape=jax.ShapeDtypeStruct((m, n_out), F32),
        scratch_shapes=[pltpu.VMEM((tn, k), BF16)],
        compiler_params=_params("parallel", "arbitrary"),
        name="in_proj",
    )(a, wt, wt)


def _up_proj_kernel(a_ref, ssq_ref, b_ref, wsrc_ref, o_ref, wdst_ref, *, d_norm):
    wdst_ref[...] = wsrc_ref[...].astype(wdst_ref.dtype)
    acc = jnp.dot(a_ref[...], b_ref[...], preferred_element_type=F32)
    acc = acc * lax.rsqrt(ssq_ref[:, 0:1] * (1.0 / d_norm) + EPS)
    acc = jnp.maximum(acc, 0.0)
    o_ref[...] = (acc * acc).astype(o_ref.dtype)


def _up_proj(xg, ssq, w, cast_w):
    m, k = xg.shape
    n = w.shape[1]
    tm = tn = MM_TILE
    grid = (m // tm, n // tn)
    cspec, cshape = _side_cast(cast_w, grid)
    return pl.pallas_call(
        functools.partial(_up_proj_kernel, d_norm=k),
        grid=grid,
        in_specs=[pl.BlockSpec((tm, k), lambda i, j: (i, 0)),
                  pl.BlockSpec((tm, LANES), lambda i, j: (i, 0)),
                  pl.BlockSpec((k, tn), lambda i, j: (0, j)),
                  cspec],
        out_specs=[pl.BlockSpec((tm, tn), lambda i, j: (i, j)), cspec],
        out_shape=[jax.ShapeDtypeStruct((m, n), BF16), cshape],
        compiler_params=_params("parallel", "parallel"),
        name="up_proj",
    )(xg, ssq, w, cast_w)


def _out_proj_kernel(a0_ref, a1_ref, b0_ref, b1_ref, r_ref, g_ref, o_ref, xg_ref, ssq_ref):
    acc = jnp.dot(a0_ref[...], b0_ref[...], preferred_element_type=F32)
    acc = acc + jnp.dot(a1_ref[...], b1_ref[...], preferred_element_type=F32)
    x1 = r_ref[...] + acc
    o_ref[...] = x1
    xg_ref[...] = (x1 * g_ref[...]).astype(xg_ref.dtype)
    part = jnp.broadcast_to(jnp.sum(x1 * x1, axis=-1, keepdims=True), ssq_ref.shape)
    j = pl.program_id(1)

    @pl.when(j == 0)
    def _():
        ssq_ref[...] = part

    @pl.when(j > 0)
    def _():
        ssq_ref[...] += part


def _out_proj(y_m, y_a, w, res, gain):
    m, k0 = y_m.shape
    k1 = y_a.shape[1]
    assert k0 == k1
    n = w.shape[1]
    tm, tn = MM_TILE, MM_TILE // 2
    return pl.pallas_call(
        _out_proj_kernel,
        grid=(m // tm, n // tn),
        in_specs=[pl.BlockSpec((tm, k0), lambda i, j: (i, 0)),
                  pl.BlockSpec((tm, k1), lambda i, j: (i, 0)),
                  pl.BlockSpec((k0, tn), lambda i, j: (0, j)),
                  pl.BlockSpec((k1, tn), lambda i, j: (1, j)),
                  pl.BlockSpec((tm, tn), lambda i, j: (i, j)),
                  pl.BlockSpec((1, tn), lambda i, j: (0, j))],
        out_specs=[pl.BlockSpec((tm, tn), lambda i, j: (i, j)),
                   pl.BlockSpec((tm, tn), lambda i, j: (i, j)),
                   pl.BlockSpec((tm, LANES), lambda i, j: (i, 0))],
        out_shape=[jax.ShapeDtypeStruct((m, n), F32),
                   jax.ShapeDtypeStruct((m, n), BF16),
                   jax.ShapeDtypeStruct((m, LANES), F32)],
        compiler_params=_params("parallel", "arbitrary"),
        name="out_proj",
    )(y_m, y_a, w, w, res, gain.reshape(1, n))


def _mmk_res_kernel(a_ref, b_ref, r_ref, o_ref):
    @pl.when(pl.program_id(2) == 0)
    def _():
        o_ref[...] = r_ref[...]

    o_ref[...] += jnp.dot(a_ref[...], b_ref[...], preferred_element_type=F32)


def _matmul_ktiled_res(a, b, res):
    m, k = a.shape
    n = b.shape[1]
    tm = tn = MM_TILE
    tk = MM_K_TILE
    return pl.pallas_call(
        _mmk_res_kernel,
        grid=(m // tm, n // tn, k // tk),
        in_specs=[pl.BlockSpec((tm, tk), lambda i, j, kk: (i, kk)),
                  pl.BlockSpec((tk, tn), lambda i, j, kk: (kk, j)),
                  pl.BlockSpec((tm, tn), lambda i, j, kk: (i, j))],
        out_specs=pl.BlockSpec((tm, tn), lambda i, j, kk: (i, j)),
        out_shape=jax.ShapeDtypeStruct((m, n), F32),
        compiler_params=_params("parallel", "parallel", "arbitrary"),
        name="down_proj",
    )(a, b, res)


def _split3(a):
    hi = a.astype(BF16)
    r1 = a - hi.astype(F32)
    mid = r1.astype(BF16)
    lo = (r1 - mid.astype(F32)).astype(BF16)
    return hi, mid, lo


def _conv_silu(u_ref, tail_ref, ext_ref, w_ref, b_ref):
    chunk = u_ref.shape[0]
    u = u_ref[...]
    ext_ref[0:SUBLANES, :] = tail_ref[...]
    ext_ref[SUBLANES:SUBLANES + chunk, :] = u
    tail_ref[...] = u[chunk - SUBLANES:, :]
    w = w_ref[...]
    y = b_ref[...] + w[CONV_WIDTH - 1:CONV_WIDTH, :] * u
    for j in range(CONV_WIDTH - 1):
        start = SUBLANES - (CONV_WIDTH - 1) + j
        y = y + w[j:j + 1, :] * ext_ref[start:start + chunk, :]
    return y * (1.0 / (1.0 + jnp.exp(-y)))


def _mlstm_kernel(qk_ref, v_ref, o_ref, g_ref, gb_ref, cw_ref, cb_ref, nrm_ref, wsrc_ref, y_ref, wdst_ref,
                  c_s, n_s, tail_s, ext_s):
    chunk = qk_ref.shape[0]
    dk, dv = MLSTM_DQK, MLSTM_DV
    wdst_ref[...] = wsrc_ref[...].astype(wdst_ref.dtype)

    @pl.when(pl.program_id(0) == 0)
    def _():
        c_s[...] = jnp.zeros_like(c_s)
        n_s[...] = jnp.zeros_like(n_s)
        tail_s[...] = jnp.zeros_like(tail_s)

    qk = _conv_silu(qk_ref, tail_s, ext_s, cw_ref, cb_ref)

    pre = GATE_SOFTCAP * jnp.tanh((g_ref[...] + gb_ref[...]) * (1.0 / GATE_SOFTCAP))
    logsig = jnp.minimum(pre, 0.0) - jnp.log(1.0 + jnp.exp(-jnp.abs(pre)))
    row = lax.broadcasted_iota(jnp.int32, (chunk, chunk), 0)
    col = lax.broadcasted_iota(jnp.int32, (chunk, chunk), 1)
    causal = row >= col
    tri = causal.astype(BF16)
    cum = sum(jnp.dot(tri, piece, preferred_element_type=F32) for piece in _split3(logsig))
    pre_t = pre.T
    cum_t = cum.T

    for hd in range(MLSTM_HEADS):
        g0 = 2 * hd
        q = qk[:, hd * dk:(hd + 1) * dk] * (dk ** -0.5)
        k = qk[:, (MLSTM_HEADS + hd) * dk:(MLSTM_HEADS + hd + 1) * dk]
        qb = q.astype(BF16)
        v = v_ref[:, hd * dv:(hd + 1) * dv].astype(BF16)
        li_col = pre[:, g0:g0 + 1]
        b_col = cum[:, g0 + 1:g0 + 2]
        b_last = cum[chunk - 1:chunk, g0 + 1:g0 + 2]
        r_col = li_col - b_col
        r_row = pre_t[g0:g0 + 1, :] - cum_t[g0 + 1:g0 + 2, :]

        dmat = jnp.where(causal, b_col + r_row, -jnp.inf)
        s_qk = lax.dot_general(qb, k.astype(BF16), (((1,), (1,)), ((), ())), preferred_element_type=F32)
        p = s_qk * jnp.exp(dmat)
        inter_w = jnp.exp(b_col)
        num = jnp.dot(p.astype(BF16), v, preferred_element_type=F32)
        num = num + inter_w * jnp.dot(qb, c_s[hd].astype(BF16), preferred_element_type=F32)
        den = jnp.sum(p, axis=-1, keepdims=True) + inter_w * jnp.sum(q * n_s[hd], axis=-1, keepdims=True)
        h = num / jnp.maximum(jnp.abs(den), 1.0)

        kw = k * jnp.exp(b_last + r_col)
        decay = jnp.exp(b_last)
        c_s[hd] = decay * c_s[hd] + lax.dot_general(kw.astype(BF16), v, (((0,), (0,)), ((), ())),
                                                    preferred_element_type=F32)
        n_s[hd] = decay * n_s[hd] + jnp.sum(kw, axis=0, keepdims=True)

        hn = h * lax.rsqrt(jnp.mean(h * h, axis=-1, keepdims=True) + EPS) * nrm_ref[:, hd * dv:(hd + 1) * dv]
        gate = 1.0 / (1.0 + jnp.exp(-o_ref[:, hd * dv:(hd + 1) * dv].astype(F32)))
        y_ref[:, hd * dv:(hd + 1) * dv] = (gate * hn).astype(y_ref.dtype)


def _mlstm(proj, gates, gate_bias, conv_w, conv_b, mlstm_norm, cast_w):
    s = proj.shape[0]
    L = MLSTM_CHUNK
    H = MLSTM_HEADS
    dk, dv = MLSTM_DQK, MLSTM_DV
    qkw = 2 * MLSTM_QK_WIDTH
    assert qkw == MLSTM_WIDTH
    cspec, cshape = _side_cast(cast_w, (s // L,))
    return pl.pallas_call(
        _mlstm_kernel,
        grid=(s // L,),
        in_specs=[pl.BlockSpec((L, qkw), lambda c: (c, 0)),
                  pl.BlockSpec((L, MLSTM_WIDTH), lambda c: (c, 1)),
                  pl.BlockSpec((L, MLSTM_WIDTH), lambda c: (c, 2)),
                  pl.BlockSpec((L, LANES), lambda c: (c, 0)),
                  pl.BlockSpec((1, LANES), lambda c: (0, 0)),
                  pl.BlockSpec((CONV_WIDTH, qkw), lambda c: (0, 0)),
                  pl.BlockSpec((1, qkw), lambda c: (0, 0)),
                  pl.BlockSpec((1, MLSTM_WIDTH), lambda c: (0, 0)),
                  cspec],
        out_specs=[pl.BlockSpec((L, MLSTM_WIDTH), lambda c: (c, 0)), cspec],
        out_shape=[jax.ShapeDtypeStruct((s, MLSTM_WIDTH), BF16), cshape],
        scratch_shapes=[pltpu.VMEM((H, dk, dv), F32),
                        pltpu.VMEM((H, 1, dk), F32),
                        pltpu.VMEM((SUBLANES, qkw), F32),
                        pltpu.VMEM((SUBLANES + L, qkw), F32)],
        compiler_params=_params("arbitrary"),
        name="mlstm",
    )(proj, proj, proj, gates, gate_bias, conv_w, conv_b, mlstm_norm, cast_w)


def _band_softmax(q, k, v, bias):
    s = lax.dot_general(q, k, (((1,), (1,)), ((), ())), preferred_element_type=F32) + bias
    m = jnp.max(s, axis=-1, keepdims=True)
    p = jnp.exp2(s - m)
    l = jnp.sum(p, axis=-1, keepdims=True)
    o = jnp.dot(p.astype(BF16), v, preferred_element_type=F32) / l
    return o, m + jnp.log2(l)


def _attn_kernel(q_ref, kp_ref, kc_ref, vp_ref, vc_ref, wsrc_ref, y_ref, wdst_ref,
                 q4_s, k4_s, v4_s, o_s, lse_s, bias_s):
    blk = ATTN_BLOCK
    sup = q_ref.shape[0]
    nblk = sup // blk
    d1, d2 = DILATIONS[1], DILATIONS[2]
    sub = sup // d1
    sb = pl.program_id(1)
    qscale = 1.4426950408889634 * ATTN_HEAD_DIM ** -0.5

    wdst_ref[...] = wsrc_ref[...].astype(wdst_ref.dtype)
    for r in range(d1):
        q4_s[r] = q_ref[pl.ds(r, sub, stride=d1), :] * qscale
        k4_s[r, 0:sub, :] = kp_ref[pl.ds(r, sub, stride=d1), :]
        k4_s[r, sub:2 * sub, :] = kc_ref[pl.ds(r, sub, stride=d1), :]
        v4_s[r, 0:sub, :] = vp_ref[pl.ds(r, sub, stride=d1), :]
        v4_s[r, sub:2 * sub, :] = vc_ref[pl.ds(r, sub, stride=d1), :]

    qi = lax.broadcasted_iota(jnp.int32, (blk, 2 * blk), 0)
    kj = lax.broadcasted_iota(jnp.int32, (blk, 2 * blk), 1)
    band = (kj >= qi) & (kj <= qi + blk)
    bias_s[0] = jnp.where(band, 0.0, -jnp.inf).astype(F32)
    bias_s[1] = jnp.where(band & (kj >= blk), 0.0, -jnp.inf).astype(F32)

    def body(t, carry):
        def run(p_idx, q, k, v, out_rows, first):
            bias = bias_s[first.astype(jnp.int32)]
            o, lse = _band_softmax(q.astype(BF16), k.astype(BF16), v.astype(BF16), bias)
            o_s[p_idx, out_rows, :] = o
            lse_s[p_idx, out_rows, :] = jnp.broadcast_to(lse, (blk, LANES))

        q0 = pl.multiple_of(t * blk, blk)
        qp = pl.multiple_of(jnp.maximum(t - 1, 0) * blk, blk)
        at_start = t == 0
        k_prev = jnp.where(at_start, kp_ref[sup - blk:sup, :], kc_ref[pl.ds(qp, blk), :])
        v_prev = jnp.where(at_start, vp_ref[sup - blk:sup, :], vc_ref[pl.ds(qp, blk), :])
        run(0, q_ref[pl.ds(q0, blk), :] * qscale,
            jnp.concatenate([k_prev, kc_ref[pl.ds(q0, blk), :]], axis=0),
            jnp.concatenate([v_prev, vc_ref[pl.ds(q0, blk), :]], axis=0),
            pl.ds(q0, blk), (sb == 0) & at_start)

        sg = lax.shift_right_logical(t, d1.bit_length() - 1)
        r = t & (d1 - 1)
        s0 = pl.multiple_of(sg * blk, blk)
        kwin = pl.ds(pl.multiple_of(sub - blk + sg * blk, blk), 2 * blk)
        run(1, q4_s[r, pl.ds(s0, blk), :], k4_s[r, kwin, :], v4_s[r, kwin, :],
            pl.ds(sg * (d1 * blk) + r, blk, stride=d1), (sb == 0) & (sg == 0))

        run(2, q4_s[r, pl.ds(sg, blk, stride=d2 // d1), :], k4_s[r, pl.ds(sg, 2 * blk, stride=d2 // d1), :],
            v4_s[r, pl.ds(sg, 2 * blk, stride=d2 // d1), :], pl.ds(t, blk, stride=d2), sb == 0)
        return carry

    lax.fori_loop(0, nblk, body, 0, unroll=ATTN_UNROLL)

    def combine(t, carry):
        rows = pl.ds(pl.multiple_of(t * blk, blk), blk)
        l0, l1, l2 = lse_s[0, rows, :], lse_s[1, rows, :], lse_s[2, rows, :]
        mx = jnp.maximum(jnp.maximum(l0, l1), l2)
        w0, w1, w2 = jnp.exp2(l0 - mx), jnp.exp2(l1 - mx), jnp.exp2(l2 - mx)
        out = (w0 * o_s[0, rows, :] + w1 * o_s[1, rows, :] + w2 * o_s[2, rows, :]) / (w0 + w1 + w2)
        y_ref[rows, :] = out.astype(y_ref.dtype)
        return carry

    lax.fori_loop(0, nblk, combine, 0)


def _dilated_attention(pf, col0, cast_w):
    s = pf.shape[0]
    sup = ATTN_SUPER
    hb = ATTN_HEADS
    d1 = DILATIONS[1]
    qb, kb, vb = (col0 // LANES + i * hb for i in range(3))
    prev = lambda b: jnp.maximum(b - 1, 0)
    grid = (hb, s // sup)
    cspec, cshape = _side_cast(cast_w, grid)
    return pl.pallas_call(
        _attn_kernel,
        grid=grid,
        in_specs=[pl.BlockSpec((sup, LANES), lambda h, b: (b, qb + h)),
                  pl.BlockSpec((sup, LANES), lambda h, b: (prev(b), kb + h)),
                  pl.BlockSpec((sup, LANES), lambda h, b: (b, kb + h)),
                  pl.BlockSpec((sup, LANES), lambda h, b: (prev(b), vb + h)),
                  pl.BlockSpec((sup, LANES), lambda h, b: (b, vb + h)),
                  cspec],
        out_specs=[pl.BlockSpec((sup, LANES), lambda h, b: (b, h)), cspec],
        out_shape=[jax.ShapeDtypeStruct((s, ATTN_WIDTH), BF16), cshape],
        scratch_shapes=[pltpu.VMEM((d1, sup // d1, LANES), F32),
                        pltpu.VMEM((d1, 2 * sup // d1, LANES), F32),
                        pltpu.VMEM((d1, 2 * sup // d1, LANES), F32),
                        pltpu.VMEM((len(DILATIONS), sup, LANES), F32),
                        pltpu.VMEM((len(DILATIONS), sup, LANES), F32),
                        pltpu.VMEM((2, ATTN_BLOCK, 2 * ATTN_BLOCK), F32)],
        compiler_params=_params("parallel", "parallel"),
        name="dilated_attn",
    )(pf, pf, pf, pf, pf, cast_w)


def _gate_weights(w_in_t, i_bias_l, f_bias_l):
    g0 = 2 * MLSTM_QK_WIDTH + 2 * MLSTM_WIDTH
    H = MLSTM_HEADS
    wg = jnp.stack([w_in_t[g0:g0 + H], w_in_t[g0 + H:g0 + 2 * H]], axis=1).reshape(2 * H, w_in_t.shape[1])
    wg = jnp.pad(wg, ((0, LANES - 2 * H), (0, 0)))
    bg = jnp.stack([i_bias_l, f_bias_l], axis=-1).reshape(1, 2 * H)
    bg = jnp.pad(bg, ((0, 0), (0, LANES - 2 * H)))
    return wg, bg.astype(F32)


def _layer(x, norm_mix, w_in, conv_w, conv_b, i_bias, f_bias, mlstm_norm, w_out, norm_mlp, w_up, w_down):
    qk_w = 2 * MLSTM_QK_WIDTH
    g0 = qk_w + 2 * MLSTM_WIDTH
    a0 = g0 + 2 * MLSTM_HEADS
    n_proj = w_in.shape[1] - 2 * MLSTM_HEADS
    assert a0 - g0 == SUBLANES
    w_in_t = w_in.T
    w_g, b_g = _gate_weights(w_in_t, i_bias, f_bias)

    h = _rmsnorm(x, norm_mix, BF16)
    proj = _in_proj(h, w_in_t, g0, n_proj)
    gates, = _matmul_nt(h, w_g, F32, tn=w_g.shape[0], name="in_proj_gates")
    y_a, w_out_b = _dilated_attention(proj, g0, w_out)
    y_m, w_up_b = _mlstm(proj, gates, b_g, conv_w, conv_b.reshape(1, -1), mlstm_norm.reshape(1, -1), w_up)
    x1, x1g, ssq = _out_proj(y_m, y_a, w_out_b, x, norm_mlp)
    u, w_down_b = _up_proj(x1g, ssq, w_up_b, w_down)
    return _matmul_ktiled_res(u, w_down_b, x1)


def kernel(x, norm_mix, w_in, conv_w, conv_b, i_bias, f_bias, mlstm_norm, w_out, norm_mlp, w_up, w_down, norm_final):
    b, s, d = x.shape
    outs = []
    for bi in range(b):
        xb = x[bi]
        for layer in range(norm_mix.shape[0]):
            xb = _layer(xb, norm_mix[layer], w_in[layer], conv_w[layer], conv_b[layer], i_bias[layer], f_bias[layer],
                        mlstm_norm[layer], w_out[layer], norm_mlp[layer], w_up[layer], w_down[layer])
        outs.append(_rmsnorm(xb, norm_final, x.dtype))
    return jnp.stack(outs, axis=0)
```

```python
import functools

import jax
import jax.numpy as jnp
from jax import lax
from jax.experimental import pallas as pl
from jax.experimental.pallas import tpu as pltpu

F32 = jnp.float32
BF16 = jnp.bfloat16

D_MODEL = 4096
MLSTM_HEADS = 4
MLSTM_DV = 512
MLSTM_DQK = 256
MLSTM_QK_WIDTH = MLSTM_HEADS * MLSTM_DQK
MLSTM_WIDTH = MLSTM_HEADS * MLSTM_DV
CONV_WIDTH = 4
GATE_SOFTCAP = 15.0
ATTN_HEAD_DIM = 128
ATTN_HEADS = 16
ATTN_WIDTH = ATTN_HEADS * ATTN_HEAD_DIM
ATTN_BLOCK = 128
DILATIONS = (1, 4, 16)
D_FF = 4 * D_MODEL
EPS = 1e-6

LANES = 128
SUBLANES = 8
VMEM_LIMIT_BYTES = 58 * 1024 * 1024

MM_TILE = 1024
MM_K_TILE = 4096
NORM_ROWS = 512
MLSTM_CHUNK = 256
ATTN_SUPER = DILATIONS[-1] * ATTN_BLOCK
ATTN_UNROLL = 8


def _params(*sem):
    return pltpu.CompilerParams(dimension_semantics=sem, vmem_limit_bytes=VMEM_LIMIT_BYTES)


def _rmsnorm_kernel(x_ref, g_ref, o_ref):
    x = x_ref[...]
    ms = jnp.mean(x * x, axis=-1, keepdims=True)
    o_ref[...] = (x * lax.rsqrt(ms + EPS) * g_ref[...]).astype(o_ref.dtype)


def _rmsnorm(x, g, out_dtype):
    s, d = x.shape
    return pl.pallas_call(
        _rmsnorm_kernel,
        grid=(s // NORM_ROWS,),
        in_specs=[pl.BlockSpec((NORM_ROWS, d), lambda i: (i, 0)),
                  pl.BlockSpec((1, d), lambda i: (0, 0))],
        out_specs=pl.BlockSpec((NORM_ROWS, d), lambda i: (i, 0)),
        out_shape=jax.ShapeDtypeStruct((s, d), out_dtype),
        compiler_params=_params("parallel"),
        name="rmsnorm",
    )(x, g.reshape(1, d))


def _side_cast(w, grid):
    steps = 1
    for g in grid:
        steps *= g
    rows = w.shape[0] // steps
    assert rows * steps == w.shape[0] and rows % (2 * SUBLANES) == 0

    def index_map(*ids):
        lin = ids[0]
        for g, i in zip(grid[1:], ids[1:]):
            lin = lin * g + i
        return (lin, 0)

    spec = pl.BlockSpec((rows, w.shape[1]), index_map)
    return spec, jax.ShapeDtypeStruct(w.shape, BF16)


def _in_proj_kernel(a_ref, w_ref, nxt_ref, o_ref, wb_s, *, first_shifted):
    j = pl.program_id(0)
    first_row_tile = pl.program_id(1) == 0

    @pl.when(first_row_tile & (j < first_shifted))
    def _():
        wb_s[...] = w_ref[...].astype(BF16)

    @pl.when(first_row_tile & (j >= first_shifted))
    def _():
        tn = w_ref.shape[0]
        body = tn - 2 * SUBLANES
        wb_s[0:body, :] = w_ref[SUBLANES:SUBLANES + body, :].astype(BF16)
        wb_s[body:tn, :] = jnp.concatenate([w_ref[tn - SUBLANES:tn, :], nxt_ref[...]], axis=0).astype(BF16)

    o_ref[...] = lax.dot_general(a_ref[...], wb_s[...], (((1,), (1,)), ((), ())), preferred_element_type=F32)


def _in_proj(a, wt, gap_row, n_out):
    m, k = a.shape
    tm, tn = MM_TILE, MM_TILE // 2
    assert gap_row % tn == 0 and n_out % tn == 0 and wt.shape[0] == n_out + SUBLANES
    per = tn // SUBLANES
    return pl.pallas_call(
        functools.partial(_in_proj_kernel, first_shifted=gap_row // tn),
        grid=(n_out // tn, m // tm),
        in_specs=[pl.BlockSpec((tm, k), lambda j, i: (i, 0)),
                  pl.BlockSpec((tn, k), lambda j, i: (j, 0)),
                  pl.BlockSpec((SUBLANES, k), lambda j, i: ((j + 1) * per, 0))],
        out_specs=pl.BlockSpec((tm, tn), lambda j, i: (i, j)),
        out_shape=jax.ShapeDtypeStruct((m, n_out), F32),
        scratch_shapes=[pltpu.VMEM((tn, k), BF16)],
        compiler_params=_params("parallel", "arbitrary"),
        name="in_proj",
    )(a, wt, wt)


def _gate_proj_kernel(a_ref, wt_ref, o_ref):
    o_ref[...] = lax.dot_general(a_ref[...], wt_ref[...].astype(BF16), (((1,), (1,)), ((), ())),
                                 preferred_element_type=F32)


def _gate_proj(a, wt):
    m, k = a.shape
    n = wt.shape[0]
    tm = MM_TILE
    return pl.pallas_call(
        _gate_proj_kernel,
        grid=(m // tm,),
        in_specs=[pl.BlockSpec((tm, k), lambda i: (i, 0)),
                  pl.BlockSpec((n, k), lambda i: (0, 0))],
        out_specs=pl.BlockSpec((tm, n), lambda i: (i, 0)),
        out_shape=jax.ShapeDtypeStruct((m, n), F32),
        compiler_params=_params("parallel"),
        name="in_proj_gates",
    )(a, wt)


def _up_proj_kernel(a_ref, ssq_ref, b_ref, wsrc_ref, o_ref, wdst_ref, *, d_norm):
    wdst_ref[...] = wsrc_ref[...].astype(wdst_ref.dtype)
    acc = jnp.dot(a_ref[...], b_ref[...], preferred_element_type=F32)
    acc = acc * lax.rsqrt(ssq_ref[:, 0:1] * (1.0 / d_norm) + EPS)
    acc = jnp.maximum(acc, 0.0)
    o_ref[...] = (acc * acc).astype(o_ref.dtype)


def _up_proj(xg, ssq, w, cast_w):
    m, k = xg.shape
    n = w.shape[1]
    tm = tn = MM_TILE
    grid = (m // tm, n // tn)
    cspec, cshape = _side_cast(cast_w, grid)
    return pl.pallas_call(
        functools.partial(_up_proj_kernel, d_norm=k),
        grid=grid,
        in_specs=[pl.BlockSpec((tm, k), lambda i, j: (i, 0)),
                  pl.BlockSpec((tm, LANES), lambda i, j: (i, 0)),
                  pl.BlockSpec((k, tn), lambda i, j: (0, j)),
                  cspec],
        out_specs=[pl.BlockSpec((tm, tn), lambda i, j: (i, j)), cspec],
        out_shape=[jax.ShapeDtypeStruct((m, n), BF16), cshape],
        compiler_params=_params("parallel", "parallel"),
        name="up_proj",
    )(xg, ssq, w, cast_w)


def _out_proj_kernel(a0_ref, a1_ref, b0_ref, b1_ref, r_ref, g_ref, o_ref, xg_ref, ssq_ref):
    acc = jnp.dot(a0_ref[...], b0_ref[...], preferred_element_type=F32)
    acc = acc + jnp.dot(a1_ref[...], b1_ref[...], preferred_element_type=F32)
    x1 = r_ref[...] + acc
    o_ref[...] = x1
    xg_ref[...] = (x1 * g_ref[...]).astype(xg_ref.dtype)
    part = jnp.broadcast_to(jnp.sum(x1 * x1, axis=-1, keepdims=True), ssq_ref.shape)
    j = pl.program_id(1)

    @pl.when(j == 0)
    def _():
        ssq_ref[...] = part

    @pl.when(j > 0)
    def _():
        ssq_ref[...] += part


def _out_proj(y_m, y_a, w, res, gain):
    m, k0 = y_m.shape
    k1 = y_a.shape[1]
    assert k0 == k1
    n = w.shape[1]
    tm, tn = MM_TILE, MM_TILE // 2
    return pl.pallas_call(
        _out_proj_kernel,
        grid=(m // tm, n // tn),
        in_specs=[pl.BlockSpec((tm, k0), lambda i, j: (i, 0)),
                  pl.BlockSpec((tm, k1), lambda i, j: (i, 0)),
                  pl.BlockSpec((k0, tn), lambda i, j: (0, j)),
                  pl.BlockSpec((k1, tn), lambda i, j: (1, j)),
                  pl.BlockSpec((tm, tn), lambda i, j: (i, j)),
                  pl.BlockSpec((1, tn), lambda i, j: (0, j))],
        out_specs=[pl.BlockSpec((tm, tn), lambda i, j: (i, j)),
                   pl.BlockSpec((tm, tn), lambda i, j: (i, j)),
                   pl.BlockSpec((tm, LANES), lambda i, j: (i, 0))],
        out_shape=[jax.ShapeDtypeStruct((m, n), F32),
                   jax.ShapeDtypeStruct((m, n), BF16),
                   jax.ShapeDtypeStruct((m, LANES), F32)],
        compiler_params=_params("parallel", "arbitrary"),
        name="out_proj",
    )(y_m, y_a, w, w, res, gain.reshape(1, n))


def _mmk_res_kernel(a_ref, b_ref, r_ref, o_ref):
    @pl.when(pl.program_id(2) == 0)
    def _():
        o_ref[...] = r_ref[...]

    o_ref[...] += jnp.dot(a_ref[...], b_ref[...], preferred_element_type=F32)


def _matmul_ktiled_res(a, b, res):
    m, k = a.shape
    n = b.shape[1]
    tm = tn = MM_TILE
    tk = MM_K_TILE
    return pl.pallas_call(
        _mmk_res_kernel,
        grid=(m // tm, n // tn, k // tk),
        in_specs=[pl.BlockSpec((tm, tk), lambda i, j, kk: (i, kk)),
                  pl.BlockSpec((tk, tn), lambda i, j, kk: (kk, j)),
                  pl.BlockSpec((tm, tn), lambda i, j, kk: (i, j))],
        out_specs=pl.BlockSpec((tm, tn), lambda i, j, kk: (i, j)),
        out_shape=jax.ShapeDtypeStruct((m, n), F32),
        compiler_params=_params("parallel", "parallel", "arbitrary"),
        name="down_proj",
    )(a, b, res)


def _split3(a):
    hi = a.astype(BF16)
    r1 = a - hi.astype(F32)
    mid = r1.astype(BF16)
    lo = (r1 - mid.astype(F32)).astype(BF16)
    return hi, mid, lo


def _conv_silu(u_ref, tail_ref, ext_ref, w_ref, b_ref):
    chunk = u_ref.shape[0]
    u = u_ref[...]
    ext_ref[0:SUBLANES, :] = tail_ref[...]
    ext_ref[SUBLANES:SUBLANES + chunk, :] = u
    tail_ref[...] = u[chunk - SUBLANES:, :]
    w = w_ref[...]
    y = b_ref[...] + w[CONV_WIDTH - 1:CONV_WIDTH, :] * u
    for j in range(CONV_WIDTH - 1):
        start = SUBLANES - (CONV_WIDTH - 1) + j
        y = y + w[j:j + 1, :] * ext_ref[start:start + chunk, :]
    return y * _sigmoid(y)


def _sigmoid(z):
    return 0.5 + 0.5 * jnp.tanh(0.5 * z)


def _mlstm_kernel(qk_ref, v_ref, o_ref, g_ref, gb_ref, cw_ref, cb_ref, nrm_ref, wsrc_ref, y_ref, wdst_ref,
                  c_s, n_s, tail_s, ext_s):
    chunk = qk_ref.shape[0]
    dk, dv = MLSTM_DQK, MLSTM_DV
    wdst_ref[...] = wsrc_ref[...].astype(wdst_ref.dtype)

    @pl.when(pl.program_id(0) == 0)
    def _():
        c_s[...] = jnp.zeros_like(c_s)
        n_s[...] = jnp.zeros_like(n_s)
        tail_s[...] = jnp.zeros_like(tail_s)

    qk = _conv_silu(qk_ref, tail_s, ext_s, cw_ref, cb_ref)

    pre = GATE_SOFTCAP * jnp.tanh((g_ref[...] + gb_ref[...]) * (1.0 / GATE_SOFTCAP))
    logsig = jnp.minimum(pre, 0.0) - jnp.log(1.0 + jnp.exp(-jnp.abs(pre)))
    row = lax.broadcasted_iota(jnp.int32, (chunk, chunk), 0)
    col = lax.broadcasted_iota(jnp.int32, (chunk, chunk), 1)
    causal = row >= col
    tri = causal.astype(BF16)
    cum = sum(jnp.dot(tri, piece, preferred_element_type=F32) for piece in _split3(logsig))
    pre_t = pre.T
    cum_t = cum.T

    for hd in range(MLSTM_HEADS):
        g0 = 2 * hd
        q = qk[:, hd * dk:(hd + 1) * dk] * (dk ** -0.5)
        k = qk[:, (MLSTM_HEADS + hd) * dk:(MLSTM_HEADS + hd + 1) * dk]
        qb = q.astype(BF16)
        v = v_ref[:, hd * dv:(hd + 1) * dv].astype(BF16)
        li_col = pre[:, g0:g0 + 1]
        b_col = cum[:, g0 + 1:g0 + 2]
        b_last = cum[chunk - 1:chunk, g0 + 1:g0 + 2]
        r_col = li_col - b_col
        r_row = pre_t[g0:g0 + 1, :] - cum_t[g0 + 1:g0 + 2, :]

        dmat = jnp.where(causal, b_col + r_row, -jnp.inf)
        s_qk = lax.dot_general(qb, k.astype(BF16), (((1,), (1,)), ((), ())), preferred_element_type=F32)
        p = s_qk * jnp.exp(dmat)
        inter_w = jnp.exp(b_col)
        num = jnp.dot(p.astype(BF16), v, preferred_element_type=F32)
        num = num + inter_w * jnp.dot(qb, c_s[hd].astype(BF16), preferred_element_type=F32)
        den = jnp.sum(p, axis=-1, keepdims=True) + inter_w * jnp.sum(q * n_s[hd], axis=-1, keepdims=True)
        h = num / jnp.maximum(jnp.abs(den), 1.0)

        kw = k * jnp.exp(b_last + r_col)
        decay = jnp.exp(b_last)
        c_s[hd] = decay * c_s[hd] + lax.dot_general(kw.astype(BF16), v, (((0,), (0,)), ((), ())),
                                                    preferred_element_type=F32)
        n_s[hd] = decay * n_s[hd] + jnp.sum(kw, axis=0, keepdims=True)

        hn = h * lax.rsqrt(jnp.mean(h * h, axis=-1, keepdims=True) + EPS) * nrm_ref[:, hd * dv:(hd + 1) * dv]
        gate = _sigmoid(o_ref[:, hd * dv:(hd + 1) * dv].astype(F32))
        y_ref[:, hd * dv:(hd + 1) * dv] = (gate * hn).astype(y_ref.dtype)


def _mlstm(proj, gates, gate_bias, conv_w, conv_b, mlstm_norm, cast_w):
    s = proj.shape[0]
    L = MLSTM_CHUNK
    H = MLSTM_HEADS
    dk, dv = MLSTM_DQK, MLSTM_DV
    qkw = 2 * MLSTM_QK_WIDTH
    assert qkw == MLSTM_WIDTH
    cspec, cshape = _side_cast(cast_w, (s // L,))
    return pl.pallas_call(
        _mlstm_kernel,
        grid=(s // L,),
        in_specs=[pl.BlockSpec((L, qkw), lambda c: (c, 0)),
                  pl.BlockSpec((L, MLSTM_WIDTH), lambda c: (c, 1)),
                  pl.BlockSpec((L, MLSTM_WIDTH), lambda c: (c, 2)),
                  pl.BlockSpec((L, LANES), lambda c: (c, 0)),
                  pl.BlockSpec((1, LANES), lambda c: (0, 0)),
                  pl.BlockSpec((CONV_WIDTH, qkw), lambda c: (0, 0)),
                  pl.BlockSpec((1, qkw), lambda c: (0, 0)),
                  pl.BlockSpec((1, MLSTM_WIDTH), lambda c: (0, 0)),
                  cspec],
        out_specs=[pl.BlockSpec((L, MLSTM_WIDTH), lambda c: (c, 0)), cspec],
        out_shape=[jax.ShapeDtypeStruct((s, MLSTM_WIDTH), BF16), cshape],
        scratch_shapes=[pltpu.VMEM((H, dk, dv), F32),
                        pltpu.VMEM((H, 1, dk), F32),
                        pltpu.VMEM((SUBLANES, qkw), F32),
                        pltpu.VMEM((SUBLANES + L, qkw), F32)],
        compiler_params=_params("arbitrary"),
        name="mlstm",
    )(proj, proj, proj, gates, gate_bias, conv_w, conv_b, mlstm_norm, cast_w)


def _band_softmax(q, k, v, bias):
    s = lax.dot_general(q, k, (((1,), (1,)), ((), ())), preferred_element_type=F32) + bias
    m = jnp.max(s, axis=-1, keepdims=True)
    p = jnp.exp2(s - m)
    l = jnp.sum(p, axis=-1, keepdims=True)
    o = jnp.dot(p.astype(BF16), v, preferred_element_type=F32) / l
    return o, m + jnp.log2(l)


def _attn_kernel(q_ref, kp_ref, kc_ref, vp_ref, vc_ref, wsrc_ref, y_ref, wdst_ref,
                 q4_s, k4_s, v4_s, o_s, lse_s, bias_s):
    blk = ATTN_BLOCK
    sup = q_ref.shape[0]
    nblk = sup // blk
    d1, d2 = DILATIONS[1], DILATIONS[2]
    sub = sup // d1
    sb = pl.program_id(1)
    cur = (sb & 1) * d1
    prv = d1 - cur
    qscale = 1.4426950408889634 * ATTN_HEAD_DIM ** -0.5

    wdst_ref[...] = wsrc_ref[...].astype(wdst_ref.dtype)

    @pl.when(sb == 0)
    def _():
        for r in range(d1):
            k4_s[prv + r] = jnp.zeros((sub, LANES), F32)
            v4_s[prv + r] = jnp.zeros((sub, LANES), F32)

    for r in range(d1):
        q4_s[r] = q_ref[pl.ds(r, sub, stride=d1), :] * qscale
        k4_s[cur + r] = kc_ref[pl.ds(r, sub, stride=d1), :]
        v4_s[cur + r] = vc_ref[pl.ds(r, sub, stride=d1), :]

    qi = lax.broadcasted_iota(jnp.int32, (blk, 2 * blk), 0)
    kj = lax.broadcasted_iota(jnp.int32, (blk, 2 * blk), 1)
    band = (kj >= qi) & (kj <= qi + blk)
    bias_s[0] = jnp.where(band, 0.0, -jnp.inf).astype(F32)
    bias_s[1] = jnp.where(band & (kj >= blk), 0.0, -jnp.inf).astype(F32)

    def body(t, carry):
        def run(p_idx, q, k, v, out_rows, first):
            bias = bias_s[first.astype(jnp.int32)]
            o, lse = _band_softmax(q.astype(BF16), k.astype(BF16), v.astype(BF16), bias)
            o_s[p_idx, out_rows, :] = o
            lse_s[p_idx, out_rows, :] = jnp.broadcast_to(lse, (blk, LANES))

        q0 = pl.multiple_of(t * blk, blk)
        qp = pl.multiple_of(jnp.maximum(t - 1, 0) * blk, blk)
        at_start = t == 0
        k_prev = jnp.where(at_start, kp_ref[...], kc_ref[pl.ds(qp, blk), :])
        v_prev = jnp.where(at_start, vp_ref[...], vc_ref[pl.ds(qp, blk), :])
        run(0, q_ref[pl.ds(q0, blk), :] * qscale,
            jnp.concatenate([k_prev, kc_ref[pl.ds(q0, blk), :]], axis=0),
            jnp.concatenate([v_prev, vc_ref[pl.ds(q0, blk), :]], axis=0),
            pl.ds(q0, blk), (sb == 0) & at_start)

        sg = lax.shift_right_logical(t, d1.bit_length() - 1)
        r = t & (d1 - 1)
        s0 = pl.multiple_of(sg * blk, blk)
        seg0 = sg == 0
        before = (jnp.where(seg0, prv, cur) + r,
                  pl.ds(pl.multiple_of(jnp.where(seg0, sub - blk, s0 - blk), blk), blk))
        own = (cur + r, pl.ds(s0, blk))
        run(1, q4_s[r, pl.ds(s0, blk), :],
            jnp.concatenate([k4_s[before[0], before[1], :], k4_s[own[0], own[1], :]], axis=0),
            jnp.concatenate([v4_s[before[0], before[1], :], v4_s[own[0], own[1], :]], axis=0),
            pl.ds(sg * (d1 * blk) + r, blk, stride=d1), (sb == 0) & seg0)

        walk = pl.ds(sg, blk, stride=d2 // d1)
        run(2, q4_s[r, walk, :],
            jnp.concatenate([k4_s[prv + r, walk, :], k4_s[cur + r, walk, :]], axis=0),
            jnp.concatenate([v4_s[prv + r, walk, :], v4_s[cur + r, walk, :]], axis=0),
            pl.ds(t, blk, stride=d2), sb == 0)
        return carry

    lax.fori_loop(0, nblk, body, 0, unroll=ATTN_UNROLL)

    def combine(t, carry):
        rows = pl.ds(pl.multiple_of(t * blk, blk), blk)
        l0, l1, l2 = lse_s[0, rows, :], lse_s[1, rows, :], lse_s[2, rows, :]
        mx = jnp.maximum(jnp.maximum(l0, l1), l2)
        w0, w1, w2 = jnp.exp2(l0 - mx), jnp.exp2(l1 - mx), jnp.exp2(l2 - mx)
        out = (w0 * o_s[0, rows, :] + w1 * o_s[1, rows, :] + w2 * o_s[2, rows, :]) / (w0 + w1 + w2)
        y_ref[rows, :] = out.astype(y_ref.dtype)
        return carry

    lax.fori_loop(0, nblk, combine, 0)


def _dilated_attention(pf, col0, cast_w):
    s = pf.shape[0]
    sup = ATTN_SUPER
    hb = ATTN_HEADS
    d1 = DILATIONS[1]
    qb, kb, vb = (col0 // LANES + i * hb for i in range(3))
    blk = ATTN_BLOCK
    nblk = sup // blk
    tail = lambda b: jnp.maximum(b * nblk - 1, 0)
    grid = (hb, s // sup)
    cspec, cshape = _side_cast(cast_w, grid)
    return pl.pallas_call(
        _attn_kernel,
        grid=grid,
        in_specs=[pl.BlockSpec((sup, LANES), lambda h, b: (b, qb + h)),
                  pl.BlockSpec((blk, LANES), lambda h, b: (tail(b), kb + h)),
                  pl.BlockSpec((sup, LANES), lambda h, b: (b, kb + h)),
                  pl.BlockSpec((blk, LANES), lambda h, b: (tail(b), vb + h)),
                  pl.BlockSpec((sup, LANES), lambda h, b: (b, vb + h)),
                  cspec],
        out_specs=[pl.BlockSpec((sup, LANES), lambda h, b: (b, h)), cspec],
        out_shape=[jax.ShapeDtypeStruct((s, ATTN_WIDTH), BF16), cshape],
        scratch_shapes=[pltpu.VMEM((d1, sup // d1, LANES), F32),
                        pltpu.VMEM((2 * d1, sup // d1, LANES), F32),
                        pltpu.VMEM((2 * d1, sup // d1, LANES), F32),
                        pltpu.VMEM((len(DILATIONS), sup, LANES), F32),
                        pltpu.VMEM((len(DILATIONS), sup, LANES), F32),
                        pltpu.VMEM((2, blk, 2 * blk), F32)],
        compiler_params=_params("parallel", "arbitrary"),
        name="dilated_attn",
    )(pf, pf, pf, pf, pf, cast_w)


def _gate_weights(w_in_t, i_bias_l, f_bias_l):
    g0 = 2 * MLSTM_QK_WIDTH + 2 * MLSTM_WIDTH
    H = MLSTM_HEADS
    wg = jnp.stack([w_in_t[g0:g0 + H], w_in_t[g0 + H:g0 + 2 * H]], axis=1).reshape(2 * H, w_in_t.shape[1])
    wg = jnp.pad(wg, ((0, LANES - 2 * H), (0, 0)))
    bg = jnp.stack([i_bias_l, f_bias_l], axis=-1).reshape(1, 2 * H)
    bg = jnp.pad(bg, ((0, 0), (0, LANES - 2 * H)))
    return wg, bg.astype(F32)


def _layer(x, norm_mix, w_in, conv_w, conv_b, i_bias, f_bias, mlstm_norm, w_out, norm_mlp, w_up, w_down):
    qk_w = 2 * MLSTM_QK_WIDTH
    g0 = qk_w + 2 * MLSTM_WIDTH
    a0 = g0 + 2 * MLSTM_HEADS
    n_proj = w_in.shape[1] - 2 * MLSTM_HEADS
    assert a0 - g0 == SUBLANES
    w_in_t = w_in.T
    w_g, b_g = _gate_weights(w_in_t, i_bias, f_bias)

    h = _rmsnorm(x, norm_mix, BF16)
    proj = _in_proj(h, w_in_t, g0, n_proj)
    gates = _gate_proj(h, w_g)
    y_a, w_out_b = _dilated_attention(proj, g0, w_out)
    y_m, w_up_b = _mlstm(proj, gates, b_g, conv_w, conv_b.reshape(1, -1), mlstm_norm.reshape(1, -1), w_up)
    x1, x1g, ssq = _out_proj(y_m, y_a, w_out_b, x, norm_mlp)
    u, w_down_b = _up_proj(x1g, ssq, w_up_b, w_down)
    return _matmul_ktiled_res(u, w_down_b, x1)


def kernel(x, norm_mix, w_in, conv_w, conv_b, i_bias, f_bias, mlstm_norm, w_out, norm_mlp, w_up, w_down, norm_final):
    b, s, d = x.shape
    outs = []
    for bi in range(b):
        xb = x[bi]
        for layer in range(norm_mix.shape[0]):
            xb = _layer(xb, norm_mix[layer], w_in[layer], conv_w[layer], conv_b[layer], i_bias[layer], f_bias[layer],
                        mlstm_norm[layer], w_out[layer], norm_mlp[layer], w_up[layer], w_down[layer])
        outs.append(_rmsnorm(xb, norm_final, x.dtype))
    return jnp.stack(outs, axis=0)
```

```python
import functools

import jax
import jax.numpy as jnp
from jax import lax
from jax.experimental import pallas as pl
from jax.experimental.pallas import tpu as pltpu

F32 = jnp.float32
BF16 = jnp.bfloat16

D_MODEL = 4096
MLSTM_HEADS = 4
MLSTM_DV = 512
MLSTM_DQK = 256
MLSTM_QK_WIDTH = MLSTM_HEADS * MLSTM_DQK
MLSTM_WIDTH = MLSTM_HEADS * MLSTM_DV
CONV_WIDTH = 4
GATE_SOFTCAP = 15.0
ATTN_HEAD_DIM = 128
ATTN_HEADS = 16
ATTN_WIDTH = ATTN_HEADS * ATTN_HEAD_DIM
ATTN_BLOCK = 128
DILATIONS = (1, 4, 16)
D_FF = 4 * D_MODEL
EPS = 1e-6

LANES = 128
SUBLANES = 8
VMEM_LIMIT_BYTES = 58 * 1024 * 1024

MM_TILE = 1024
MM_K_TILE = 4096
NORM_ROWS = 512
CAST_SLABS = 128
MLSTM_CHUNK = 256
ATTN_SUPER = DILATIONS[-1] * ATTN_BLOCK
ATTN_UNROLL = 8
ATTN_QSCALE = 1.4426950408889634 * ATTN_HEAD_DIM ** -0.5


def _params(*sem):
    return pltpu.CompilerParams(dimension_semantics=sem, vmem_limit_bytes=VMEM_LIMIT_BYTES)


def _rmsnorm_kernel(x_ref, g_ref, o_ref):
    x = x_ref[...]
    ms = jnp.mean(x * x, axis=-1, keepdims=True)
    o_ref[...] = (x * lax.rsqrt(ms + EPS) * g_ref[...]).astype(o_ref.dtype)


def _rmsnorm(x, g, out_dtype):
    s, d = x.shape
    return pl.pallas_call(
        _rmsnorm_kernel,
        grid=(s // NORM_ROWS,),
        in_specs=[pl.BlockSpec((NORM_ROWS, d), lambda i: (i, 0)),
                  pl.BlockSpec((1, d), lambda i: (0, 0))],
        out_specs=pl.BlockSpec((NORM_ROWS, d), lambda i: (i, 0)),
        out_shape=jax.ShapeDtypeStruct((s, d), out_dtype),
        compiler_params=_params("parallel"),
        name="rmsnorm",
    )(x, g.reshape(1, d))


def _side_cast(w, grid, slabs=None):
    steps = 1
    for g in grid:
        steps *= g
    slabs = steps if slabs is None else slabs
    rows = w.shape[0] // slabs
    assert slabs <= steps and rows * slabs == w.shape[0] and rows % (2 * SUBLANES) == 0

    def index_map(*ids):
        lin = ids[0]
        for g, i in zip(grid[1:], ids[1:]):
            lin = lin * g + i
        return (jnp.minimum(lin, slabs - 1), 0)

    spec = pl.BlockSpec((rows, w.shape[1]), index_map)
    return spec, jax.ShapeDtypeStruct(w.shape, BF16)


def _in_proj_kernel(a_ref, w_ref, nxt_ref, wsrc_ref, o_ref, wdst_ref, wb_s, *, first_shifted, scaled_tiles, scale):
    wdst_ref[...] = wsrc_ref[...].astype(wdst_ref.dtype)
    j = pl.program_id(0)
    first_row_tile = pl.program_id(1) == 0

    @pl.when(first_row_tile & (j < first_shifted))
    def _():
        wb_s[...] = w_ref[...].astype(BF16)

    @pl.when(first_row_tile & (j >= first_shifted))
    def _():
        tn = w_ref.shape[0]
        body = tn - 2 * SUBLANES
        wb_s[0:body, :] = w_ref[SUBLANES:SUBLANES + body, :].astype(BF16)
        wb_s[body:tn, :] = jnp.concatenate([w_ref[tn - SUBLANES:tn, :], nxt_ref[...]], axis=0).astype(BF16)

    acc = lax.dot_general(a_ref[...], wb_s[...], (((1,), (1,)), ((), ())), preferred_element_type=F32)
    o_ref[...] = acc * jnp.where((j >= scaled_tiles[0]) & (j < scaled_tiles[1]), scale, 1.0)


def _in_proj(a, wt, gap_row, n_out, scaled_cols, scale, cast_w, cast_slabs):
    m, k = a.shape
    tm, tn = MM_TILE, MM_TILE // 2
    assert gap_row % tn == 0 and n_out % tn == 0 and wt.shape[0] == n_out + SUBLANES
    assert scaled_cols[0] % tn == 0 and scaled_cols[1] % tn == 0
    per = tn // SUBLANES
    grid = (n_out // tn, m // tm)
    cspec, cshape = _side_cast(cast_w, grid, cast_slabs)
    return pl.pallas_call(
        functools.partial(_in_proj_kernel, first_shifted=gap_row // tn,
                          scaled_tiles=(scaled_cols[0] // tn, scaled_cols[1] // tn), scale=scale),
        grid=grid,
        in_specs=[pl.BlockSpec((tm, k), lambda j, i: (i, 0)),
                  pl.BlockSpec((tn, k), lambda j, i: (j, 0)),
                  pl.BlockSpec((SUBLANES, k), lambda j, i: ((j + 1) * per, 0)),
                  cspec],
        out_specs=[pl.BlockSpec((tm, tn), lambda j, i: (i, j)), cspec],
        out_shape=[jax.ShapeDtypeStruct((m, n_out), F32), cshape],
        scratch_shapes=[pltpu.VMEM((tn, k), BF16)],
        compiler_params=_params("arbitrary", "arbitrary"),
        name="in_proj",
    )(a, wt, wt, cast_w)


def _gate_proj_kernel(a_ref, wt_ref, o_ref):
    o_ref[...] = lax.dot_general(a_ref[...], wt_ref[...].astype(BF16), (((1,), (1,)), ((), ())),
                                 preferred_element_type=F32)


def _gate_proj(a, wt):
    m, k = a.shape
    n = wt.shape[0]
    tm = MM_TILE
    return pl.pallas_call(
        _gate_proj_kernel,
        grid=(m // tm,),
        in_specs=[pl.BlockSpec((tm, k), lambda i: (i, 0)),
                  pl.BlockSpec((n, k), lambda i: (0, 0))],
        out_specs=pl.BlockSpec((tm, n), lambda i: (i, 0)),
        out_shape=jax.ShapeDtypeStruct((m, n), F32),
        compiler_params=_params("parallel"),
        name="in_proj_gates",
    )(a, wt)


def _up_proj_kernel(a_ref, ssq_ref, b_ref, wsrc_ref, o_ref, wdst_ref, *, d_norm):
    wdst_ref[...] = wsrc_ref[...].astype(wdst_ref.dtype)
    acc = jnp.dot(a_ref[...], b_ref[...], preferred_element_type=F32)
    acc = acc * lax.rsqrt(ssq_ref[:, 0:1] * (1.0 / d_norm) + EPS)
    acc = jnp.maximum(acc, 0.0)
    o_ref[...] = (acc * acc).astype(o_ref.dtype)


def _up_proj(xg, ssq, w, cast_w):
    m, k = xg.shape
    n = w.shape[1]
    tm = tn = MM_TILE
    grid = (m // tm, n // tn)
    cspec, cshape = _side_cast(cast_w, grid)
    return pl.pallas_call(
        functools.partial(_up_proj_kernel, d_norm=k),
        grid=grid,
        in_specs=[pl.BlockSpec((tm, k), lambda i, j: (i, 0)),
                  pl.BlockSpec((tm, LANES), lambda i, j: (i, 0)),
                  pl.BlockSpec((k, tn), lambda i, j: (0, j)),
                  cspec],
        out_specs=[pl.BlockSpec((tm, tn), lambda i, j: (i, j)), cspec],
        out_shape=[jax.ShapeDtypeStruct((m, n), BF16), cshape],
        compiler_params=_params("parallel", "parallel"),
        name="up_proj",
    )(xg, ssq, w, cast_w)


def _out_proj_kernel(a0_ref, a1_ref, b0_ref, b1_ref, r_ref, g_ref, o_ref, xg_ref, ssq_ref):
    acc = jnp.dot(a0_ref[...], b0_ref[...], preferred_element_type=F32)
    acc = acc + jnp.dot(a1_ref[...], b1_ref[...], preferred_element_type=F32)
    x1 = r_ref[...] + acc
    o_ref[...] = x1
    xg_ref[...] = (x1 * g_ref[...]).astype(xg_ref.dtype)
    part = jnp.broadcast_to(jnp.sum(x1 * x1, axis=-1, keepdims=True), ssq_ref.shape)
    j = pl.program_id(1)

    @pl.when(j == 0)
    def _():
        ssq_ref[...] = part

    @pl.when(j > 0)
    def _():
        ssq_ref[...] += part


def _out_proj(y_m, y_a, w, res, gain):
    m, k0 = y_m.shape
    k1 = y_a.shape[1]
    assert k0 == k1
    n = w.shape[1]
    tm, tn = MM_TILE, MM_TILE // 2
    return pl.pallas_call(
        _out_proj_kernel,
        grid=(m // tm, n // tn),
        in_specs=[pl.BlockSpec((tm, k0), lambda i, j: (i, 0)),
                  pl.BlockSpec((tm, k1), lambda i, j: (i, 0)),
                  pl.BlockSpec((k0, tn), lambda i, j: (0, j)),
                  pl.BlockSpec((k1, tn), lambda i, j: (1, j)),
                  pl.BlockSpec((tm, tn), lambda i, j: (i, j)),
                  pl.BlockSpec((1, tn), lambda i, j: (0, j))],
        out_specs=[pl.BlockSpec((tm, tn), lambda i, j: (i, j)),
                   pl.BlockSpec((tm, tn), lambda i, j: (i, j)),
                   pl.BlockSpec((tm, LANES), lambda i, j: (i, 0))],
        out_shape=[jax.ShapeDtypeStruct((m, n), F32),
                   jax.ShapeDtypeStruct((m, n), BF16),
                   jax.ShapeDtypeStruct((m, LANES), F32)],
        compiler_params=_params("parallel", "arbitrary"),
        name="out_proj",
    )(y_m, y_a, w, w, res, gain.reshape(1, n))


def _mmk_res_kernel(a_ref, b_ref, r_ref, o_ref):
    @pl.when(pl.program_id(2) == 0)
    def _():
        o_ref[...] = r_ref[...]

    o_ref[...] += jnp.dot(a_ref[...], b_ref[...], preferred_element_type=F32)


def _matmul_ktiled_res(a, b, res):
    m, k = a.shape
    n = b.shape[1]
    tm = tn = MM_TILE
    tk = MM_K_TILE
    return pl.pallas_call(
        _mmk_res_kernel,
        grid=(m // tm, n // tn, k // tk),
        in_specs=[pl.BlockSpec((tm, tk), lambda i, j, kk: (i, kk)),
                  pl.BlockSpec((tk, tn), lambda i, j, kk: (kk, j)),
                  pl.BlockSpec((tm, tn), lambda i, j, kk: (i, j))],
        out_specs=pl.BlockSpec((tm, tn), lambda i, j, kk: (i, j)),
        out_shape=jax.ShapeDtypeStruct((m, n), F32),
        compiler_params=_params("parallel", "parallel", "arbitrary"),
        name="down_proj",
    )(a, b, res)


def _split3(a):
    hi = a.astype(BF16)
    r1 = a - hi.astype(F32)
    mid = r1.astype(BF16)
    lo = (r1 - mid.astype(F32)).astype(BF16)
    return hi, mid, lo


def _conv_silu(u_ref, tail_ref, ext_ref, w_ref, b_ref):
    chunk = u_ref.shape[0]
    u = u_ref[...]
    ext_ref[0:SUBLANES, :] = tail_ref[...]
    ext_ref[SUBLANES:SUBLANES + chunk, :] = u
    tail_ref[...] = u[chunk - SUBLANES:, :]
    w = w_ref[...]
    y = b_ref[...] + w[CONV_WIDTH - 1:CONV_WIDTH, :] * u
    for j in range(CONV_WIDTH - 1):
        start = SUBLANES - (CONV_WIDTH - 1) + j
        y = y + w[j:j + 1, :] * ext_ref[start:start + chunk, :]
    return y * _sigmoid(y)


def _sigmoid(z):
    return 0.5 + 0.5 * jnp.tanh(0.5 * z)


def _mlstm_kernel(qk_ref, v_ref, o_ref, g_ref, gb_ref, cw_ref, cb_ref, nrm_ref, y_ref, c_s, n_s, tail_s, ext_s):
    chunk = qk_ref.shape[0]
    dk, dv = MLSTM_DQK, MLSTM_DV

    @pl.when(pl.program_id(0) == 0)
    def _():
        c_s[...] = jnp.zeros_like(c_s)
        n_s[...] = jnp.zeros_like(n_s)
        tail_s[...] = jnp.zeros_like(tail_s)

    qk = _conv_silu(qk_ref, tail_s, ext_s, cw_ref, cb_ref)

    pre = GATE_SOFTCAP * jnp.tanh((g_ref[...] + gb_ref[...]) * (1.0 / GATE_SOFTCAP))
    logsig = jnp.minimum(pre, 0.0) - jnp.log(1.0 + jnp.exp(-jnp.abs(pre)))
    row = lax.broadcasted_iota(jnp.int32, (chunk, chunk), 0)
    col = lax.broadcasted_iota(jnp.int32, (chunk, chunk), 1)
    causal = row >= col
    tri = causal.astype(BF16)
    cum = sum(jnp.dot(tri, piece, preferred_element_type=F32) for piece in _split3(logsig))
    pre_t = pre.T
    cum_t = cum.T

    for hd in range(MLSTM_HEADS):
        g0 = 2 * hd
        q = qk[:, hd * dk:(hd + 1) * dk] * (dk ** -0.5)
        k = qk[:, (MLSTM_HEADS + hd) * dk:(MLSTM_HEADS + hd + 1) * dk]
        qb = q.astype(BF16)
        v = v_ref[:, hd * dv:(hd + 1) * dv].astype(BF16)
        li_col = pre[:, g0:g0 + 1]
        b_col = cum[:, g0 + 1:g0 + 2]
        b_last = cum[chunk - 1:chunk, g0 + 1:g0 + 2]
        r_col = li_col - b_col
        r_row = pre_t[g0:g0 + 1, :] - cum_t[g0 + 1:g0 + 2, :]

        dmat = jnp.where(causal, b_col + r_row, -jnp.inf)
        s_qk = lax.dot_general(qb, k.astype(BF16), (((1,), (1,)), ((), ())), preferred_element_type=F32)
        p = s_qk * jnp.exp(dmat)
        inter_w = jnp.exp(b_col)
        num = jnp.dot(p.astype(BF16), v, preferred_element_type=F32)
        num = num + inter_w * jnp.dot(qb, c_s[hd].astype(BF16), preferred_element_type=F32)
        den = jnp.sum(p, axis=-1, keepdims=True) + inter_w * jnp.sum(q * n_s[hd], axis=-1, keepdims=True)
        h = num / jnp.maximum(jnp.abs(den), 1.0)

        kw = k * jnp.exp(b_last + r_col)
        decay = jnp.exp(b_last)
        c_s[hd] = decay * c_s[hd] + lax.dot_general(kw.astype(BF16), v, (((0,), (0,)), ((), ())),
                                                    preferred_element_type=F32)
        n_s[hd] = decay * n_s[hd] + jnp.sum(kw, axis=0, keepdims=True)

        hn = h * lax.rsqrt(jnp.mean(h * h, axis=-1, keepdims=True) + EPS) * nrm_ref[:, hd * dv:(hd + 1) * dv]
        gate = _sigmoid(o_ref[:, hd * dv:(hd + 1) * dv].astype(F32))
        y_ref[:, hd * dv:(hd + 1) * dv] = (gate * hn).astype(y_ref.dtype)


def _mlstm(proj, gates, gate_bias, conv_w, conv_b, mlstm_norm):
    s = proj.shape[0]
    L = MLSTM_CHUNK
    H = MLSTM_HEADS
    dk, dv = MLSTM_DQK, MLSTM_DV
    qkw = 2 * MLSTM_QK_WIDTH
    assert qkw == MLSTM_WIDTH
    return pl.pallas_call(
        _mlstm_kernel,
        grid=(s // L,),
        in_specs=[pl.BlockSpec((L, qkw), lambda c: (c, 0)),
                  pl.BlockSpec((L, MLSTM_WIDTH), lambda c: (c, 1)),
                  pl.BlockSpec((L, MLSTM_WIDTH), lambda c: (c, 2)),
                  pl.BlockSpec((L, LANES), lambda c: (c, 0)),
                  pl.BlockSpec((1, LANES), lambda c: (0, 0)),
                  pl.BlockSpec((CONV_WIDTH, qkw), lambda c: (0, 0)),
                  pl.BlockSpec((1, qkw), lambda c: (0, 0)),
                  pl.BlockSpec((1, MLSTM_WIDTH), lambda c: (0, 0))],
        out_specs=pl.BlockSpec((L, MLSTM_WIDTH), lambda c: (c, 0)),
        out_shape=jax.ShapeDtypeStruct((s, MLSTM_WIDTH), BF16),
        scratch_shapes=[pltpu.VMEM((H, dk, dv), F32),
                        pltpu.VMEM((H, 1, dk), F32),
                        pltpu.VMEM((SUBLANES, qkw), F32),
                        pltpu.VMEM((SUBLANES + L, qkw), F32)],
        compiler_params=_params("arbitrary"),
        name="mlstm",
    )(proj, proj, proj, gates, gate_bias, conv_w, conv_b, mlstm_norm)


def _band_softmax(q, k, v, bias):
    s = lax.dot_general(q, k, (((1,), (1,)), ((), ())), preferred_element_type=F32) + bias
    m = jnp.max(s, axis=-1, keepdims=True)
    p = jnp.exp2(s - m)
    l = jnp.sum(p, axis=-1, keepdims=True)
    o = jnp.dot(p.astype(BF16), v, preferred_element_type=F32) / l
    return o, m + jnp.log2(l)


def _attn_kernel(q_ref, kp_ref, kc_ref, vp_ref, vc_ref, wsrc_ref, y_ref, wdst_ref,
                 q4_s, k4_s, v4_s, o_s, lse_s, bias_s):
    blk = ATTN_BLOCK
    sup = q_ref.shape[0]
    nblk = sup // blk
    d1, d2 = DILATIONS[1], DILATIONS[2]
    sub = sup // d1
    sb = pl.program_id(1)
    cur = (sb & 1) * d1
    prv = d1 - cur

    wdst_ref[...] = wsrc_ref[...].astype(wdst_ref.dtype)

    @pl.when(sb == 0)
    def _():
        for r in range(d1):
            k4_s[prv + r] = jnp.zeros((sub, LANES), F32)
            v4_s[prv + r] = jnp.zeros((sub, LANES), F32)

    for r in range(d1):
        q4_s[r] = q_ref[pl.ds(r, sub, stride=d1), :]
        k4_s[cur + r] = kc_ref[pl.ds(r, sub, stride=d1), :]
        v4_s[cur + r] = vc_ref[pl.ds(r, sub, stride=d1), :]

    qi = lax.broadcasted_iota(jnp.int32, (blk, 2 * blk), 0)
    kj = lax.broadcasted_iota(jnp.int32, (blk, 2 * blk), 1)
    band = (kj >= qi) & (kj <= qi + blk)
    bias_s[0] = jnp.where(band, 0.0, -jnp.inf).astype(F32)
    bias_s[1] = jnp.where(band & (kj >= blk), 0.0, -jnp.inf).astype(F32)

    def body(t, carry):
        def run(p_idx, q, k, v, out_rows, first):
            bias = bias_s[first.astype(jnp.int32)]
            o, lse = _band_softmax(q.astype(BF16), k.astype(BF16), v.astype(BF16), bias)
            o_s[p_idx, out_rows, :] = o
            lse_s[p_idx, out_rows, :] = jnp.broadcast_to(lse, (blk, LANES))

        q0 = pl.multiple_of(t * blk, blk)
        qp = pl.multiple_of(jnp.maximum(t - 1, 0) * blk, blk)
        at_start = t == 0
        k_prev = jnp.where(at_start, kp_ref[...], kc_ref[pl.ds(qp, blk), :])
        v_prev = jnp.where(at_start, vp_ref[...], vc_ref[pl.ds(qp, blk), :])
        run(0, q_ref[pl.ds(q0, blk), :],
            jnp.concatenate([k_prev, kc_ref[pl.ds(q0, blk), :]], axis=0),
            jnp.concatenate([v_prev, vc_ref[pl.ds(q0, blk), :]], axis=0),
            pl.ds(q0, blk), (sb == 0) & at_start)

        sg = lax.shift_right_logical(t, d1.bit_length() - 1)
        r = t & (d1 - 1)
        s0 = pl.multiple_of(sg * blk, blk)
        seg0 = sg == 0
        before = (jnp.where(seg0, prv, cur) + r,
                  pl.ds(pl.multiple_of(jnp.where(seg0, sub - blk, s0 - blk), blk), blk))
        own = (cur + r, pl.ds(s0, blk))
        run(1, q4_s[r, pl.ds(s0, blk), :],
            jnp.concatenate([k4_s[before[0], before[1], :], k4_s[own[0], own[1], :]], axis=0),
            jnp.concatenate([v4_s[before[0], before[1], :], v4_s[own[0], own[1], :]], axis=0),
            pl.ds(sg * (d1 * blk) + r, blk, stride=d1), (sb == 0) & seg0)

        walk = pl.ds(sg, blk, stride=d2 // d1)
        run(2, q4_s[r, walk, :],
            jnp.concatenate([k4_s[prv + r, walk, :], k4_s[cur + r, walk, :]], axis=0),
            jnp.concatenate([v4_s[prv + r, walk, :], v4_s[cur + r, walk, :]], axis=0),
            pl.ds(t, blk, stride=d2), sb == 0)
        return carry

    lax.fori_loop(0, nblk, body, 0, unroll=ATTN_UNROLL)

    def combine(t, carry):
        rows = pl.ds(pl.multiple_of(t * blk, blk), blk)
        l0, l1, l2 = lse_s[0, rows, :], lse_s[1, rows, :], lse_s[2, rows, :]
        mx = jnp.maximum(jnp.maximum(l0, l1), l2)
        w0, w1, w2 = jnp.exp2(l0 - mx), jnp.exp2(l1 - mx), jnp.exp2(l2 - mx)
        out = (w0 * o_s[0, rows, :] + w1 * o_s[1, rows, :] + w2 * o_s[2, rows, :]) / (w0 + w1 + w2)
        y_ref[rows, :] = out.astype(y_ref.dtype)
        return carry

    lax.fori_loop(0, nblk, combine, 0)


def _dilated_attention(pf, col0, cast_w):
    s = pf.shape[0]
    sup = ATTN_SUPER
    hb = ATTN_HEADS
    d1 = DILATIONS[1]
    qb, kb, vb = (col0 // LANES + i * hb for i in range(3))
    blk = ATTN_BLOCK
    nblk = sup // blk
    tail = lambda b: jnp.maximum(b * nblk - 1, 0)
    grid = (hb, s // sup)
    cspec, cshape = _side_cast(cast_w, grid)
    return pl.pallas_call(
        _attn_kernel,
        grid=grid,
        in_specs=[pl.BlockSpec((sup, LANES), lambda h, b: (b, qb + h)),
                  pl.BlockSpec((blk, LANES), lambda h, b: (tail(b), kb + h)),
                  pl.BlockSpec((sup, LANES), lambda h, b: (b, kb + h)),
                  pl.BlockSpec((blk, LANES), lambda h, b: (tail(b), vb + h)),
                  pl.BlockSpec((sup, LANES), lambda h, b: (b, vb + h)),
                  cspec],
        out_specs=[pl.BlockSpec((sup, LANES), lambda h, b: (b, h)), cspec],
        out_shape=[jax.ShapeDtypeStruct((s, ATTN_WIDTH), BF16), cshape],
        scratch_shapes=[pltpu.VMEM((d1, sup // d1, LANES), F32),
                        pltpu.VMEM((2 * d1, sup // d1, LANES), F32),
                        pltpu.VMEM((2 * d1, sup // d1, LANES), F32),
                        pltpu.VMEM((len(DILATIONS), sup, LANES), F32),
                        pltpu.VMEM((len(DILATIONS), sup, LANES), F32),
                        pltpu.VMEM((2, blk, 2 * blk), F32)],
        compiler_params=_params("parallel", "arbitrary"),
        name="dilated_attn",
    )(pf, pf, pf, pf, pf, cast_w)


def _gate_weights(w_in_t, i_bias_l, f_bias_l):
    g0 = 2 * MLSTM_QK_WIDTH + 2 * MLSTM_WIDTH
    H = MLSTM_HEADS
    wg = jnp.stack([w_in_t[g0:g0 + H], w_in_t[g0 + H:g0 + 2 * H]], axis=1).reshape(2 * H, w_in_t.shape[1])
    wg = jnp.pad(wg, ((0, LANES - 2 * H), (0, 0)))
    bg = jnp.stack([i_bias_l, f_bias_l], axis=-1).reshape(1, 2 * H)
    bg = jnp.pad(bg, ((0, 0), (0, LANES - 2 * H)))
    return wg, bg.astype(F32)


def _layer(x, norm_mix, w_in, conv_w, conv_b, i_bias, f_bias, mlstm_norm, w_out, norm_mlp, w_up, w_down):
    qk_w = 2 * MLSTM_QK_WIDTH
    g0 = qk_w + 2 * MLSTM_WIDTH
    a0 = g0 + 2 * MLSTM_HEADS
    n_proj = w_in.shape[1] - 2 * MLSTM_HEADS
    assert a0 - g0 == SUBLANES
    w_in_t = w_in.T
    w_g, b_g = _gate_weights(w_in_t, i_bias, f_bias)

    h = _rmsnorm(x, norm_mix, BF16)
    proj, w_up_b = _in_proj(h, w_in_t, g0, n_proj, (g0, g0 + ATTN_WIDTH), ATTN_QSCALE, w_up, CAST_SLABS)
    gates = _gate_proj(h, w_g)
    y_a, w_out_b = _dilated_attention(proj, g0, w_out)
    y_m = _mlstm(proj, gates, b_g, conv_w, conv_b.reshape(1, -1), mlstm_norm.reshape(1, -1))
    x1, x1g, ssq = _out_proj(y_m, y_a, w_out_b, x, norm_mlp)
    u, w_down_b = _up_proj(x1g, ssq, w_up_b, w_down)
    return _matmul_ktiled_res(u, w_down_b, x1)


def kernel(x, norm_mix, w_in, conv_w, conv_b, i_bias, f_bias, mlstm_norm, w_out, norm_mlp, w_up, w_down, norm_final):
    b, s, d = x.shape
    outs = []
    for bi in range(b):
        xb = x[bi]
        for layer in range(norm_mix.shape[0]):
            xb = _layer(xb, norm_mix[layer], w_in[layer], conv_w[layer], conv_b[layer], i_bias[layer], f_bias[layer],
                        mlstm_norm[layer], w_out[layer], norm_mlp[layer], w_up[layer], w_down[layer])
        outs.append(_rmsnorm(xb, norm_final, x.dtype))
    return jnp.stack(outs, axis=0)
```

```python
import functools

import jax
import jax.numpy as jnp
from jax import lax
from jax.experimental import pallas as pl
from jax.experimental.pallas import tpu as pltpu

F32 = jnp.float32
BF16 = jnp.bfloat16

D_MODEL = 4096
MLSTM_HEADS = 4
MLSTM_DV = 512
MLSTM_DQK = 256
MLSTM_QK_WIDTH = MLSTM_HEADS * MLSTM_DQK
MLSTM_WIDTH = MLSTM_HEADS * MLSTM_DV
CONV_WIDTH = 4
GATE_SOFTCAP = 15.0
ATTN_HEAD_DIM = 128
ATTN_HEADS = 16
ATTN_WIDTH = ATTN_HEADS * ATTN_HEAD_DIM
ATTN_BLOCK = 128
DILATIONS = (1, 4, 16)
D_FF = 4 * D_MODEL
EPS = 1e-6

LANES = 128
SUBLANES = 8
VMEM_LIMIT_BYTES = 58 * 1024 * 1024

MM_TILE = 1024
MM_K_TILE = 4096
NORM_ROWS = 512
CAST_SLABS = 128
MLSTM_CHUNK = 256
ATTN_SUPER = DILATIONS[-1] * ATTN_BLOCK
ATTN_UNROLL = 16
ATTN_QSCALE = 1.4426950408889634 * ATTN_HEAD_DIM ** -0.5


def _params(*sem):
    return pltpu.CompilerParams(dimension_semantics=sem, vmem_limit_bytes=VMEM_LIMIT_BYTES)


def _rmsnorm_kernel(x_ref, g_ref, o_ref):
    x = x_ref[...]
    ms = jnp.mean(x * x, axis=-1, keepdims=True)
    o_ref[...] = (x * lax.rsqrt(ms + EPS) * g_ref[...]).astype(o_ref.dtype)


def _rmsnorm(x, g, out_dtype):
    s, d = x.shape
    return pl.pallas_call(
        _rmsnorm_kernel,
        grid=(s // NORM_ROWS,),
        in_specs=[pl.BlockSpec((NORM_ROWS, d), lambda i: (i, 0)),
                  pl.BlockSpec((1, d), lambda i: (0, 0))],
        out_specs=pl.BlockSpec((NORM_ROWS, d), lambda i: (i, 0)),
        out_shape=jax.ShapeDtypeStruct((s, d), out_dtype),
        compiler_params=_params("parallel"),
        name="rmsnorm",
    )(x, g.reshape(1, d))


def _side_cast(w, grid, slabs=None):
    steps = 1
    for g in grid:
        steps *= g
    slabs = steps if slabs is None else slabs
    rows = w.shape[0] // slabs
    assert slabs <= steps and rows * slabs == w.shape[0] and rows % (2 * SUBLANES) == 0

    def index_map(*ids):
        lin = ids[0]
        for g, i in zip(grid[1:], ids[1:]):
            lin = lin * g + i
        return (jnp.minimum(lin, slabs - 1), 0)

    spec = pl.BlockSpec((rows, w.shape[1]), index_map)
    return spec, jax.ShapeDtypeStruct(w.shape, BF16)


def _in_proj_kernel(a_ref, w_ref, nxt_ref, wsrc_ref, o_ref, wdst_ref, wb_s, *, first_shifted, scaled_tiles, scale):
    wdst_ref[...] = wsrc_ref[...].astype(wdst_ref.dtype)
    j = pl.program_id(0)
    first_row_tile = pl.program_id(1) == 0

    @pl.when(first_row_tile & (j < first_shifted))
    def _():
        wb_s[...] = w_ref[...].astype(BF16)

    @pl.when(first_row_tile & (j >= first_shifted))
    def _():
        tn = w_ref.shape[0]
        body = tn - 2 * SUBLANES
        wb_s[0:body, :] = w_ref[SUBLANES:SUBLANES + body, :].astype(BF16)
        wb_s[body:tn, :] = jnp.concatenate([w_ref[tn - SUBLANES:tn, :], nxt_ref[...]], axis=0).astype(BF16)

    acc = lax.dot_general(a_ref[...], wb_s[...], (((1,), (1,)), ((), ())), preferred_element_type=F32)
    o_ref[...] = acc * jnp.where((j >= scaled_tiles[0]) & (j < scaled_tiles[1]), scale, 1.0)


def _in_proj(a, wt, gap_row, n_out, scaled_cols, scale, cast_w, cast_slabs):
    m, k = a.shape
    tm, tn = MM_TILE, MM_TILE // 2
    assert gap_row % tn == 0 and n_out % tn == 0 and wt.shape[0] == n_out + SUBLANES
    assert scaled_cols[0] % tn == 0 and scaled_cols[1] % tn == 0
    per = tn // SUBLANES
    grid = (n_out // tn, m // tm)
    cspec, cshape = _side_cast(cast_w, grid, cast_slabs)
    return pl.pallas_call(
        functools.partial(_in_proj_kernel, first_shifted=gap_row // tn,
                          scaled_tiles=(scaled_cols[0] // tn, scaled_cols[1] // tn), scale=scale),
        grid=grid,
        in_specs=[pl.BlockSpec((tm, k), lambda j, i: (i, 0)),
                  pl.BlockSpec((tn, k), lambda j, i: (j, 0)),
                  pl.BlockSpec((SUBLANES, k), lambda j, i: ((j + 1) * per, 0)),
                  cspec],
        out_specs=[pl.BlockSpec((tm, tn), lambda j, i: (i, j)), cspec],
        out_shape=[jax.ShapeDtypeStruct((m, n_out), F32), cshape],
        scratch_shapes=[pltpu.VMEM((tn, k), BF16)],
        compiler_params=_params("arbitrary", "arbitrary"),
        name="in_proj",
    )(a, wt, wt, cast_w)


def _gate_proj_kernel(a_ref, wt_ref, o_ref):
    o_ref[...] = lax.dot_general(a_ref[...], wt_ref[...].astype(BF16), (((1,), (1,)), ((), ())),
                                 preferred_element_type=F32)


def _gate_proj(a, wt):
    m, k = a.shape
    n = wt.shape[0]
    tm = MM_TILE
    return pl.pallas_call(
        _gate_proj_kernel,
        grid=(m // tm,),
        in_specs=[pl.BlockSpec((tm, k), lambda i: (i, 0)),
                  pl.BlockSpec((n, k), lambda i: (0, 0))],
        out_specs=pl.BlockSpec((tm, n), lambda i: (i, 0)),
        out_shape=jax.ShapeDtypeStruct((m, n), F32),
        compiler_params=_params("parallel"),
        name="in_proj_gates",
    )(a, wt)


def _up_proj_kernel(a_ref, ssq_ref, b_ref, wsrc_ref, o_ref, wdst_ref, *, d_norm):
    wdst_ref[...] = wsrc_ref[...].astype(wdst_ref.dtype)
    acc = jnp.dot(a_ref[...], b_ref[...], preferred_element_type=F32)
    acc = acc * lax.rsqrt(ssq_ref[:, 0:1] * (1.0 / d_norm) + EPS)
    acc = jnp.maximum(acc, 0.0)
    o_ref[...] = (acc * acc).astype(o_ref.dtype)


def _up_proj(xg, ssq, w, cast_w):
    m, k = xg.shape
    n = w.shape[1]
    tm = tn = MM_TILE
    grid = (m // tm, n // tn)
    cspec, cshape = _side_cast(cast_w, grid)
    return pl.pallas_call(
        functools.partial(_up_proj_kernel, d_norm=k),
        grid=grid,
        in_specs=[pl.BlockSpec((tm, k), lambda i, j: (i, 0)),
                  pl.BlockSpec((tm, LANES), lambda i, j: (i, 0)),
                  pl.BlockSpec((k, tn), lambda i, j: (0, j)),
                  cspec],
        out_specs=[pl.BlockSpec((tm, tn), lambda i, j: (i, j)), cspec],
        out_shape=[jax.ShapeDtypeStruct((m, n), BF16), cshape],
        compiler_params=_params("parallel", "parallel"),
        name="up_proj",
    )(xg, ssq, w, cast_w)


def _out_proj_kernel(a0_ref, a1_ref, b0_ref, b1_ref, r_ref, g_ref, o_ref, xg_ref, ssq_ref):
    acc = jnp.dot(a0_ref[...], b0_ref[...], preferred_element_type=F32)
    acc = acc + jnp.dot(a1_ref[...], b1_ref[...], preferred_element_type=F32)
    x1 = r_ref[...] + acc
    o_ref[...] = x1
    xg_ref[...] = (x1 * g_ref[...]).astype(xg_ref.dtype)
    part = jnp.broadcast_to(jnp.sum(x1 * x1, axis=-1, keepdims=True), ssq_ref.shape)
    j = pl.program_id(1)

    @pl.when(j == 0)
    def _():
        ssq_ref[...] = part

    @pl.when(j > 0)
    def _():
        ssq_ref[...] += part


def _out_proj(y_m, y_a, w, res, gain):
    m, k0 = y_m.shape
    k1 = y_a.shape[1]
    assert k0 == k1
    n = w.shape[1]
    tm, tn = MM_TILE, MM_TILE // 2
    return pl.pallas_call(
        _out_proj_kernel,
        grid=(m // tm, n // tn),
        in_specs=[pl.BlockSpec((tm, k0), lambda i, j: (i, 0)),
                  pl.BlockSpec((tm, k1), lambda i, j: (i, 0)),
                  pl.BlockSpec((k0, tn), lambda i, j: (0, j)),
                  pl.BlockSpec((k1, tn), lambda i, j: (1, j)),
                  pl.BlockSpec((tm, tn), lambda i, j: (i, j)),
                  pl.BlockSpec((1, tn), lambda i, j: (0, j))],
        out_specs=[pl.BlockSpec((tm, tn), lambda i, j: (i, j)),
                   pl.BlockSpec((tm, tn), lambda i, j: (i, j)),
                   pl.BlockSpec((tm, LANES), lambda i, j: (i, 0))],
        out_shape=[jax.ShapeDtypeStruct((m, n), F32),
                   jax.ShapeDtypeStruct((m, n), BF16),
                   jax.ShapeDtypeStruct((m, LANES), F32)],
        compiler_params=_params("parallel", "arbitrary"),
        name="out_proj",
    )(y_m, y_a, w, w, res, gain.reshape(1, n))


def _mmk_res_kernel(a_ref, b_ref, r_ref, o_ref):
    @pl.when(pl.program_id(2) == 0)
    def _():
        o_ref[...] = r_ref[...]

    o_ref[...] += jnp.dot(a_ref[...], b_ref[...], preferred_element_type=F32)


def _matmul_ktiled_res(a, b, res):
    m, k = a.shape
    n = b.shape[1]
    tm = tn = MM_TILE
    tk = MM_K_TILE
    return pl.pallas_call(
        _mmk_res_kernel,
        grid=(m // tm, n // tn, k // tk),
        in_specs=[pl.BlockSpec((tm, tk), lambda i, j, kk: (i, kk)),
                  pl.BlockSpec((tk, tn), lambda i, j, kk: (kk, j)),
                  pl.BlockSpec((tm, tn), lambda i, j, kk: (i, j))],
        out_specs=pl.BlockSpec((tm, tn), lambda i, j, kk: (i, j)),
        out_shape=jax.ShapeDtypeStruct((m, n), F32),
        compiler_params=_params("parallel", "parallel", "arbitrary"),
        name="down_proj",
    )(a, b, res)


def _split3(a):
    hi = a.astype(BF16)
    r1 = a - hi.astype(F32)
    mid = r1.astype(BF16)
    lo = (r1 - mid.astype(F32)).astype(BF16)
    return hi, mid, lo


def _conv_silu(u_ref, tail_ref, ext_ref, w_ref, b_ref):
    chunk = u_ref.shape[0]
    u = u_ref[...]
    ext_ref[0:SUBLANES, :] = tail_ref[...]
    ext_ref[SUBLANES:SUBLANES + chunk, :] = u
    tail_ref[...] = u[chunk - SUBLANES:, :]
    w = w_ref[...]
    y = b_ref[...] + w[CONV_WIDTH - 1:CONV_WIDTH, :] * u
    for j in range(CONV_WIDTH - 1):
        start = SUBLANES - (CONV_WIDTH - 1) + j
        y = y + w[j:j + 1, :] * ext_ref[start:start + chunk, :]
    return y * _sigmoid(y)


def _sigmoid(z):
    return 0.5 + 0.5 * jnp.tanh(0.5 * z)


def _mlstm_kernel(qk_ref, v_ref, o_ref, g_ref, gb_ref, cw_ref, cb_ref, nrm_ref, y_ref, c_s, n_s, tail_s, ext_s):
    chunk = qk_ref.shape[0]
    dk, dv = MLSTM_DQK, MLSTM_DV

    @pl.when(pl.program_id(0) == 0)
    def _():
        c_s[...] = jnp.zeros_like(c_s)
        n_s[...] = jnp.zeros_like(n_s)
        tail_s[...] = jnp.zeros_like(tail_s)

    qk = _conv_silu(qk_ref, tail_s, ext_s, cw_ref, cb_ref)

    pre = GATE_SOFTCAP * jnp.tanh((g_ref[...] + gb_ref[...]) * (1.0 / GATE_SOFTCAP))
    logsig = jnp.minimum(pre, 0.0) - jnp.log(1.0 + jnp.exp(-jnp.abs(pre)))
    row = lax.broadcasted_iota(jnp.int32, (chunk, chunk), 0)
    col = lax.broadcasted_iota(jnp.int32, (chunk, chunk), 1)
    causal = row >= col
    tri = causal.astype(BF16)
    cum = sum(jnp.dot(tri, piece, preferred_element_type=F32) for piece in _split3(logsig))
    pre_t = pre.T
    cum_t = cum.T

    for hd in range(MLSTM_HEADS):
        g0 = 2 * hd
        q = qk[:, hd * dk:(hd + 1) * dk] * (dk ** -0.5)
        k = qk[:, (MLSTM_HEADS + hd) * dk:(MLSTM_HEADS + hd + 1) * dk]
        qb = q.astype(BF16)
        v = v_ref[:, hd * dv:(hd + 1) * dv].astype(BF16)
        li_col = pre[:, g0:g0 + 1]
        b_col = cum[:, g0 + 1:g0 + 2]
        b_last = cum[chunk - 1:chunk, g0 + 1:g0 + 2]
        r_col = li_col - b_col
        r_row = pre_t[g0:g0 + 1, :] - cum_t[g0 + 1:g0 + 2, :]

        dmat = jnp.where(causal, b_col + r_row, -jnp.inf)
        s_qk = lax.dot_general(qb, k.astype(BF16), (((1,), (1,)), ((), ())), preferred_element_type=F32)
        p = s_qk * jnp.exp(dmat)
        inter_w = jnp.exp(b_col)
        num = jnp.dot(p.astype(BF16), v, preferred_element_type=F32)
        num = num + inter_w * jnp.dot(qb, c_s[hd].astype(BF16), preferred_element_type=F32)
        den = jnp.sum(p, axis=-1, keepdims=True) + inter_w * jnp.sum(q * n_s[hd], axis=-1, keepdims=True)
        h = num / jnp.maximum(jnp.abs(den), 1.0)

        kw = k * jnp.exp(b_last + r_col)
        decay = jnp.exp(b_last)
        c_s[hd] = decay * c_s[hd] + lax.dot_general(kw.astype(BF16), v, (((0,), (0,)), ((), ())),
                                                    preferred_element_type=F32)
        n_s[hd] = decay * n_s[hd] + jnp.sum(kw, axis=0, keepdims=True)

        hn = h * lax.rsqrt(jnp.mean(h * h, axis=-1, keepdims=True) + EPS) * nrm_ref[:, hd * dv:(hd + 1) * dv]
        gate = _sigmoid(o_ref[:, hd * dv:(hd + 1) * dv].astype(F32))
        y_ref[:, hd * dv:(hd + 1) * dv] = (gate * hn).astype(y_ref.dtype)


def _mlstm(proj, gates, gate_bias, conv_w, conv_b, mlstm_norm):
    s = proj.shape[0]
    L = MLSTM_CHUNK
    H = MLSTM_HEADS
    dk, dv = MLSTM_DQK, MLSTM_DV
    qkw = 2 * MLSTM_QK_WIDTH
    assert qkw == MLSTM_WIDTH
    return pl.pallas_call(
        _mlstm_kernel,
        grid=(s // L,),
        in_specs=[pl.BlockSpec((L, qkw), lambda c: (c, 0)),
                  pl.BlockSpec((L, MLSTM_WIDTH), lambda c: (c, 1)),
                  pl.BlockSpec((L, MLSTM_WIDTH), lambda c: (c, 2)),
                  pl.BlockSpec((L, LANES), lambda c: (c, 0)),
                  pl.BlockSpec((1, LANES), lambda c: (0, 0)),
                  pl.BlockSpec((CONV_WIDTH, qkw), lambda c: (0, 0)),
                  pl.BlockSpec((1, qkw), lambda c: (0, 0)),
                  pl.BlockSpec((1, MLSTM_WIDTH), lambda c: (0, 0))],
        out_specs=pl.BlockSpec((L, MLSTM_WIDTH), lambda c: (c, 0)),
        out_shape=jax.ShapeDtypeStruct((s, MLSTM_WIDTH), BF16),
        scratch_shapes=[pltpu.VMEM((H, dk, dv), F32),
                        pltpu.VMEM((H, 1, dk), F32),
                        pltpu.VMEM((SUBLANES, qkw), F32),
                        pltpu.VMEM((SUBLANES + L, qkw), F32)],
        compiler_params=_params("arbitrary"),
        name="mlstm",
    )(proj, proj, proj, gates, gate_bias, conv_w, conv_b, mlstm_norm)


def _band_softmax(q, k, v, bias):
    s = lax.dot_general(q, k, (((1,), (1,)), ((), ())), preferred_element_type=F32) + bias
    m = jnp.max(s, axis=-1, keepdims=True)
    p = jnp.exp2(s - m)
    l = jnp.sum(p, axis=-1, keepdims=True)
    acc = jnp.dot(p.astype(BF16), v, preferred_element_type=F32)
    return acc, m, l


def _attn_kernel(q_ref, kp_ref, kc_ref, vp_ref, vc_ref, wsrc_ref, y_ref, wdst_ref,
                 q4_s, k4_s, v4_s, o_s, m_s, l_s, bias_s):
    blk = ATTN_BLOCK
    sup = q_ref.shape[0]
    nblk = sup // blk
    d1, d2 = DILATIONS[1], DILATIONS[2]
    sub = sup // d1
    sb = pl.program_id(1)
    cur = (sb & 1) * d1
    prv = d1 - cur

    wdst_ref[...] = wsrc_ref[...].astype(wdst_ref.dtype)

    @pl.when(sb == 0)
    def _():
        for r in range(d1):
            k4_s[prv + r] = jnp.zeros((sub, LANES), F32)
            v4_s[prv + r] = jnp.zeros((sub, LANES), F32)

    for r in range(d1):
        q4_s[r] = q_ref[pl.ds(r, sub, stride=d1), :]
        k4_s[cur + r] = kc_ref[pl.ds(r, sub, stride=d1), :]
        v4_s[cur + r] = vc_ref[pl.ds(r, sub, stride=d1), :]

    qi = lax.broadcasted_iota(jnp.int32, (blk, 2 * blk), 0)
    kj = lax.broadcasted_iota(jnp.int32, (blk, 2 * blk), 1)
    band = (kj >= qi) & (kj <= qi + blk)
    bias_s[0] = jnp.where(band, 0.0, -jnp.inf).astype(F32)
    bias_s[1] = jnp.where(band & (kj >= blk), 0.0, -jnp.inf).astype(F32)

    def body(t, carry):
        def run(p_idx, q, k, v, out_rows, first):
            bias = bias_s[first.astype(jnp.int32)]
            acc, m, l = _band_softmax(q.astype(BF16), k.astype(BF16), v.astype(BF16), bias)
            o_s[p_idx, out_rows, :] = acc
            m_s[p_idx, out_rows, :] = jnp.broadcast_to(m, (blk, LANES))
            l_s[p_idx, out_rows, :] = jnp.broadcast_to(l, (blk, LANES))

        q0 = pl.multiple_of(t * blk, blk)
        qp = pl.multiple_of(jnp.maximum(t - 1, 0) * blk, blk)
        at_start = t == 0
        k_prev = jnp.where(at_start, kp_ref[...], kc_ref[pl.ds(qp, blk), :])
        v_prev = jnp.where(at_start, vp_ref[...], vc_ref[pl.ds(qp, blk), :])
        run(0, q_ref[pl.ds(q0, blk), :],
            jnp.concatenate([k_prev, kc_ref[pl.ds(q0, blk), :]], axis=0),
            jnp.concatenate([v_prev, vc_ref[pl.ds(q0, blk), :]], axis=0),
            pl.ds(q0, blk), (sb == 0) & at_start)

        sg = lax.shift_right_logical(t, d1.bit_length() - 1)
        r = t & (d1 - 1)
        s0 = pl.multiple_of(sg * blk, blk)
        seg0 = sg == 0
        before = (jnp.where(seg0, prv, cur) + r,
                  pl.ds(pl.multiple_of(jnp.where(seg0, sub - blk, s0 - blk), blk), blk))
        own = (cur + r, pl.ds(s0, blk))
        run(1, q4_s[r, pl.ds(s0, blk), :],
            jnp.concatenate([k4_s[before[0], before[1], :], k4_s[own[0], own[1], :]], axis=0),
            jnp.concatenate([v4_s[before[0], before[1], :], v4_s[own[0], own[1], :]], axis=0),
            pl.ds(sg * (d1 * blk) + r, blk, stride=d1), (sb == 0) & seg0)

        walk = pl.ds(sg, blk, stride=d2 // d1)
        run(2, q4_s[r, walk, :],
            jnp.concatenate([k4_s[prv + r, walk, :], k4_s[cur + r, walk, :]], axis=0),
            jnp.concatenate([v4_s[prv + r, walk, :], v4_s[cur + r, walk, :]], axis=0),
            pl.ds(t, blk, stride=d2), sb == 0)
        return carry

    lax.fori_loop(0, nblk, body, 0, unroll=ATTN_UNROLL)

    def combine(t, carry):
        rows = pl.ds(pl.multiple_of(t * blk, blk), blk)
        m0, m1, m2 = m_s[0, rows, :], m_s[1, rows, :], m_s[2, rows, :]
        mx = jnp.maximum(jnp.maximum(m0, m1), m2)
        w0, w1, w2 = jnp.exp2(m0 - mx), jnp.exp2(m1 - mx), jnp.exp2(m2 - mx)
        num = w0 * o_s[0, rows, :] + w1 * o_s[1, rows, :] + w2 * o_s[2, rows, :]
        den = w0 * l_s[0, rows, :] + w1 * l_s[1, rows, :] + w2 * l_s[2, rows, :]
        y_ref[rows, :] = (num / den).astype(y_ref.dtype)
        return carry

    lax.fori_loop(0, nblk, combine, 0, unroll=4)


def _dilated_attention(pf, col0, cast_w):
    s = pf.shape[0]
    sup = ATTN_SUPER
    hb = ATTN_HEADS
    d1 = DILATIONS[1]
    qb, kb, vb = (col0 // LANES + i * hb for i in range(3))
    blk = ATTN_BLOCK
    nblk = sup // blk
    tail = lambda b: jnp.maximum(b * nblk - 1, 0)
    grid = (hb, s // sup)
    cspec, cshape = _side_cast(cast_w, grid)
    return pl.pallas_call(
        _attn_kernel,
        grid=grid,
        in_specs=[pl.BlockSpec((sup, LANES), lambda h, b: (b, qb + h)),
                  pl.BlockSpec((blk, LANES), lambda h, b: (tail(b), kb + h)),
                  pl.BlockSpec((sup, LANES), lambda h, b: (b, kb + h)),
                  pl.BlockSpec((blk, LANES), lambda h, b: (tail(b), vb + h)),
                  pl.BlockSpec((sup, LANES), lambda h, b: (b, vb + h)),
                  cspec],
        out_specs=[pl.BlockSpec((sup, LANES), lambda h, b: (b, h)), cspec],
        out_shape=[jax.ShapeDtypeStruct((s, ATTN_WIDTH), BF16), cshape],
        scratch_shapes=[pltpu.VMEM((d1, sup // d1, LANES), F32),
                        pltpu.VMEM((2 * d1, sup // d1, LANES), F32),
                        pltpu.VMEM((2 * d1, sup // d1, LANES), F32),
                        pltpu.VMEM((len(DILATIONS), sup, LANES), F32),
                        pltpu.VMEM((len(DILATIONS), sup, LANES), F32),
                        pltpu.VMEM((len(DILATIONS), sup, LANES), F32),
                        pltpu.VMEM((2, blk, 2 * blk), F32)],
        compiler_params=_params("parallel", "arbitrary"),
        name="dilated_attn",
    )(pf, pf, pf, pf, pf, cast_w)


def _gate_weights(w_in_t, i_bias_l, f_bias_l):
    g0 = 2 * MLSTM_QK_WIDTH + 2 * MLSTM_WIDTH
    H = MLSTM_HEADS
    wg = jnp.stack([w_in_t[g0:g0 + H], w_in_t[g0 + H:g0 + 2 * H]], axis=1).reshape(2 * H, w_in_t.shape[1])
    wg = jnp.pad(wg, ((0, LANES - 2 * H), (0, 0)))
    bg = jnp.stack([i_bias_l, f_bias_l], axis=-1).reshape(1, 2 * H)
    bg = jnp.pad(bg, ((0, 0), (0, LANES - 2 * H)))
    return wg, bg.astype(F32)


def _layer(x, norm_mix, w_in, conv_w, conv_b, i_bias, f_bias, mlstm_norm, w_out, norm_mlp, w_up, w_down):
    qk_w = 2 * MLSTM_QK_WIDTH
    g0 = qk_w + 2 * MLSTM_WIDTH
    a0 = g0 + 2 * MLSTM_HEADS
    n_proj = w_in.shape[1] - 2 * MLSTM_HEADS
    assert a0 - g0 == SUBLANES
    w_in_t = w_in.T
    w_g, b_g = _gate_weights(w_in_t, i_bias, f_bias)

    h = _rmsnorm(x, norm_mix, BF16)
    proj, w_up_b = _in_proj(h, w_in_t, g0, n_proj, (g0, g0 + ATTN_WIDTH), ATTN_QSCALE, w_up, CAST_SLABS)
    gates = _gate_proj(h, w_g)
    y_a, w_out_b = _dilated_attention(proj, g0, w_out)
    y_m = _mlstm(proj, gates, b_g, conv_w, conv_b.reshape(1, -1), mlstm_norm.reshape(1, -1))
    x1, x1g, ssq = _out_proj(y_m, y_a, w_out_b, x, norm_mlp)
    u, w_down_b = _up_proj(x1g, ssq, w_up_b, w_down)
    return _matmul_ktiled_res(u, w_down_b, x1)


def kernel(x, norm_mix, w_in, conv_w, conv_b, i_bias, f_bias, mlstm_norm, w_out, norm_mlp, w_up, w_down, norm_final):
    b, s, d = x.shape
    outs = []
    for bi in range(b):
        xb = x[bi]
        for layer in range(norm_mix.shape[0]):
            xb = _layer(xb, norm_mix[layer], w_in[layer], conv_w[layer], conv_b[layer], i_bias[layer], f_bias[layer],
                        mlstm_norm[layer], w_out[layer], norm_mlp[layer], w_up[layer], w_down[layer])
        outs.append(_rmsnorm(xb, norm_final, x.dtype))
    return jnp.stack(outs, axis=0)
```

```python
import functools

import jax
import jax.numpy as jnp
from jax import lax
from jax.experimental import pallas as pl
from jax.experimental.pallas import tpu as pltpu

F32 = jnp.float32
BF16 = jnp.bfloat16

D_MODEL = 4096
MLSTM_HEADS = 4
MLSTM_DV = 512
MLSTM_DQK = 256
MLSTM_QK_WIDTH = MLSTM_HEADS * MLSTM_DQK
MLSTM_WIDTH = MLSTM_HEADS * MLSTM_DV
CONV_WIDTH = 4
GATE_SOFTCAP = 15.0
ATTN_HEAD_DIM = 128
ATTN_HEADS = 16
ATTN_WIDTH = ATTN_HEADS * ATTN_HEAD_DIM
ATTN_BLOCK = 128
DILATIONS = (1, 4, 16)
D_FF = 4 * D_MODEL
EPS = 1e-6

LANES = 128
SUBLANES = 8
VMEM_LIMIT_BYTES = 58 * 1024 * 1024

MM_TILE = 1024
MM_K_TILE = 4096
NORM_ROWS = 512
CAST_SLABS = 128
MLSTM_CHUNK = 256
ATTN_SUPER = DILATIONS[-1] * ATTN_BLOCK
ATTN_UNROLL = 16
ATTN_QSCALE = 1.4426950408889634 * ATTN_HEAD_DIM ** -0.5


def _params(*sem):
    return pltpu.CompilerParams(dimension_semantics=sem, vmem_limit_bytes=VMEM_LIMIT_BYTES)


def _rmsnorm_kernel(x_ref, g_ref, o_ref):
    x = x_ref[...]
    ms = jnp.mean(x * x, axis=-1, keepdims=True)
    o_ref[...] = (x * lax.rsqrt(ms + EPS) * g_ref[...]).astype(o_ref.dtype)


def _rmsnorm(x, g, out_dtype):
    s, d = x.shape
    return pl.pallas_call(
        _rmsnorm_kernel,
        grid=(s // NORM_ROWS,),
        in_specs=[pl.BlockSpec((NORM_ROWS, d), lambda i: (i, 0)),
                  pl.BlockSpec((1, d), lambda i: (0, 0))],
        out_specs=pl.BlockSpec((NORM_ROWS, d), lambda i: (i, 0)),
        out_shape=jax.ShapeDtypeStruct((s, d), out_dtype),
        compiler_params=_params("parallel"),
        name="rmsnorm",
    )(x, g.reshape(1, d))


def _side_cast(w, grid, slabs=None):
    steps = 1
    for g in grid:
        steps *= g
    slabs = steps if slabs is None else slabs
    rows = w.shape[0] // slabs
    assert slabs <= steps and rows * slabs == w.shape[0] and rows % (2 * SUBLANES) == 0

    def index_map(*ids):
        lin = ids[0]
        for g, i in zip(grid[1:], ids[1:]):
            lin = lin * g + i
        return (jnp.minimum(lin, slabs - 1), 0)

    spec = pl.BlockSpec((rows, w.shape[1]), index_map)
    return spec, jax.ShapeDtypeStruct(w.shape, BF16)


def _in_proj_kernel(a_ref, w_ref, nxt_ref, wsrc_ref, o_ref, wdst_ref, wb_s, *, first_shifted, tile_scales):
    wdst_ref[...] = wsrc_ref[...].astype(wdst_ref.dtype)
    j = pl.program_id(0)
    first_row_tile = pl.program_id(1) == 0

    @pl.when(first_row_tile & (j < first_shifted))
    def _():
        wb_s[...] = w_ref[...].astype(BF16)

    @pl.when(first_row_tile & (j >= first_shifted))
    def _():
        tn = w_ref.shape[0]
        body = tn - 2 * SUBLANES
        wb_s[0:body, :] = w_ref[SUBLANES:SUBLANES + body, :].astype(BF16)
        wb_s[body:tn, :] = jnp.concatenate([w_ref[tn - SUBLANES:tn, :], nxt_ref[...]], axis=0).astype(BF16)

    acc = lax.dot_general(a_ref[...], wb_s[...], (((1,), (1,)), ((), ())), preferred_element_type=F32)
    scale = 1.0
    for lo, hi, value in tile_scales:
        scale = jnp.where((j >= lo) & (j < hi), value, scale)
    o_ref[...] = acc * scale


def _in_proj(a, wt, gap_row, n_out, col_scales, cast_w, cast_slabs):
    m, k = a.shape
    tm, tn = MM_TILE, MM_TILE // 2
    assert gap_row % tn == 0 and n_out % tn == 0 and wt.shape[0] == n_out + SUBLANES
    assert all(lo % tn == 0 and hi % tn == 0 for lo, hi, _ in col_scales)
    per = tn // SUBLANES
    grid = (n_out // tn, m // tm)
    cspec, cshape = _side_cast(cast_w, grid, cast_slabs)
    return pl.pallas_call(
        functools.partial(_in_proj_kernel, first_shifted=gap_row // tn,
                          tile_scales=tuple((lo // tn, hi // tn, value) for lo, hi, value in col_scales)),
        grid=grid,
        in_specs=[pl.BlockSpec((tm, k), lambda j, i: (i, 0)),
                  pl.BlockSpec((tn, k), lambda j, i: (j, 0)),
                  pl.BlockSpec((SUBLANES, k), lambda j, i: ((j + 1) * per, 0)),
                  cspec],
        out_specs=[pl.BlockSpec((tm, tn), lambda j, i: (i, j)), cspec],
        out_shape=[jax.ShapeDtypeStruct((m, n_out), F32), cshape],
        scratch_shapes=[pltpu.VMEM((tn, k), BF16)],
        compiler_params=_params("arbitrary", "arbitrary"),
        name="in_proj",
    )(a, wt, wt, cast_w)


def _gate_proj_kernel(a_ref, wt_ref, o_ref):
    o_ref[...] = lax.dot_general(a_ref[...], wt_ref[...].astype(BF16), (((1,), (1,)), ((), ())),
                                 preferred_element_type=F32)


def _gate_proj(a, wt):
    m, k = a.shape
    n = wt.shape[0]
    tm = MM_TILE
    return pl.pallas_call(
        _gate_proj_kernel,
        grid=(m // tm,),
        in_specs=[pl.BlockSpec((tm, k), lambda i: (i, 0)),
                  pl.BlockSpec((n, k), lambda i: (0, 0))],
        out_specs=pl.BlockSpec((tm, n), lambda i: (i, 0)),
        out_shape=jax.ShapeDtypeStruct((m, n), F32),
        compiler_params=_params("parallel"),
        name="in_proj_gates",
    )(a, wt)


def _up_proj_kernel(a_ref, ssq_ref, b_ref, wsrc_ref, o_ref, wdst_ref, *, d_norm):
    wdst_ref[...] = wsrc_ref[...].astype(wdst_ref.dtype)
    acc = jnp.dot(a_ref[...], b_ref[...], preferred_element_type=F32)
    acc = acc * lax.rsqrt(ssq_ref[:, 0:1] * (1.0 / d_norm) + EPS)
    acc = jnp.maximum(acc, 0.0)
    o_ref[...] = (acc * acc).astype(o_ref.dtype)


def _up_proj(xg, ssq, w, cast_w):
    m, k = xg.shape
    n = w.shape[1]
    tm = tn = MM_TILE
    grid = (m // tm, n // tn)
    cspec, cshape = _side_cast(cast_w, grid)
    return pl.pallas_call(
        functools.partial(_up_proj_kernel, d_norm=k),
        grid=grid,
        in_specs=[pl.BlockSpec((tm, k), lambda i, j: (i, 0)),
                  pl.BlockSpec((tm, LANES), lambda i, j: (i, 0)),
                  pl.BlockSpec((k, tn), lambda i, j: (0, j)),
                  cspec],
        out_specs=[pl.BlockSpec((tm, tn), lambda i, j: (i, j)), cspec],
        out_shape=[jax.ShapeDtypeStruct((m, n), BF16), cshape],
        compiler_params=_params("parallel", "parallel"),
        name="up_proj",
    )(xg, ssq, w, cast_w)


def _out_proj_kernel(a0_ref, a1_ref, b0_ref, b1_ref, r_ref, g_ref, o_ref, xg_ref, ssq_ref):
    acc = jnp.dot(a0_ref[...], b0_ref[...], preferred_element_type=F32)
    acc = acc + jnp.dot(a1_ref[...], b1_ref[...], preferred_element_type=F32)
    x1 = r_ref[...] + acc
    o_ref[...] = x1
    xg_ref[...] = (x1 * g_ref[...]).astype(xg_ref.dtype)
    part = jnp.broadcast_to(jnp.sum(x1 * x1, axis=-1, keepdims=True), ssq_ref.shape)
    j = pl.program_id(1)

    @pl.when(j == 0)
    def _():
        ssq_ref[...] = part

    @pl.when(j > 0)
    def _():
        ssq_ref[...] += part


def _out_proj(y_m, y_a, w, res, gain):
    m, k0 = y_m.shape
    k1 = y_a.shape[1]
    assert k0 == k1
    n = w.shape[1]
    tm, tn = MM_TILE, MM_TILE // 2
    return pl.pallas_call(
        _out_proj_kernel,
        grid=(m // tm, n // tn),
        in_specs=[pl.BlockSpec((tm, k0), lambda i, j: (i, 0)),
                  pl.BlockSpec((tm, k1), lambda i, j: (i, 0)),
                  pl.BlockSpec((k0, tn), lambda i, j: (0, j)),
                  pl.BlockSpec((k1, tn), lambda i, j: (1, j)),
                  pl.BlockSpec((tm, tn), lambda i, j: (i, j)),
                  pl.BlockSpec((1, tn), lambda i, j: (0, j))],
        out_specs=[pl.BlockSpec((tm, tn), lambda i, j: (i, j)),
                   pl.BlockSpec((tm, tn), lambda i, j: (i, j)),
                   pl.BlockSpec((tm, LANES), lambda i, j: (i, 0))],
        out_shape=[jax.ShapeDtypeStruct((m, n), F32),
                   jax.ShapeDtypeStruct((m, n), BF16),
                   jax.ShapeDtypeStruct((m, LANES), F32)],
        compiler_params=_params("parallel", "arbitrary"),
        name="out_proj",
    )(y_m, y_a, w, w, res, gain.reshape(1, n))


def _mmk_res_kernel(a_ref, b_ref, r_ref, o_ref):
    @pl.when(pl.program_id(2) == 0)
    def _():
        o_ref[...] = r_ref[...]

    o_ref[...] += jnp.dot(a_ref[...], b_ref[...], preferred_element_type=F32)


def _matmul_ktiled_res(a, b, res):
    m, k = a.shape
    n = b.shape[1]
    tm = tn = MM_TILE
    tk = MM_K_TILE
    return pl.pallas_call(
        _mmk_res_kernel,
        grid=(m // tm, n // tn, k // tk),
        in_specs=[pl.BlockSpec((tm, tk), lambda i, j, kk: (i, kk)),
                  pl.BlockSpec((tk, tn), lambda i, j, kk: (kk, j)),
                  pl.BlockSpec((tm, tn), lambda i, j, kk: (i, j))],
        out_specs=pl.BlockSpec((tm, tn), lambda i, j, kk: (i, j)),
        out_shape=jax.ShapeDtypeStruct((m, n), F32),
        compiler_params=_params("parallel", "parallel", "arbitrary"),
        name="down_proj",
    )(a, b, res)


def _split3(a):
    hi = a.astype(BF16)
    r1 = a - hi.astype(F32)
    mid = r1.astype(BF16)
    lo = (r1 - mid.astype(F32)).astype(BF16)
    return hi, mid, lo


def _conv_silu(u_ref, tail_ref, ext_ref, w_ref, b_ref):
    chunk = u_ref.shape[0]
    u = u_ref[...]
    ext_ref[0:SUBLANES, :] = tail_ref[...]
    ext_ref[SUBLANES:SUBLANES + chunk, :] = u
    tail_ref[...] = u[chunk - SUBLANES:, :]
    w = 0.5 * w_ref[...]
    y = 0.5 * b_ref[...] + w[CONV_WIDTH - 1:CONV_WIDTH, :] * u
    for j in range(CONV_WIDTH - 1):
        start = SUBLANES - (CONV_WIDTH - 1) + j
        y = y + w[j:j + 1, :] * ext_ref[start:start + chunk, :]
    return y + y * jnp.tanh(y)


def _mlstm_kernel(qk_ref, v_ref, o_ref, g_ref, gb_ref, cw_ref, cb_ref, nrm_ref, y_ref, c_s, n_s, tail_s, ext_s):
    chunk = qk_ref.shape[0]
    dk, dv = MLSTM_DQK, MLSTM_DV

    @pl.when(pl.program_id(0) == 0)
    def _():
        c_s[...] = jnp.zeros_like(c_s)
        n_s[...] = jnp.zeros_like(n_s)
        tail_s[...] = jnp.zeros_like(tail_s)

    qk = _conv_silu(qk_ref, tail_s, ext_s, cw_ref, cb_ref)

    pre = GATE_SOFTCAP * jnp.tanh((g_ref[...] + gb_ref[...]) * (1.0 / GATE_SOFTCAP))
    logsig = jnp.minimum(pre, 0.0) - jnp.log(1.0 + jnp.exp(-jnp.abs(pre)))
    row = lax.broadcasted_iota(jnp.int32, (chunk, chunk), 0)
    col = lax.broadcasted_iota(jnp.int32, (chunk, chunk), 1)
    causal = row >= col
    tri = causal.astype(BF16)
    cum = sum(jnp.dot(tri, piece, preferred_element_type=F32) for piece in _split3(logsig))
    pre_t = pre.T
    cum_t = cum.T

    for hd in range(MLSTM_HEADS):
        g0 = 2 * hd
        q = qk[:, hd * dk:(hd + 1) * dk] * (dk ** -0.5)
        k = qk[:, (MLSTM_HEADS + hd) * dk:(MLSTM_HEADS + hd + 1) * dk]
        qb = q.astype(BF16)
        v = v_ref[:, hd * dv:(hd + 1) * dv].astype(BF16)
        li_col = pre[:, g0:g0 + 1]
        b_col = cum[:, g0 + 1:g0 + 2]
        b_last = cum[chunk - 1:chunk, g0 + 1:g0 + 2]
        r_col = li_col - b_col
        r_row = pre_t[g0:g0 + 1, :] - cum_t[g0 + 1:g0 + 2, :]

        dmat = jnp.where(causal, b_col + r_row, -jnp.inf)
        s_qk = lax.dot_general(qb, k.astype(BF16), (((1,), (1,)), ((), ())), preferred_element_type=F32)
        p = s_qk * jnp.exp(dmat)
        inter_w = jnp.exp(b_col)
        num = jnp.dot(p.astype(BF16), v, preferred_element_type=F32)
        num = num + inter_w * jnp.dot(qb, c_s[hd].astype(BF16), preferred_element_type=F32)
        den = jnp.sum(p, axis=-1, keepdims=True) + inter_w * jnp.sum(q * n_s[hd], axis=-1, keepdims=True)
        h = num / jnp.maximum(jnp.abs(den), 1.0)

        kw = k * jnp.exp(b_last + r_col)
        decay = jnp.exp(b_last)
        c_s[hd] = decay * c_s[hd] + lax.dot_general(kw.astype(BF16), v, (((0,), (0,)), ((), ())),
                                                    preferred_element_type=F32)
        n_s[hd] = decay * n_s[hd] + jnp.sum(kw, axis=0, keepdims=True)

        hn = h * lax.rsqrt(jnp.mean(h * h, axis=-1, keepdims=True) + EPS) * (0.5 * nrm_ref[:, hd * dv:(hd + 1) * dv])
        y_ref[:, hd * dv:(hd + 1) * dv] = (hn + hn * jnp.tanh(o_ref[:, hd * dv:(hd + 1) * dv])).astype(y_ref.dtype)


def _mlstm(proj, gates, gate_bias, conv_w, conv_b, mlstm_norm):
    s = proj.shape[0]
    L = MLSTM_CHUNK
    H = MLSTM_HEADS
    dk, dv = MLSTM_DQK, MLSTM_DV
    qkw = 2 * MLSTM_QK_WIDTH
    assert qkw == MLSTM_WIDTH
    return pl.pallas_call(
        _mlstm_kernel,
        grid=(s // L,),
        in_specs=[pl.BlockSpec((L, qkw), lambda c: (c, 0)),
                  pl.BlockSpec((L, MLSTM_WIDTH), lambda c: (c, 1)),
                  pl.BlockSpec((L, MLSTM_WIDTH), lambda c: (c, 2)),
                  pl.BlockSpec((L, LANES), lambda c: (c, 0)),
                  pl.BlockSpec((1, LANES), lambda c: (0, 0)),
                  pl.BlockSpec((CONV_WIDTH, qkw), lambda c: (0, 0)),
                  pl.BlockSpec((1, qkw), lambda c: (0, 0)),
                  pl.BlockSpec((1, MLSTM_WIDTH), lambda c: (0, 0))],
        out_specs=pl.BlockSpec((L, MLSTM_WIDTH), lambda c: (c, 0)),
        out_shape=jax.ShapeDtypeStruct((s, MLSTM_WIDTH), BF16),
        scratch_shapes=[pltpu.VMEM((H, dk, dv), F32),
                        pltpu.VMEM((H, 1, dk), F32),
                        pltpu.VMEM((SUBLANES, qkw), F32),
                        pltpu.VMEM((SUBLANES + L, qkw), F32)],
        compiler_params=_params("arbitrary"),
        name="mlstm",
    )(proj, proj, proj, gates, gate_bias, conv_w, conv_b, mlstm_norm)


def _band_softmax(q, k, v, bias):
    s = lax.dot_general(q, k, (((1,), (1,)), ((), ())), preferred_element_type=F32) + bias
    m = jnp.max(s, axis=-1, keepdims=True)
    p = jnp.exp2(s - m)
    l = jnp.sum(p, axis=-1, keepdims=True)
    acc = jnp.dot(p.astype(BF16), v, preferred_element_type=F32)
    return acc, m, l


def _attn_kernel(q_ref, kp_ref, kc_ref, vp_ref, vc_ref, wsrc_ref, y_ref, wdst_ref,
                 q4_s, k4_s, v4_s, o_s, m_s, l_s, bias_s):
    blk = ATTN_BLOCK
    sup = q_ref.shape[0]
    nblk = sup // blk
    d1, d2 = DILATIONS[1], DILATIONS[2]
    sub = sup // d1
    sb = pl.program_id(1)
    cur = (sb & 1) * d1
    prv = d1 - cur

    wdst_ref[...] = wsrc_ref[...].astype(wdst_ref.dtype)

    @pl.when(sb == 0)
    def _():
        for r in range(d1):
            k4_s[prv + r] = jnp.zeros((sub, LANES), F32)
            v4_s[prv + r] = jnp.zeros((sub, LANES), F32)

    for r in range(d1):
        q4_s[r] = q_ref[pl.ds(r, sub, stride=d1), :]
        k4_s[cur + r] = kc_ref[pl.ds(r, sub, stride=d1), :]
        v4_s[cur + r] = vc_ref[pl.ds(r, sub, stride=d1), :]

    qi = lax.broadcasted_iota(jnp.int32, (blk, 2 * blk), 0)
    kj = lax.broadcasted_iota(jnp.int32, (blk, 2 * blk), 1)
    band = (kj >= qi) & (kj <= qi + blk)
    bias_s[0] = jnp.where(band, 0.0, -jnp.inf).astype(F32)
    bias_s[1] = jnp.where(band & (kj >= blk), 0.0, -jnp.inf).astype(F32)

    def body(t, carry):
        def run(p_idx, q, k, v, out_rows, first):
            bias = bias_s[first.astype(jnp.int32)]
            acc, m, l = _band_softmax(q.astype(BF16), k.astype(BF16), v.astype(BF16), bias)
            o_s[p_idx, out_rows, :] = acc
            m_s[p_idx, out_rows, :] = jnp.broadcast_to(m, (blk, LANES))
            l_s[p_idx, out_rows, :] = jnp.broadcast_to(l, (blk, LANES))

        q0 = pl.multiple_of(t * blk, blk)
        qp = pl.multiple_of(jnp.maximum(t - 1, 0) * blk, blk)
        at_start = t == 0
        k_prev = jnp.where(at_start, kp_ref[...], kc_ref[pl.ds(qp, blk), :])
        v_prev = jnp.where(at_start, vp_ref[...], vc_ref[pl.ds(qp, blk), :])
        run(0, q_ref[pl.ds(q0, blk), :],
            jnp.concatenate([k_prev, kc_ref[pl.ds(q0, blk), :]], axis=0),
            jnp.concatenate([v_prev, vc_ref[pl.ds(q0, blk), :]], axis=0),
            pl.ds(q0, blk), (sb == 0) & at_start)

        sg = lax.shift_right_logical(t, d1.bit_length() - 1)
        r = t & (d1 - 1)
        s0 = pl.multiple_of(sg * blk, blk)
        seg0 = sg == 0
        before = (jnp.where(seg0, prv, cur) + r,
                  pl.ds(pl.multiple_of(jnp.where(seg0, sub - blk, s0 - blk), blk), blk))
        own = (cur + r, pl.ds(s0, blk))
        run(1, q4_s[r, pl.ds(s0, blk), :],
            jnp.concatenate([k4_s[before[0], before[1], :], k4_s[own[0], own[1], :]], axis=0),
            jnp.concatenate([v4_s[before[0], before[1], :], v4_s[own[0], own[1], :]], axis=0),
            pl.ds(sg * (d1 * blk) + r, blk, stride=d1), (sb == 0) & seg0)

        walk = pl.ds(sg, blk, stride=d2 // d1)
        run(2, q4_s[r, walk, :],
            jnp.concatenate([k4_s[prv + r, walk, :], k4_s[cur + r, walk, :]], axis=0),
            jnp.concatenate([v4_s[prv + r, walk, :], v4_s[cur + r, walk, :]], axis=0),
            pl.ds(t, blk, stride=d2), sb == 0)
        return carry

    lax.fori_loop(0, nblk, body, 0, unroll=ATTN_UNROLL)

    def combine(t, carry):
        rows = pl.ds(pl.multiple_of(t * blk, blk), blk)
        m0, m1, m2 = m_s[0, rows, :], m_s[1, rows, :], m_s[2, rows, :]
        mx = jnp.maximum(jnp.maximum(m0, m1), m2)
        w0, w1, w2 = jnp.exp2(m0 - mx), jnp.exp2(m1 - mx), jnp.exp2(m2 - mx)
        num = w0 * o_s[0, rows, :] + w1 * o_s[1, rows, :] + w2 * o_s[2, rows, :]
        den = w0 * l_s[0, rows, :] + w1 * l_s[1, rows, :] + w2 * l_s[2, rows, :]
        y_ref[rows, :] = (num / den).astype(y_ref.dtype)
        return carry

    lax.fori_loop(0, nblk, combine, 0, unroll=4)


def _dilated_attention(pf, col0, cast_w):
    s = pf.shape[0]
    sup = ATTN_SUPER
    hb = ATTN_HEADS
    d1 = DILATIONS[1]
    qb, kb, vb = (col0 // LANES + i * hb for i in range(3))
    blk = ATTN_BLOCK
    nblk = sup // blk
    tail = lambda b: jnp.maximum(b * nblk - 1, 0)
    grid = (hb, s // sup)
    cspec, cshape = _side_cast(cast_w, grid)
    return pl.pallas_call(
        _attn_kernel,
        grid=grid,
        in_specs=[pl.BlockSpec((sup, LANES), lambda h, b: (b, qb + h)),
                  pl.BlockSpec((blk, LANES), lambda h, b: (tail(b), kb + h)),
                  pl.BlockSpec((sup, LANES), lambda h, b: (b, kb + h)),
                  pl.BlockSpec((blk, LANES), lambda h, b: (tail(b), vb + h)),
                  pl.BlockSpec((sup, LANES), lambda h, b: (b, vb + h)),
                  cspec],
        out_specs=[pl.BlockSpec((sup, LANES), lambda h, b: (b, h)), cspec],
        out_shape=[jax.ShapeDtypeStruct((s, ATTN_WIDTH), BF16), cshape],
        scratch_shapes=[pltpu.VMEM((d1, sup // d1, LANES), F32),
                        pltpu.VMEM((2 * d1, sup // d1, LANES), F32),
                        pltpu.VMEM((2 * d1, sup // d1, LANES), F32),
                        pltpu.VMEM((len(DILATIONS), sup, LANES), F32),
                        pltpu.VMEM((len(DILATIONS), sup, LANES), F32),
                        pltpu.VMEM((len(DILATIONS), sup, LANES), F32),
                        pltpu.VMEM((2, blk, 2 * blk), F32)],
        compiler_params=_params("parallel", "arbitrary"),
        name="dilated_attn",
    )(pf, pf, pf, pf, pf, cast_w)


def _gate_weights(w_in_t, i_bias_l, f_bias_l):
    g0 = 2 * MLSTM_QK_WIDTH + 2 * MLSTM_WIDTH
    H = MLSTM_HEADS
    wg = jnp.stack([w_in_t[g0:g0 + H], w_in_t[g0 + H:g0 + 2 * H]], axis=1).reshape(2 * H, w_in_t.shape[1])
    wg = jnp.pad(wg, ((0, LANES - 2 * H), (0, 0)))
    bg = jnp.stack([i_bias_l, f_bias_l], axis=-1).reshape(1, 2 * H)
    bg = jnp.pad(bg, ((0, 0), (0, LANES - 2 * H)))
    return wg, bg.astype(F32)


def _layer(x, norm_mix, w_in, conv_w, conv_b, i_bias, f_bias, mlstm_norm, w_out, norm_mlp, w_up, w_down):
    qk_w = 2 * MLSTM_QK_WIDTH
    g0 = qk_w + 2 * MLSTM_WIDTH
    a0 = g0 + 2 * MLSTM_HEADS
    n_proj = w_in.shape[1] - 2 * MLSTM_HEADS
    assert a0 - g0 == SUBLANES
    w_in_t = w_in.T
    w_g, b_g = _gate_weights(w_in_t, i_bias, f_bias)

    h = _rmsnorm(x, norm_mix, BF16)
    o0 = qk_w + MLSTM_WIDTH
    col_scales = ((o0, g0, 0.5),
                  (g0, g0 + ATTN_WIDTH, ATTN_QSCALE))
    proj, w_up_b = _in_proj(h, w_in_t, g0, n_proj, col_scales, w_up, CAST_SLABS)
    gates = _gate_proj(h, w_g)
    y_a, w_out_b = _dilated_attention(proj, g0, w_out)
    y_m = _mlstm(proj, gates, b_g, conv_w, conv_b.reshape(1, -1), mlstm_norm.reshape(1, -1))
    x1, x1g, ssq = _out_proj(y_m, y_a, w_out_b, x, norm_mlp)
    u, w_down_b = _up_proj(x1g, ssq, w_up_b, w_down)
    return _matmul_ktiled_res(u, w_down_b, x1)


def kernel(x, norm_mix, w_in, conv_w, conv_b, i_bias, f_bias, mlstm_norm, w_out, norm_mlp, w_up, w_down, norm_final):
    b, s, d = x.shape
    outs = []
    for bi in range(b):
        xb = x[bi]
        for layer in range(norm_mix.shape[0]):
            xb = _layer(xb, norm_mix[layer], w_in[layer], conv_w[layer], conv_b[layer], i_bias[layer], f_bias[layer],
                        mlstm_norm[layer], w_out[layer], norm_mlp[layer], w_up[layer], w_down[layer])
        outs.append(_rmsnorm(xb, norm_final, x.dtype))
    return jnp.stack(outs, axis=0)
```

```python
import functools

import jax
import jax.numpy as jnp
from jax import lax
from jax.experimental import pallas as pl
from jax.experimental.pallas import tpu as pltpu

F32 = jnp.float32
BF16 = jnp.bfloat16

D_MODEL = 4096
MLSTM_HEADS = 4
MLSTM_DV = 512
MLSTM_DQK = 256
MLSTM_QK_WIDTH = MLSTM_HEADS * MLSTM_DQK
MLSTM_WIDTH = MLSTM_HEADS * MLSTM_DV
CONV_WIDTH = 4
GATE_SOFTCAP = 15.0
ATTN_HEAD_DIM = 128
ATTN_HEADS = 16
ATTN_WIDTH = ATTN_HEADS * ATTN_HEAD_DIM
ATTN_BLOCK = 128
DILATIONS = (1, 4, 16)
D_FF = 4 * D_MODEL
EPS = 1e-6

LANES = 128
SUBLANES = 8
VMEM_LIMIT_BYTES = 58 * 1024 * 1024

MM_TILE = 1024
MM_K_TILE = 4096
NORM_ROWS = 512
CAST_SLABS = 128
MLSTM_CHUNK = 256
ATTN_SUPER = DILATIONS[-1] * ATTN_BLOCK
ATTN_QSCALE = 1.4426950408889634 * ATTN_HEAD_DIM ** -0.5


def _params(*sem):
    return pltpu.CompilerParams(dimension_semantics=sem, vmem_limit_bytes=VMEM_LIMIT_BYTES)


def _rmsnorm_kernel(x_ref, g_ref, o_ref):
    x = x_ref[...]
    ms = jnp.mean(x * x, axis=-1, keepdims=True)
    o_ref[...] = (x * lax.rsqrt(ms + EPS) * g_ref[...]).astype(o_ref.dtype)


def _rmsnorm(x, g, out_dtype):
    s, d = x.shape
    return pl.pallas_call(
        _rmsnorm_kernel,
        grid=(s // NORM_ROWS,),
        in_specs=[pl.BlockSpec((NORM_ROWS, d), lambda i: (i, 0)),
                  pl.BlockSpec((1, d), lambda i: (0, 0))],
        out_specs=pl.BlockSpec((NORM_ROWS, d), lambda i: (i, 0)),
        out_shape=jax.ShapeDtypeStruct((s, d), out_dtype),
        compiler_params=_params("parallel"),
        name="rmsnorm",
    )(x, g.reshape(1, d))


def _side_cast(w, grid, slabs=None):
    steps = 1
    for g in grid:
        steps *= g
    slabs = steps if slabs is None else slabs
    rows = w.shape[0] // slabs
    assert slabs <= steps and rows * slabs == w.shape[0] and rows % (2 * SUBLANES) == 0

    def index_map(*ids):
        lin = ids[0]
        for g, i in zip(grid[1:], ids[1:]):
            lin = lin * g + i
        return (jnp.minimum(lin, slabs - 1), 0)

    spec = pl.BlockSpec((rows, w.shape[1]), index_map)
    return spec, jax.ShapeDtypeStruct(w.shape, BF16)


def _in_proj_kernel(a_ref, w_ref, nxt_ref, wsrc_ref, o_ref, wdst_ref, wb_s, *, first_shifted, tile_scales):
    wdst_ref[...] = wsrc_ref[...].astype(wdst_ref.dtype)
    j = pl.program_id(0)
    first_row_tile = pl.program_id(1) == 0

    @pl.when(first_row_tile & (j < first_shifted))
    def _():
        wb_s[...] = w_ref[...].astype(BF16)

    @pl.when(first_row_tile & (j >= first_shifted))
    def _():
        tn = w_ref.shape[0]
        body = tn - 2 * SUBLANES
        wb_s[0:body, :] = w_ref[SUBLANES:SUBLANES + body, :].astype(BF16)
        wb_s[body:tn, :] = jnp.concatenate([w_ref[tn - SUBLANES:tn, :], nxt_ref[...]], axis=0).astype(BF16)

    acc = lax.dot_general(a_ref[...], wb_s[...], (((1,), (1,)), ((), ())), preferred_element_type=F32)
    scale = 1.0
    for lo, hi, value in tile_scales:
        scale = jnp.where((j >= lo) & (j < hi), value, scale)
    o_ref[...] = acc * scale


def _in_proj(a, wt, gap_row, n_out, col_scales, cast_w, cast_slabs):
    m, k = a.shape
    tm, tn = MM_TILE, MM_TILE // 2
    assert gap_row % tn == 0 and n_out % tn == 0 and wt.shape[0] == n_out + SUBLANES
    assert all(lo % tn == 0 and hi % tn == 0 for lo, hi, _ in col_scales)
    per = tn // SUBLANES
    grid = (n_out // tn, m // tm)
    cspec, cshape = _side_cast(cast_w, grid, cast_slabs)
    return pl.pallas_call(
        functools.partial(_in_proj_kernel, first_shifted=gap_row // tn,
                          tile_scales=tuple((lo // tn, hi // tn, value) for lo, hi, value in col_scales)),
        grid=grid,
        in_specs=[pl.BlockSpec((tm, k), lambda j, i: (i, 0)),
                  pl.BlockSpec((tn, k), lambda j, i: (j, 0)),
                  pl.BlockSpec((SUBLANES, k), lambda j, i: ((j + 1) * per, 0)),
                  cspec],
        out_specs=[pl.BlockSpec((tm, tn), lambda j, i: (i, j)), cspec],
        out_shape=[jax.ShapeDtypeStruct((m, n_out), F32), cshape],
        scratch_shapes=[pltpu.VMEM((tn, k), BF16)],
        compiler_params=_params("arbitrary", "arbitrary"),
        name="in_proj",
    )(a, wt, wt, cast_w)


def _gate_proj_kernel(a_ref, wt_ref, o_ref):
    o_ref[...] = lax.dot_general(a_ref[...], wt_ref[...].astype(BF16), (((1,), (1,)), ((), ())),
                                 preferred_element_type=F32)


def _gate_proj(a, wt):
    m, k = a.shape
    n = wt.shape[0]
    tm = MM_TILE
    return pl.pallas_call(
        _gate_proj_kernel,
        grid=(m // tm,),
        in_specs=[pl.BlockSpec((tm, k), lambda i: (i, 0)),
                  pl.BlockSpec((n, k), lambda i: (0, 0))],
        out_specs=pl.BlockSpec((tm, n), lambda i: (i, 0)),
        out_shape=jax.ShapeDtypeStruct((m, n), F32),
        compiler_params=_params("parallel"),
        name="in_proj_gates",
    )(a, wt)


def _up_proj_kernel(a_ref, ssq_ref, b_ref, wsrc_ref, o_ref, wdst_ref, *, d_norm):
    wdst_ref[...] = wsrc_ref[...].astype(wdst_ref.dtype)
    acc = jnp.dot(a_ref[...], b_ref[...], preferred_element_type=F32)
    acc = acc * lax.rsqrt(ssq_ref[:, 0:1] * (1.0 / d_norm) + EPS)
    acc = jnp.maximum(acc, 0.0)
    o_ref[...] = (acc * acc).astype(o_ref.dtype)


def _up_proj(xg, ssq, w, cast_w):
    m, k = xg.shape
    n = w.shape[1]
    tm = tn = MM_TILE
    grid = (m // tm, n // tn)
    cspec, cshape = _side_cast(cast_w, grid)
    return pl.pallas_call(
        functools.partial(_up_proj_kernel, d_norm=k),
        grid=grid,
        in_specs=[pl.BlockSpec((tm, k), lambda i, j: (i, 0)),
                  pl.BlockSpec((tm, LANES), lambda i, j: (i, 0)),
                  pl.BlockSpec((k, tn), lambda i, j: (0, j)),
                  cspec],
        out_specs=[pl.BlockSpec((tm, tn), lambda i, j: (i, j)), cspec],
        out_shape=[jax.ShapeDtypeStruct((m, n), BF16), cshape],
        compiler_params=_params("parallel", "parallel"),
        name="up_proj",
    )(xg, ssq, w, cast_w)


def _out_proj_kernel(a0_ref, a1_ref, b0_ref, b1_ref, r_ref, g_ref, o_ref, xg_ref, ssq_ref):
    acc = jnp.dot(a0_ref[...], b0_ref[...], preferred_element_type=F32)
    acc = acc + jnp.dot(a1_ref[...], b1_ref[...], preferred_element_type=F32)
    x1 = r_ref[...] + acc
    o_ref[...] = x1
    xg_ref[...] = (x1 * g_ref[...]).astype(xg_ref.dtype)
    part = jnp.broadcast_to(jnp.sum(x1 * x1, axis=-1, keepdims=True), ssq_ref.shape)
    j = pl.program_id(1)

    @pl.when(j == 0)
    def _():
        ssq_ref[...] = part

    @pl.when(j > 0)
    def _():
        ssq_ref[...] += part


def _out_proj(y_m, y_a, w, res, gain):
    m, k0 = y_m.shape
    k1 = y_a.shape[1]
    assert k0 == k1
    n = w.shape[1]
    tm, tn = MM_TILE, MM_TILE // 2
    return pl.pallas_call(
        _out_proj_kernel,
        grid=(m // tm, n // tn),
        in_specs=[pl.BlockSpec((tm, k0), lambda i, j: (i, 0)),
                  pl.BlockSpec((tm, k1), lambda i, j: (i, 0)),
                  pl.BlockSpec((k0, tn), lambda i, j: (0, j)),
                  pl.BlockSpec((k1, tn), lambda i, j: (1, j)),
                  pl.BlockSpec((tm, tn), lambda i, j: (i, j)),
                  pl.BlockSpec((1, tn), lambda i, j: (0, j))],
        out_specs=[pl.BlockSpec((tm, tn), lambda i, j: (i, j)),
                   pl.BlockSpec((tm, tn), lambda i, j: (i, j)),
                   pl.BlockSpec((tm, LANES), lambda i, j: (i, 0))],
        out_shape=[jax.ShapeDtypeStruct((m, n), F32),
                   jax.ShapeDtypeStruct((m, n), BF16),
                   jax.ShapeDtypeStruct((m, LANES), F32)],
        compiler_params=_params("parallel", "arbitrary"),
        name="out_proj",
    )(y_m, y_a, w, w, res, gain.reshape(1, n))


def _mmk_res_kernel(a_ref, b_ref, r_ref, o_ref):
    @pl.when(pl.program_id(2) == 0)
    def _():
        o_ref[...] = r_ref[...]

    o_ref[...] += jnp.dot(a_ref[...], b_ref[...], preferred_element_type=F32)


def _matmul_ktiled_res(a, b, res):
    m, k = a.shape
    n = b.shape[1]
    tm = tn = MM_TILE
    tk = MM_K_TILE
    return pl.pallas_call(
        _mmk_res_kernel,
        grid=(m // tm, n // tn, k // tk),
        in_specs=[pl.BlockSpec((tm, tk), lambda i, j, kk: (i, kk)),
                  pl.BlockSpec((tk, tn), lambda i, j, kk: (kk, j)),
                  pl.BlockSpec((tm, tn), lambda i, j, kk: (i, j))],
        out_specs=pl.BlockSpec((tm, tn), lambda i, j, kk: (i, j)),
        out_shape=jax.ShapeDtypeStruct((m, n), F32),
        compiler_params=_params("parallel", "parallel", "arbitrary"),
        name="down_proj",
    )(a, b, res)


def _split3(a):
    hi = a.astype(BF16)
    r1 = a - hi.astype(F32)
    mid = r1.astype(BF16)
    lo = (r1 - mid.astype(F32)).astype(BF16)
    return hi, mid, lo


def _conv_silu(u_ref, tail_ref, ext_ref, w_ref, b_ref):
    chunk = u_ref.shape[0]
    u = u_ref[...]
    ext_ref[0:SUBLANES, :] = tail_ref[...]
    ext_ref[SUBLANES:SUBLANES + chunk, :] = u
    tail_ref[...] = u[chunk - SUBLANES:, :]
    w = 0.5 * w_ref[...]
    y = 0.5 * b_ref[...] + w[CONV_WIDTH - 1:CONV_WIDTH, :] * u
    for j in range(CONV_WIDTH - 1):
        start = SUBLANES - (CONV_WIDTH - 1) + j
        y = y + w[j:j + 1, :] * ext_ref[start:start + chunk, :]
    return y + y * jnp.tanh(y)


def _mlstm_kernel(qk_ref, v_ref, o_ref, g_ref, gb_ref, cw_ref, cb_ref, nrm_ref, y_ref, c_s, n_s, tail_s, ext_s):
    chunk = qk_ref.shape[0]
    dk, dv = MLSTM_DQK, MLSTM_DV

    @pl.when(pl.program_id(0) == 0)
    def _():
        c_s[...] = jnp.zeros_like(c_s)
        n_s[...] = jnp.zeros_like(n_s)
        tail_s[...] = jnp.zeros_like(tail_s)

    qk = _conv_silu(qk_ref, tail_s, ext_s, cw_ref, cb_ref)

    pre = GATE_SOFTCAP * jnp.tanh((g_ref[...] + gb_ref[...]) * (1.0 / GATE_SOFTCAP))
    logsig = jnp.minimum(pre, 0.0) - jnp.log(1.0 + jnp.exp(-jnp.abs(pre)))
    row = lax.broadcasted_iota(jnp.int32, (chunk, chunk), 0)
    col = lax.broadcasted_iota(jnp.int32, (chunk, chunk), 1)
    causal = row >= col
    tri = causal.astype(BF16)
    cum = sum(jnp.dot(tri, piece, preferred_element_type=F32) for piece in _split3(logsig))
    pre_t = pre.T
    cum_t = cum.T

    for hd in range(MLSTM_HEADS):
        g0 = 2 * hd
        q = qk[:, hd * dk:(hd + 1) * dk] * (dk ** -0.5)
        k = qk[:, (MLSTM_HEADS + hd) * dk:(MLSTM_HEADS + hd + 1) * dk]
        qb = q.astype(BF16)
        v = v_ref[:, hd * dv:(hd + 1) * dv].astype(BF16)
        li_col = pre[:, g0:g0 + 1]
        b_col = cum[:, g0 + 1:g0 + 2]
        b_last = cum[chunk - 1:chunk, g0 + 1:g0 + 2]
        r_col = li_col - b_col
        r_row = pre_t[g0:g0 + 1, :] - cum_t[g0 + 1:g0 + 2, :]

        dmat = jnp.where(causal, b_col + r_row, -jnp.inf)
        s_qk = lax.dot_general(qb, k.astype(BF16), (((1,), (1,)), ((), ())), preferred_element_type=F32)
        p = s_qk * jnp.exp(dmat)
        inter_w = jnp.exp(b_col)
        num = jnp.dot(p.astype(BF16), v, preferred_element_type=F32)
        num = num + inter_w * jnp.dot(qb, c_s[hd].astype(BF16), preferred_element_type=F32)
        den = jnp.sum(p, axis=-1, keepdims=True) + inter_w * jnp.sum(q * n_s[hd], axis=-1, keepdims=True)
        h = num / jnp.maximum(jnp.abs(den), 1.0)

        kw = k * jnp.exp(b_last + r_col)
        decay = jnp.exp(b_last)
        c_s[hd] = decay * c_s[hd] + lax.dot_general(kw.astype(BF16), v, (((0,), (0,)), ((), ())),
                                                    preferred_element_type=F32)
        n_s[hd] = decay * n_s[hd] + jnp.sum(kw, axis=0, keepdims=True)

        hn = h * lax.rsqrt(jnp.mean(h * h, axis=-1, keepdims=True) + EPS) * (0.5 * nrm_ref[:, hd * dv:(hd + 1) * dv])
        y_ref[:, hd * dv:(hd + 1) * dv] = (hn + hn * jnp.tanh(o_ref[:, hd * dv:(hd + 1) * dv])).astype(y_ref.dtype)


def _mlstm(proj, gates, gate_bias, conv_w, conv_b, mlstm_norm):
    s = proj.shape[0]
    L = MLSTM_CHUNK
    H = MLSTM_HEADS
    dk, dv = MLSTM_DQK, MLSTM_DV
    qkw = 2 * MLSTM_QK_WIDTH
    assert qkw == MLSTM_WIDTH
    return pl.pallas_call(
        _mlstm_kernel,
        grid=(s // L,),
        in_specs=[pl.BlockSpec((L, qkw), lambda c: (c, 0)),
                  pl.BlockSpec((L, MLSTM_WIDTH), lambda c: (c, 1)),
                  pl.BlockSpec((L, MLSTM_WIDTH), lambda c: (c, 2)),
                  pl.BlockSpec((L, LANES), lambda c: (c, 0)),
                  pl.BlockSpec((1, LANES), lambda c: (0, 0)),
                  pl.BlockSpec((CONV_WIDTH, qkw), lambda c: (0, 0)),
                  pl.BlockSpec((1, qkw), lambda c: (0, 0)),
                  pl.BlockSpec((1, MLSTM_WIDTH), lambda c: (0, 0))],
        out_specs=pl.BlockSpec((L, MLSTM_WIDTH), lambda c: (c, 0)),
        out_shape=jax.ShapeDtypeStruct((s, MLSTM_WIDTH), BF16),
        scratch_shapes=[pltpu.VMEM((H, dk, dv), F32),
                        pltpu.VMEM((H, 1, dk), F32),
                        pltpu.VMEM((SUBLANES, qkw), F32),
                        pltpu.VMEM((SUBLANES + L, qkw), F32)],
        compiler_params=_params("arbitrary"),
        name="mlstm",
    )(proj, proj, proj, gates, gate_bias, conv_w, conv_b, mlstm_norm)


def _band_softmax(q, k, v, bias):
    s = lax.dot_general(q, k, (((1,), (1,)), ((), ())), preferred_element_type=F32) + bias
    m = jnp.max(s, axis=-1, keepdims=True)
    p = jnp.exp2(s - m).astype(BF16)
    v1 = jnp.concatenate([v, jnp.ones_like(v)], axis=1)
    accl = jnp.dot(p, v1, preferred_element_type=F32)
    return accl[:, :v.shape[1]], m, accl[:, v.shape[1]:]


def _attn_kernel(q_ref, kp_ref, kc_ref, vp_ref, vc_ref, wsrc_ref, y_ref, wdst_ref,
                 q4_s, k4_s, v4_s, o_s, m_s, l_s, bias_s):
    blk = ATTN_BLOCK
    sup = q_ref.shape[0]
    nblk = sup // blk
    d1, d2 = DILATIONS[1], DILATIONS[2]
    sub = sup // d1
    sb = pl.program_id(1)
    cur = (sb & 1) * d1
    prv = d1 - cur

    wdst_ref[...] = wsrc_ref[...].astype(wdst_ref.dtype)

    @pl.when(sb == 0)
    def _():
        for r in range(d1):
            k4_s[prv + r] = jnp.zeros((sub, LANES), F32)
            v4_s[prv + r] = jnp.zeros((sub, LANES), F32)

    for r in range(d1):
        q4_s[r] = q_ref[pl.ds(r, sub, stride=d1), :]
        k4_s[cur + r] = kc_ref[pl.ds(r, sub, stride=d1), :]
        v4_s[cur + r] = vc_ref[pl.ds(r, sub, stride=d1), :]

    qi = lax.broadcasted_iota(jnp.int32, (blk, 2 * blk), 0)
    kj = lax.broadcasted_iota(jnp.int32, (blk, 2 * blk), 1)
    band = (kj >= qi) & (kj <= qi + blk)
    bias_s[0] = jnp.where(band, 0.0, -jnp.inf).astype(F32)
    bias_s[1] = jnp.where(band & (kj >= blk), 0.0, -jnp.inf).astype(F32)

    first_sb = (sb == 0).astype(jnp.int32)

    def run(p_idx, q, k, v, out_rows, first):
        bias = bias_s[first_sb if first else 0]
        acc, m, l = _band_softmax(q.astype(BF16), k.astype(BF16), v.astype(BF16), bias)
        o_s[p_idx, out_rows, :] = acc
        m_s[p_idx, out_rows, :] = jnp.broadcast_to(m, (blk, LANES))
        l_s[p_idx, out_rows, :] = jnp.broadcast_to(l, (blk, LANES))

    for t in range(nblk):
        q0 = t * blk
        k_prev = kp_ref[...] if t == 0 else kc_ref[q0 - blk:q0, :]
        v_prev = vp_ref[...] if t == 0 else vc_ref[q0 - blk:q0, :]
        run(0, q_ref[q0:q0 + blk, :],
            jnp.concatenate([k_prev, kc_ref[q0:q0 + blk, :]], axis=0),
            jnp.concatenate([v_prev, vc_ref[q0:q0 + blk, :]], axis=0),
            pl.ds(q0, blk), t == 0)

        sg, r = divmod(t, d1)
        s0 = sg * blk
        before = (prv + r, pl.ds(sub - blk, blk)) if sg == 0 else (cur + r, pl.ds(s0 - blk, blk))
        own = (cur + r, pl.ds(s0, blk))
        run(1, q4_s[r, s0:s0 + blk, :],
            jnp.concatenate([k4_s[before[0], before[1], :], k4_s[own[0], own[1], :]], axis=0),
            jnp.concatenate([v4_s[before[0], before[1], :], v4_s[own[0], own[1], :]], axis=0),
            pl.ds(sg * (d1 * blk) + r, blk, stride=d1), sg == 0)

        walk = pl.ds(sg, blk, stride=d2 // d1)
        run(2, q4_s[r, walk, :],
            jnp.concatenate([k4_s[prv + r, walk, :], k4_s[cur + r, walk, :]], axis=0),
            jnp.concatenate([v4_s[prv + r, walk, :], v4_s[cur + r, walk, :]], axis=0),
            pl.ds(t, blk, stride=d2), True)

    def combine(t, carry):
        rows = pl.ds(pl.multiple_of(t * blk, blk), blk)
        m0, m1, m2 = m_s[0, rows, :], m_s[1, rows, :], m_s[2, rows, :]
        mx = jnp.maximum(jnp.maximum(m0, m1), m2)
        w0, w1, w2 = jnp.exp2(m0 - mx), jnp.exp2(m1 - mx), jnp.exp2(m2 - mx)
        num = w0 * o_s[0, rows, :] + w1 * o_s[1, rows, :] + w2 * o_s[2, rows, :]
        den = w0 * l_s[0, rows, :] + w1 * l_s[1, rows, :] + w2 * l_s[2, rows, :]
        y_ref[rows, :] = (num / den).astype(y_ref.dtype)
        return carry

    lax.fori_loop(0, nblk, combine, 0, unroll=4)


def _dilated_attention(pf, col0, cast_w):
    s = pf.shape[0]
    sup = ATTN_SUPER
    hb = ATTN_HEADS
    d1 = DILATIONS[1]
    qb, kb, vb = (col0 // LANES + i * hb for i in range(3))
    blk = ATTN_BLOCK
    nblk = sup // blk
    tail = lambda b: jnp.maximum(b * nblk - 1, 0)
    grid = (hb, s // sup)
    cspec, cshape = _side_cast(cast_w, grid)
    return pl.pallas_call(
        _attn_kernel,
        grid=grid,
        in_specs=[pl.BlockSpec((sup, LANES), lambda h, b: (b, qb + h)),
                  pl.BlockSpec((blk, LANES), lambda h, b: (tail(b), kb + h)),
                  pl.BlockSpec((sup, LANES), lambda h, b: (b, kb + h)),
                  pl.BlockSpec((blk, LANES), lambda h, b: (tail(b), vb + h)),
                  pl.BlockSpec((sup, LANES), lambda h, b: (b, vb + h)),
                  cspec],
        out_specs=[pl.BlockSpec((sup, LANES), lambda h, b: (b, h)), cspec],
        out_shape=[jax.ShapeDtypeStruct((s, ATTN_WIDTH), BF16), cshape],
        scratch_shapes=[pltpu.VMEM((d1, sup // d1, LANES), F32),
                        pltpu.VMEM((2 * d1, sup // d1, LANES), F32),
                        pltpu.VMEM((2 * d1, sup // d1, LANES), F32),
                        pltpu.VMEM((len(DILATIONS), sup, LANES), F32),
                        pltpu.VMEM((len(DILATIONS), sup, LANES), F32),
                        pltpu.VMEM((len(DILATIONS), sup, LANES), F32),
                        pltpu.VMEM((2, blk, 2 * blk), F32)],
        compiler_params=_params("parallel", "arbitrary"),
        name="dilated_attn",
    )(pf, pf, pf, pf, pf, cast_w)


def _gate_weights(w_in_t, i_bias_l, f_bias_l):
    g0 = 2 * MLSTM_QK_WIDTH + 2 * MLSTM_WIDTH
    H = MLSTM_HEADS
    wg = jnp.stack([w_in_t[g0:g0 + H], w_in_t[g0 + H:g0 + 2 * H]], axis=1).reshape(2 * H, w_in_t.shape[1])
    wg = jnp.pad(wg, ((0, LANES - 2 * H), (0, 0)))
    bg = jnp.stack([i_bias_l, f_bias_l], axis=-1).reshape(1, 2 * H)
    bg = jnp.pad(bg, ((0, 0), (0, LANES - 2 * H)))
    return wg, bg.astype(F32)


def _layer(x, norm_mix, w_in, conv_w, conv_b, i_bias, f_bias, mlstm_norm, w_out, norm_mlp, w_up, w_down):
    qk_w = 2 * MLSTM_QK_WIDTH
    g0 = qk_w + 2 * MLSTM_WIDTH
    a0 = g0 + 2 * MLSTM_HEADS
    n_proj = w_in.shape[1] - 2 * MLSTM_HEADS
    assert a0 - g0 == SUBLANES
    w_in_t = w_in.T
    w_g, b_g = _gate_weights(w_in_t, i_bias, f_bias)

    h = _rmsnorm(x, norm_mix, BF16)
    o0 = qk_w + MLSTM_WIDTH
    col_scales = ((o0, g0, 0.5),
                  (g0, g0 + ATTN_WIDTH, ATTN_QSCALE))
    proj, w_up_b = _in_proj(h, w_in_t, g0, n_proj, col_scales, w_up, CAST_SLABS)
    gates = _gate_proj(h, w_g)
    y_a, w_out_b = _dilated_attention(proj, g0, w_out)
    y_m = _mlstm(proj, gates, b_g, conv_w, conv_b.reshape(1, -1), mlstm_norm.reshape(1, -1))
    x1, x1g, ssq = _out_proj(y_m, y_a, w_out_b, x, norm_mlp)
    u, w_down_b = _up_proj(x1g, ssq, w_up_b, w_down)
    return _matmul_ktiled_res(u, w_down_b, x1)


def kernel(x, norm_mix, w_in, conv_w, conv_b, i_bias, f_bias, mlstm_norm, w_out, norm_mlp, w_up, w_down, norm_final):
    b, s, d = x.shape
    outs = []
    for bi in range(b):
        xb = x[bi]
        for layer in range(norm_mix.shape[0]):
            xb = _layer(xb, norm_mix[layer], w_in[layer], conv_w[layer], conv_b[layer], i_bias[layer], f_bias[layer],
                        mlstm_norm[layer], w_out[layer], norm_mlp[layer], w_up[layer], w_down[layer])
        outs.append(_rmsnorm(xb, norm_final, x.dtype))
    return jnp.stack(outs, axis=0)
```

```python
import functools

import jax
import jax.numpy as jnp
from jax import lax
from jax.experimental import pallas as pl
from jax.experimental.pallas import tpu as pltpu

F32 = jnp.float32
BF16 = jnp.bfloat16

D_MODEL = 4096
MLSTM_HEADS = 4
MLSTM_DV = 512
MLSTM_DQK = 256
MLSTM_QK_WIDTH = MLSTM_HEADS * MLSTM_DQK
MLSTM_WIDTH = MLSTM_HEADS * MLSTM_DV
CONV_WIDTH = 4
GATE_SOFTCAP = 15.0
ATTN_HEAD_DIM = 128
ATTN_HEADS = 16
ATTN_WIDTH = ATTN_HEADS * ATTN_HEAD_DIM
ATTN_BLOCK = 128
DILATIONS = (1, 4, 16)
D_FF = 4 * D_MODEL
EPS = 1e-6

LANES = 128
SUBLANES = 8
VMEM_LIMIT_BYTES = 58 * 1024 * 1024

MM_TILE = 1024
MM_K_TILE = 4096
NORM_ROWS = 512
CAST_SLABS = 128
MLSTM_CHUNK = 256
ATTN_SUPER = DILATIONS[-1] * ATTN_BLOCK
ATTN_QSCALE = 1.4426950408889634 * ATTN_HEAD_DIM ** -0.5


def _params(*sem):
    return pltpu.CompilerParams(dimension_semantics=sem, vmem_limit_bytes=VMEM_LIMIT_BYTES)


def _rmsnorm_kernel(x_ref, g_ref, o_ref):
    x = x_ref[...]
    ms = jnp.mean(x * x, axis=-1, keepdims=True)
    o_ref[...] = (x * lax.rsqrt(ms + EPS) * g_ref[...]).astype(o_ref.dtype)


def _rmsnorm(x, g, out_dtype):
    s, d = x.shape
    return pl.pallas_call(
        _rmsnorm_kernel,
        grid=(s // NORM_ROWS,),
        in_specs=[pl.BlockSpec((NORM_ROWS, d), lambda i: (i, 0)),
                  pl.BlockSpec((1, d), lambda i: (0, 0))],
        out_specs=pl.BlockSpec((NORM_ROWS, d), lambda i: (i, 0)),
        out_shape=jax.ShapeDtypeStruct((s, d), out_dtype),
        compiler_params=_params("parallel"),
        name="rmsnorm",
    )(x, g.reshape(1, d))


def _side_cast(w, grid, slabs=None):
    steps = 1
    for g in grid:
        steps *= g
    slabs = steps if slabs is None else slabs
    rows = w.shape[0] // slabs
    assert slabs <= steps and rows * slabs == w.shape[0] and rows % (2 * SUBLANES) == 0

    def index_map(*ids):
        lin = ids[0]
        for g, i in zip(grid[1:], ids[1:]):
            lin = lin * g + i
        return (jnp.minimum(lin, slabs - 1), 0)

    spec = pl.BlockSpec((rows, w.shape[1]), index_map)
    return spec, jax.ShapeDtypeStruct(w.shape, BF16)


def _in_proj_kernel(a_ref, w_ref, nxt_ref, wsrc_ref, o_ref, wdst_ref, wb_s, *, first_shifted, tile_scales):
    wdst_ref[...] = wsrc_ref[...].astype(wdst_ref.dtype)
    j = pl.program_id(0)
    first_row_tile = pl.program_id(1) == 0

    @pl.when(first_row_tile & (j < first_shifted))
    def _():
        wb_s[...] = w_ref[...].astype(BF16)

    @pl.when(first_row_tile & (j >= first_shifted))
    def _():
        tn = w_ref.shape[0]
        body = tn - 2 * SUBLANES
        wb_s[0:body, :] = w_ref[SUBLANES:SUBLANES + body, :].astype(BF16)
        wb_s[body:tn, :] = jnp.concatenate([w_ref[tn - SUBLANES:tn, :], nxt_ref[...]], axis=0).astype(BF16)

    acc = lax.dot_general(a_ref[...], wb_s[...], (((1,), (1,)), ((), ())), preferred_element_type=F32)
    scale = 1.0
    for lo, hi, value in tile_scales:
        scale = jnp.where((j >= lo) & (j < hi), value, scale)
    o_ref[...] = acc * scale


def _in_proj(a, wt, gap_row, n_out, col_scales, cast_w, cast_slabs):
    m, k = a.shape
    tm, tn = MM_TILE, MM_TILE // 2
    assert gap_row % tn == 0 and n_out % tn == 0 and wt.shape[0] == n_out + SUBLANES
    assert all(lo % tn == 0 and hi % tn == 0 for lo, hi, _ in col_scales)
    per = tn // SUBLANES
    grid = (n_out // tn, m // tm)
    cspec, cshape = _side_cast(cast_w, grid, cast_slabs)
    return pl.pallas_call(
        functools.partial(_in_proj_kernel, first_shifted=gap_row // tn,
                          tile_scales=tuple((lo // tn, hi // tn, value) for lo, hi, value in col_scales)),
        grid=grid,
        in_specs=[pl.BlockSpec((tm, k), lambda j, i: (i, 0)),
                  pl.BlockSpec((tn, k), lambda j, i: (j, 0)),
                  pl.BlockSpec((SUBLANES, k), lambda j, i: ((j + 1) * per, 0)),
                  cspec],
        out_specs=[pl.BlockSpec((tm, tn), lambda j, i: (i, j)), cspec],
        out_shape=[jax.ShapeDtypeStruct((m, n_out), F32), cshape],
        scratch_shapes=[pltpu.VMEM((tn, k), BF16)],
        compiler_params=_params("arbitrary", "arbitrary"),
        name="in_proj",
    )(a, wt, wt, cast_w)


def _gate_proj_kernel(a_ref, wt_ref, o_ref):
    o_ref[...] = lax.dot_general(a_ref[...], wt_ref[...].astype(BF16), (((1,), (1,)), ((), ())),
                                 preferred_element_type=F32)


def _gate_proj(a, wt):
    m, k = a.shape
    n = wt.shape[0]
    tm = MM_TILE
    return pl.pallas_call(
        _gate_proj_kernel,
        grid=(m // tm,),
        in_specs=[pl.BlockSpec((tm, k), lambda i: (i, 0)),
                  pl.BlockSpec((n, k), lambda i: (0, 0))],
        out_specs=pl.BlockSpec((tm, n), lambda i: (i, 0)),
        out_shape=jax.ShapeDtypeStruct((m, n), F32),
        compiler_params=_params("parallel"),
        name="in_proj_gates",
    )(a, wt)


def _up_proj_kernel(a_ref, ssq_ref, b_ref, wsrc_ref, o_ref, wdst_ref, *, d_norm):
    wdst_ref[...] = wsrc_ref[...].astype(wdst_ref.dtype)
    acc = jnp.dot(a_ref[...], b_ref[...], preferred_element_type=F32)
    acc = acc * lax.rsqrt(ssq_ref[:, 0:1] * (1.0 / d_norm) + EPS)
    acc = jnp.maximum(acc, 0.0)
    o_ref[...] = (acc * acc).astype(o_ref.dtype)


def _up_proj(xg, ssq, w, cast_w):
    m, k = xg.shape
    n = w.shape[1]
    tm = tn = MM_TILE
    grid = (m // tm, n // tn)
    cspec, cshape = _side_cast(cast_w, grid)
    return pl.pallas_call(
        functools.partial(_up_proj_kernel, d_norm=k),
        grid=grid,
        in_specs=[pl.BlockSpec((tm, k), lambda i, j: (i, 0)),
                  pl.BlockSpec((tm, LANES), lambda i, j: (i, 0)),
                  pl.BlockSpec((k, tn), lambda i, j: (0, j)),
                  cspec],
        out_specs=[pl.BlockSpec((tm, tn), lambda i, j: (i, j)), cspec],
        out_shape=[jax.ShapeDtypeStruct((m, n), BF16), cshape],
        compiler_params=_params("parallel", "parallel"),
        name="up_proj",
    )(xg, ssq, w, cast_w)


def _out_proj_kernel(a0_ref, a1_ref, b0_ref, b1_ref, r_ref, g_ref, o_ref, xg_ref, ssq_ref):
    acc = jnp.dot(a0_ref[...], b0_ref[...], preferred_element_type=F32)
    acc = acc + jnp.dot(a1_ref[...], b1_ref[...], preferred_element_type=F32)
    x1 = r_ref[...] + acc
    o_ref[...] = x1
    xg_ref[...] = (x1 * g_ref[...]).astype(xg_ref.dtype)
    part = jnp.broadcast_to(jnp.sum(x1 * x1, axis=-1, keepdims=True), ssq_ref.shape)
    j = pl.program_id(1)

    @pl.when(j == 0)
    def _():
        ssq_ref[...] = part

    @pl.when(j > 0)
    def _():
        ssq_ref[...] += part


def _out_proj(y_m, y_a, w, res, gain):
    m, k0 = y_m.shape
    k1 = y_a.shape[1]
    assert k0 == k1
    n = w.shape[1]
    tm, tn = MM_TILE, MM_TILE // 2
    return pl.pallas_call(
        _out_proj_kernel,
        grid=(m // tm, n // tn),
        in_specs=[pl.BlockSpec((tm, k0), lambda i, j: (i, 0)),
                  pl.BlockSpec((tm, k1), lambda i, j: (i, 0)),
                  pl.BlockSpec((k0, tn), lambda i, j: (0, j)),
                  pl.BlockSpec((k1, tn), lambda i, j: (1, j)),
                  pl.BlockSpec((tm, tn), lambda i, j: (i, j)),
                  pl.BlockSpec((1, tn), lambda i, j: (0, j))],
        out_specs=[pl.BlockSpec((tm, tn), lambda i, j: (i, j)),
                   pl.BlockSpec((tm, tn), lambda i, j: (i, j)),
                   pl.BlockSpec((tm, LANES), lambda i, j: (i, 0))],
        out_shape=[jax.ShapeDtypeStruct((m, n), F32),
                   jax.ShapeDtypeStruct((m, n), BF16),
                   jax.ShapeDtypeStruct((m, LANES), F32)],
        compiler_params=_params("parallel", "arbitrary"),
        name="out_proj",
    )(y_m, y_a, w, w, res, gain.reshape(1, n))


def _mmk_res_kernel(a_ref, b_ref, r_ref, o_ref):
    @pl.when(pl.program_id(2) == 0)
    def _():
        o_ref[...] = r_ref[...]

    o_ref[...] += jnp.dot(a_ref[...], b_ref[...], preferred_element_type=F32)


def _matmul_ktiled_res(a, b, res):
    m, k = a.shape
    n = b.shape[1]
    tm = tn = MM_TILE
    tk = MM_K_TILE
    return pl.pallas_call(
        _mmk_res_kernel,
        grid=(m // tm, n // tn, k // tk),
        in_specs=[pl.BlockSpec((tm, tk), lambda i, j, kk: (i, kk)),
                  pl.BlockSpec((tk, tn), lambda i, j, kk: (kk, j)),
                  pl.BlockSpec((tm, tn), lambda i, j, kk: (i, j))],
        out_specs=pl.BlockSpec((tm, tn), lambda i, j, kk: (i, j)),
        out_shape=jax.ShapeDtypeStruct((m, n), F32),
        compiler_params=_params("parallel", "parallel", "arbitrary"),
        name="down_proj",
    )(a, b, res)


def _split3(a):
    hi = a.astype(BF16)
    r1 = a - hi.astype(F32)
    mid = r1.astype(BF16)
    lo = (r1 - mid.astype(F32)).astype(BF16)
    return hi, mid, lo


def _conv_silu(u_ref, tail_ref, ext_ref, w_ref, b_ref):
    chunk = u_ref.shape[0]
    u = u_ref[...]
    ext_ref[0:SUBLANES, :] = tail_ref[...]
    ext_ref[SUBLANES:SUBLANES + chunk, :] = u
    tail_ref[...] = u[chunk - SUBLANES:, :]
    w = 0.5 * w_ref[...]
    y = 0.5 * b_ref[...] + w[CONV_WIDTH - 1:CONV_WIDTH, :] * u
    for j in range(CONV_WIDTH - 1):
        start = SUBLANES - (CONV_WIDTH - 1) + j
        y = y + w[j:j + 1, :] * ext_ref[start:start + chunk, :]
    return y + y * jnp.tanh(y)


def _mlstm_kernel(qk_ref, v_ref, o_ref, g_ref, gb_ref, cw_ref, cb_ref, nrm_ref, y_ref, c_s, n_s, tail_s, ext_s):
    chunk = qk_ref.shape[0]
    dk, dv = MLSTM_DQK, MLSTM_DV

    @pl.when(pl.program_id(0) == 0)
    def _():
        c_s[...] = jnp.zeros_like(c_s)
        n_s[...] = jnp.zeros_like(n_s)
        tail_s[...] = jnp.zeros_like(tail_s)

    qk = _conv_silu(qk_ref, tail_s, ext_s, cw_ref, cb_ref)

    pre = GATE_SOFTCAP * jnp.tanh((g_ref[...] + gb_ref[...]) * (1.0 / GATE_SOFTCAP))
    logsig = jnp.minimum(pre, 0.0) - jnp.log(1.0 + jnp.exp(-jnp.abs(pre)))
    row = lax.broadcasted_iota(jnp.int32, (chunk, chunk), 0)
    col = lax.broadcasted_iota(jnp.int32, (chunk, chunk), 1)
    causal = row >= col
    tri = causal.astype(BF16)
    cum = sum(jnp.dot(tri, piece, preferred_element_type=F32) for piece in _split3(logsig))
    pre_t = pre.T
    cum_t = cum.T

    for hd in range(MLSTM_HEADS):
        g0 = 2 * hd
        q = qk[:, hd * dk:(hd + 1) * dk] * (dk ** -0.5)
        k = qk[:, (MLSTM_HEADS + hd) * dk:(MLSTM_HEADS + hd + 1) * dk]
        qb = q.astype(BF16)
        v = v_ref[:, hd * dv:(hd + 1) * dv].astype(BF16)
        li_col = pre[:, g0:g0 + 1]
        b_col = cum[:, g0 + 1:g0 + 2]
        b_last = cum[chunk - 1:chunk, g0 + 1:g0 + 2]
        r_col = li_col - b_col
        r_row = pre_t[g0:g0 + 1, :] - cum_t[g0 + 1:g0 + 2, :]

        dmat = jnp.where(causal, b_col + r_row, -jnp.inf)
        s_qk = lax.dot_general(qb, k.astype(BF16), (((1,), (1,)), ((), ())), preferred_element_type=F32)
        p = s_qk * jnp.exp(dmat)
        inter_w = jnp.exp(b_col)
        pv = jnp.dot(p.astype(BF16), jnp.concatenate([v, jnp.ones((chunk, LANES), BF16)], axis=1),
                     preferred_element_type=F32)
        num = pv[:, :dv] + inter_w * jnp.dot(qb, c_s[hd].astype(BF16), preferred_element_type=F32)
        den = pv[:, dv:dv + 1] + inter_w * jnp.sum(q * n_s[hd], axis=-1, keepdims=True)
        h = num / jnp.maximum(jnp.abs(den), 1.0)

        kw = k * jnp.exp(b_last + r_col)
        decay = jnp.exp(b_last)
        c_s[hd] = decay * c_s[hd] + lax.dot_general(kw.astype(BF16), v, (((0,), (0,)), ((), ())),
                                                    preferred_element_type=F32)
        n_s[hd] = decay * n_s[hd] + jnp.sum(kw, axis=0, keepdims=True)

        hn = h * lax.rsqrt(jnp.mean(h * h, axis=-1, keepdims=True) + EPS) * (0.5 * nrm_ref[:, hd * dv:(hd + 1) * dv])
        y_ref[:, hd * dv:(hd + 1) * dv] = (hn + hn * jnp.tanh(o_ref[:, hd * dv:(hd + 1) * dv])).astype(y_ref.dtype)


def _mlstm(proj, gates, gate_bias, conv_w, conv_b, mlstm_norm):
    s = proj.shape[0]
    L = MLSTM_CHUNK
    H = MLSTM_HEADS
    dk, dv = MLSTM_DQK, MLSTM_DV
    qkw = 2 * MLSTM_QK_WIDTH
    assert qkw == MLSTM_WIDTH
    return pl.pallas_call(
        _mlstm_kernel,
        grid=(s // L,),
        in_specs=[pl.BlockSpec((L, qkw), lambda c: (c, 0)),
                  pl.BlockSpec((L, MLSTM_WIDTH), lambda c: (c, 1)),
                  pl.BlockSpec((L, MLSTM_WIDTH), lambda c: (c, 2)),
                  pl.BlockSpec((L, LANES), lambda c: (c, 0)),
                  pl.BlockSpec((1, LANES), lambda c: (0, 0)),
                  pl.BlockSpec((CONV_WIDTH, qkw), lambda c: (0, 0)),
                  pl.BlockSpec((1, qkw), lambda c: (0, 0)),
                  pl.BlockSpec((1, MLSTM_WIDTH), lambda c: (0, 0))],
        out_specs=pl.BlockSpec((L, MLSTM_WIDTH), lambda c: (c, 0)),
        out_shape=jax.ShapeDtypeStruct((s, MLSTM_WIDTH), BF16),
        scratch_shapes=[pltpu.VMEM((H, dk, dv), F32),
                        pltpu.VMEM((H, 1, dk), F32),
                        pltpu.VMEM((SUBLANES, qkw), F32),
                        pltpu.VMEM((SUBLANES + L, qkw), F32)],
        compiler_params=_params("arbitrary"),
        name="mlstm",
    )(proj, proj, proj, gates, gate_bias, conv_w, conv_b, mlstm_norm)


def _band_softmax(q, k, v, bias):
    s = lax.dot_general(q, k, (((1,), (1,)), ((), ())), preferred_element_type=F32) + bias
    m = jnp.max(s, axis=-1, keepdims=True)
    p = jnp.exp2(s - m).astype(BF16)
    v1 = jnp.concatenate([v, jnp.ones_like(v)], axis=1)
    accl = jnp.dot(p, v1, preferred_element_type=F32)
    return accl[:, :v.shape[1]], m, accl[:, v.shape[1]:]


def _attn_kernel(q_ref, kp_ref, kc_ref, vp_ref, vc_ref, wsrc_ref, y_ref, wdst_ref,
                 q4_s, k4_s, v4_s, o_s, m_s, l_s, bias_s):
    blk = ATTN_BLOCK
    sup = q_ref.shape[0]
    nblk = sup // blk
    d1, d2 = DILATIONS[1], DILATIONS[2]
    sub = sup // d1
    sb = pl.program_id(1)
    cur = (sb & 1) * d1
    prv = d1 - cur

    wdst_ref[...] = wsrc_ref[...].astype(wdst_ref.dtype)

    @pl.when(sb == 0)
    def _():
        for r in range(d1):
            k4_s[prv + r] = jnp.zeros((sub, LANES), F32)
            v4_s[prv + r] = jnp.zeros((sub, LANES), F32)

    for r in range(d1):
        q4_s[r] = q_ref[pl.ds(r, sub, stride=d1), :]
        k4_s[cur + r] = kc_ref[pl.ds(r, sub, stride=d1), :]
        v4_s[cur + r] = vc_ref[pl.ds(r, sub, stride=d1), :]

    qi = lax.broadcasted_iota(jnp.int32, (blk, 2 * blk), 0)
    kj = lax.broadcasted_iota(jnp.int32, (blk, 2 * blk), 1)
    band = (kj >= qi) & (kj <= qi + blk)
    bias_s[0] = jnp.where(band, 0.0, -jnp.inf).astype(F32)
    bias_s[1] = jnp.where(band & (kj >= blk), 0.0, -jnp.inf).astype(F32)

    first_sb = (sb == 0).astype(jnp.int32)

    def run(p_idx, q, k, v, out_rows, first):
        bias = bias_s[first_sb if first else 0]
        acc, m, l = _band_softmax(q.astype(BF16), k.astype(BF16), v.astype(BF16), bias)
        o_s[p_idx, out_rows, :] = acc
        m_s[p_idx, out_rows, :] = jnp.broadcast_to(m, (blk, LANES))
        l_s[p_idx, out_rows, :] = jnp.broadcast_to(l, (blk, LANES))

    for t in range(nblk):
        q0 = t * blk
        k_prev = kp_ref[...] if t == 0 else kc_ref[q0 - blk:q0, :]
        v_prev = vp_ref[...] if t == 0 else vc_ref[q0 - blk:q0, :]
        run(0, q_ref[q0:q0 + blk, :],
            jnp.concatenate([k_prev, kc_ref[q0:q0 + blk, :]], axis=0),
            jnp.concatenate([v_prev, vc_ref[q0:q0 + blk, :]], axis=0),
            pl.ds(q0, blk), t == 0)

        sg, r = divmod(t, d1)
        s0 = sg * blk
        before = (prv + r, pl.ds(sub - blk, blk)) if sg == 0 else (cur + r, pl.ds(s0 - blk, blk))
        own = (cur + r, pl.ds(s0, blk))
        run(1, q4_s[r, s0:s0 + blk, :],
            jnp.concatenate([k4_s[before[0], before[1], :], k4_s[own[0], own[1], :]], axis=0),
            jnp.concatenate([v4_s[before[0], before[1], :], v4_s[own[0], own[1], :]], axis=0),
            pl.ds(sg * (d1 * blk) + r, blk, stride=d1), sg == 0)

        walk = pl.ds(sg, blk, stride=d2 // d1)
        run(2, q4_s[r, walk, :],
            jnp.concatenate([k4_s[prv + r, walk, :], k4_s[cur + r, walk, :]], axis=0),
            jnp.concatenate([v4_s[prv + r, walk, :], v4_s[cur + r, walk, :]], axis=0),
            pl.ds(t, blk, stride=d2), True)

    def combine(t, carry):
        rows = pl.ds(pl.multiple_of(t * blk, blk), blk)
        m0, m1, m2 = m_s[0, rows, :], m_s[1, rows, :], m_s[2, rows, :]
        mx = jnp.maximum(jnp.maximum(m0, m1), m2)
        w0, w1, w2 = jnp.exp2(m0 - mx), jnp.exp2(m1 - mx), jnp.exp2(m2 - mx)
        num = w0 * o_s[0, rows, :] + w1 * o_s[1, rows, :] + w2 * o_s[2, rows, :]
        den = w0 * l_s[0, rows, :] + w1 * l_s[1, rows, :] + w2 * l_s[2, rows, :]
        y_ref[rows, :] = (num / den).astype(y_ref.dtype)
        return carry

    lax.fori_loop(0, nblk, combine, 0, unroll=4)


def _dilated_attention(pf, col0, cast_w):
    s = pf.shape[0]
    sup = ATTN_SUPER
    hb = ATTN_HEADS
    d1 = DILATIONS[1]
    qb, kb, vb = (col0 // LANES + i * hb for i in range(3))
    blk = ATTN_BLOCK
    nblk = sup // blk
    tail = lambda b: jnp.maximum(b * nblk - 1, 0)
    grid = (hb, s // sup)
    cspec, cshape = _side_cast(cast_w, grid)
    return pl.pallas_call(
        _attn_kernel,
        grid=grid,
        in_specs=[pl.BlockSpec((sup, LANES), lambda h, b: (b, qb + h)),
                  pl.BlockSpec((blk, LANES), lambda h, b: (tail(b), kb + h)),
                  pl.BlockSpec((sup, LANES), lambda h, b: (b, kb + h)),
                  pl.BlockSpec((blk, LANES), lambda h, b: (tail(b), vb + h)),
                  pl.BlockSpec((sup, LANES), lambda h, b: (b, vb + h)),
                  cspec],
        out_specs=[pl.BlockSpec((sup, LANES), lambda h, b: (b, h)), cspec],
        out_shape=[jax.ShapeDtypeStruct((s, ATTN_WIDTH), BF16), cshape],
        scratch_shapes=[pltpu.VMEM((d1, sup // d1, LANES), F32),
                        pltpu.VMEM((2 * d1, sup // d1, LANES), F32),
                        pltpu.VMEM((2 * d1, sup // d1, LANES), F32),
                        pltpu.VMEM((len(DILATIONS), sup, LANES), F32),
                        pltpu.VMEM((len(DILATIONS), sup, LANES), F32),
                        pltpu.VMEM((len(DILATIONS), sup, LANES), F32),
                        pltpu.VMEM((2, blk, 2 * blk), F32)],
        compiler_params=_params("parallel", "arbitrary"),
        name="dilated_attn",
    )(pf, pf, pf, pf, pf, cast_w)


def _gate_weights(w_in_t, i_bias_l, f_bias_l):
    g0 = 2 * MLSTM_QK_WIDTH + 2 * MLSTM_WIDTH
    H = MLSTM_HEADS
    wg = jnp.stack([w_in_t[g0:g0 + H], w_in_t[g0 + H:g0 + 2 * H]], axis=1).reshape(2 * H, w_in_t.shape[1])
    wg = jnp.pad(wg, ((0, LANES - 2 * H), (0, 0)))
    bg = jnp.stack([i_bias_l, f_bias_l], axis=-1).reshape(1, 2 * H)
    bg = jnp.pad(bg, ((0, 0), (0, LANES - 2 * H)))
    return wg, bg.astype(F32)


def _layer(x, norm_mix, w_in, conv_w, conv_b, i_bias, f_bias, mlstm_norm, w_out, norm_mlp, w_up, w_down):
    qk_w = 2 * MLSTM_QK_WIDTH
    g0 = qk_w + 2 * MLSTM_WIDTH
    a0 = g0 + 2 * MLSTM_HEADS
    n_proj = w_in.shape[1] - 2 * MLSTM_HEADS
    assert a0 - g0 == SUBLANES
    w_in_t = w_in.T
    w_g, b_g = _gate_weights(w_in_t, i_bias, f_bias)

    h = _rmsnorm(x, norm_mix, BF16)
    o0 = qk_w + MLSTM_WIDTH
    col_scales = ((o0, g0, 0.5),
                  (g0, g0 + ATTN_WIDTH, ATTN_QSCALE))
    proj, w_up_b = _in_proj(h, w_in_t, g0, n_proj, col_scales, w_up, CAST_SLABS)
    gates = _gate_proj(h, w_g)
    y_a, w_out_b = _dilated_attention(proj, g0, w_out)
    y_m = _mlstm(proj, gates, b_g, conv_w, conv_b.reshape(1, -1), mlstm_norm.reshape(1, -1))
    x1, x1g, ssq = _out_proj(y_m, y_a, w_out_b, x, norm_mlp)
    u, w_down_b = _up_proj(x1g, ssq, w_up_b, w_down)
    return _matmul_ktiled_res(u, w_down_b, x1)


def kernel(x, norm_mix, w_in, conv_w, conv_b, i_bias, f_bias, mlstm_norm, w_out, norm_mlp, w_up, w_down, norm_final):
    b, s, d = x.shape
    outs = []
    for bi in range(b):
        xb = x[bi]
        for layer in range(norm_mix.shape[0]):
            xb = _layer(xb, norm_mix[layer], w_in[layer], conv_w[layer], conv_b[layer], i_bias[layer], f_bias[layer],
                        mlstm_norm[layer], w_out[layer], norm_mlp[layer], w_up[layer], w_down[layer])
        outs.append(_rmsnorm(xb, norm_final, x.dtype))
    return jnp.stack(outs, axis=0)
```

```python
import functools

import jax
import jax.numpy as jnp
from jax import lax
from jax.experimental import pallas as pl
from jax.experimental.pallas import tpu as pltpu

F32 = jnp.float32
BF16 = jnp.bfloat16

D_MODEL = 4096
MLSTM_HEADS = 4
MLSTM_DV = 512
MLSTM_DQK = 256
MLSTM_QK_WIDTH = MLSTM_HEADS * MLSTM_DQK
MLSTM_WIDTH = MLSTM_HEADS * MLSTM_DV
CONV_WIDTH = 4
GATE_SOFTCAP = 15.0
ATTN_HEAD_DIM = 128
ATTN_HEADS = 16
ATTN_WIDTH = ATTN_HEADS * ATTN_HEAD_DIM
ATTN_BLOCK = 128
DILATIONS = (1, 4, 16)
D_FF = 4 * D_MODEL
EPS = 1e-6

LANES = 128
SUBLANES = 8
VMEM_LIMIT_BYTES = 58 * 1024 * 1024

MM_TILE = 1024
MM_K_TILE = 4096
NORM_ROWS = 512
CAST_SLABS = 128
MLSTM_CHUNK = 256
ATTN_SUPER = DILATIONS[-1] * ATTN_BLOCK
ATTN_QSCALE = 1.4426950408889634 * ATTN_HEAD_DIM ** -0.5


def _params(*sem):
    return pltpu.CompilerParams(dimension_semantics=sem, vmem_limit_bytes=VMEM_LIMIT_BYTES)


def _rmsnorm_kernel(x_ref, g_ref, o_ref):
    x = x_ref[...]
    ms = jnp.mean(x * x, axis=-1, keepdims=True)
    o_ref[...] = (x * lax.rsqrt(ms + EPS) * g_ref[...]).astype(o_ref.dtype)


def _rmsnorm(x, g, out_dtype):
    s, d = x.shape
    return pl.pallas_call(
        _rmsnorm_kernel,
        grid=(s // NORM_ROWS,),
        in_specs=[pl.BlockSpec((NORM_ROWS, d), lambda i: (i, 0)),
                  pl.BlockSpec((1, d), lambda i: (0, 0))],
        out_specs=pl.BlockSpec((NORM_ROWS, d), lambda i: (i, 0)),
        out_shape=jax.ShapeDtypeStruct((s, d), out_dtype),
        compiler_params=_params("parallel"),
        name="rmsnorm",
    )(x, g.reshape(1, d))


def _side_cast(w, grid, slabs=None):
    steps = 1
    for g in grid:
        steps *= g
    slabs = steps if slabs is None else slabs
    rows = w.shape[0] // slabs
    assert slabs <= steps and rows * slabs == w.shape[0] and rows % (2 * SUBLANES) == 0

    def index_map(*ids):
        lin = ids[0]
        for g, i in zip(grid[1:], ids[1:]):
            lin = lin * g + i
        return (jnp.minimum(lin, slabs - 1), 0)

    spec = pl.BlockSpec((rows, w.shape[1]), index_map)
    return spec, jax.ShapeDtypeStruct(w.shape, BF16)


def _in_proj_kernel(a_ref, w_ref, nxt_ref, wsrc_ref, o_ref, wdst_ref, wb_s, *, first_shifted, tile_scales):
    wdst_ref[...] = wsrc_ref[...].astype(wdst_ref.dtype)
    j = pl.program_id(0)
    first_row_tile = pl.program_id(1) == 0

    @pl.when(first_row_tile & (j < first_shifted))
    def _():
        wb_s[...] = w_ref[...].astype(BF16)

    @pl.when(first_row_tile & (j >= first_shifted))
    def _():
        tn = w_ref.shape[0]
        body = tn - 2 * SUBLANES
        wb_s[0:body, :] = w_ref[SUBLANES:SUBLANES + body, :].astype(BF16)
        wb_s[body:tn, :] = jnp.concatenate([w_ref[tn - SUBLANES:tn, :], nxt_ref[...]], axis=0).astype(BF16)

    acc = lax.dot_general(a_ref[...], wb_s[...], (((1,), (1,)), ((), ())), preferred_element_type=F32)
    scale = 1.0
    for lo, hi, value in tile_scales:
        scale = jnp.where((j >= lo) & (j < hi), value, scale)
    o_ref[...] = acc * scale


def _in_proj(a, wt, gap_row, n_out, col_scales, cast_w, cast_slabs):
    m, k = a.shape
    tm, tn = MM_TILE, MM_TILE // 2
    assert gap_row % tn == 0 and n_out % tn == 0 and wt.shape[0] == n_out + SUBLANES
    assert all(lo % tn == 0 and hi % tn == 0 for lo, hi, _ in col_scales)
    per = tn // SUBLANES
    grid = (n_out // tn, m // tm)
    cspec, cshape = _side_cast(cast_w, grid, cast_slabs)
    return pl.pallas_call(
        functools.partial(_in_proj_kernel, first_shifted=gap_row // tn,
                          tile_scales=tuple((lo // tn, hi // tn, value) for lo, hi, value in col_scales)),
        grid=grid,
        in_specs=[pl.BlockSpec((tm, k), lambda j, i: (i, 0)),
                  pl.BlockSpec((tn, k), lambda j, i: (j, 0)),
                  pl.BlockSpec((SUBLANES, k), lambda j, i: ((j + 1) * per, 0)),
                  cspec],
        out_specs=[pl.BlockSpec((tm, tn), lambda j, i: (i, j)), cspec],
        out_shape=[jax.ShapeDtypeStruct((m, n_out), F32), cshape],
        scratch_shapes=[pltpu.VMEM((tn, k), BF16)],
        compiler_params=_params("arbitrary", "arbitrary"),
        name="in_proj",
    )(a, wt, wt, cast_w)


def _gate_proj_kernel(a_ref, wt_ref, o_ref):
    o_ref[...] = lax.dot_general(a_ref[...], wt_ref[...].astype(BF16), (((1,), (1,)), ((), ())),
                                 preferred_element_type=F32)


def _gate_proj(a, wt):
    m, k = a.shape
    n = wt.shape[0]
    tm = MM_TILE
    return pl.pallas_call(
        _gate_proj_kernel,
        grid=(m // tm,),
        in_specs=[pl.BlockSpec((tm, k), lambda i: (i, 0)),
                  pl.BlockSpec((n, k), lambda i: (0, 0))],
        out_specs=pl.BlockSpec((tm, n), lambda i: (i, 0)),
        out_shape=jax.ShapeDtypeStruct((m, n), F32),
        compiler_params=_params("parallel"),
        name="in_proj_gates",
    )(a, wt)


def _up_proj_kernel(a_ref, ssq_ref, b_ref, wsrc_ref, o_ref, wdst_ref, *, d_norm):
    wdst_ref[...] = wsrc_ref[...].astype(wdst_ref.dtype)
    acc = jnp.dot(a_ref[...], b_ref[...], preferred_element_type=F32)
    acc = acc * lax.rsqrt(ssq_ref[:, 0:1] * (1.0 / d_norm) + EPS)
    acc = jnp.maximum(acc, 0.0)
    o_ref[...] = (acc * acc).astype(o_ref.dtype)


def _up_proj(xg, ssq, w, cast_w):
    m, k = xg.shape
    n = w.shape[1]
    tm = tn = MM_TILE
    grid = (m // tm, n // tn)
    cspec, cshape = _side_cast(cast_w, grid)
    return pl.pallas_call(
        functools.partial(_up_proj_kernel, d_norm=k),
        grid=grid,
        in_specs=[pl.BlockSpec((tm, k), lambda i, j: (i, 0)),
                  pl.BlockSpec((tm, LANES), lambda i, j: (i, 0)),
                  pl.BlockSpec((k, tn), lambda i, j: (0, j)),
                  cspec],
        out_specs=[pl.BlockSpec((tm, tn), lambda i, j: (i, j)), cspec],
        out_shape=[jax.ShapeDtypeStruct((m, n), BF16), cshape],
        compiler_params=_params("parallel", "parallel"),
        name="up_proj",
    )(xg, ssq, w, cast_w)


def _out_proj_kernel(a0_ref, a1_ref, b0_ref, b1_ref, r_ref, g_ref, o_ref, xg_ref, ssq_ref):
    acc = jnp.dot(a0_ref[...], b0_ref[...], preferred_element_type=F32)
    acc = acc + jnp.dot(a1_ref[...], b1_ref[...], preferred_element_type=F32)
    x1 = r_ref[...] + acc
    o_ref[...] = x1
    xg_ref[...] = (x1 * g_ref[...]).astype(xg_ref.dtype)
    part = jnp.broadcast_to(jnp.sum(x1 * x1, axis=-1, keepdims=True), ssq_ref.shape)
    j = pl.program_id(1)

    @pl.when(j == 0)
    def _():
        ssq_ref[...] = part

    @pl.when(j > 0)
    def _():
        ssq_ref[...] += part


def _out_proj(y_m, y_a, w, res, gain):
    m, k0 = y_m.shape
    k1 = y_a.shape[1]
    assert k0 == k1
    n = w.shape[1]
    tm, tn = MM_TILE, MM_TILE // 2
    return pl.pallas_call(
        _out_proj_kernel,
        grid=(m // tm, n // tn),
        in_specs=[pl.BlockSpec((tm, k0), lambda i, j: (i, 0)),
                  pl.BlockSpec((tm, k1), lambda i, j: (i, 0)),
                  pl.BlockSpec((k0, tn), lambda i, j: (0, j)),
                  pl.BlockSpec((k1, tn), lambda i, j: (1, j)),
                  pl.BlockSpec((tm, tn), lambda i, j: (i, j)),
                  pl.BlockSpec((1, tn), lambda i, j: (0, j))],
        out_specs=[pl.BlockSpec((tm, tn), lambda i, j: (i, j)),
                   pl.BlockSpec((tm, tn), lambda i, j: (i, j)),
                   pl.BlockSpec((tm, LANES), lambda i, j: (i, 0))],
        out_shape=[jax.ShapeDtypeStruct((m, n), F32),
                   jax.ShapeDtypeStruct((m, n), BF16),
                   jax.ShapeDtypeStruct((m, LANES), F32)],
        compiler_params=_params("parallel", "arbitrary"),
        name="out_proj",
    )(y_m, y_a, w, w, res, gain.reshape(1, n))


def _mmk_res_kernel(a_ref, b_ref, r_ref, o_ref):
    @pl.when(pl.program_id(2) == 0)
    def _():
        o_ref[...] = r_ref[...]

    o_ref[...] += jnp.dot(a_ref[...], b_ref[...], preferred_element_type=F32)


def _matmul_ktiled_res(a, b, res):
    m, k = a.shape
    n = b.shape[1]
    tm = tn = MM_TILE
    tk = MM_K_TILE
    return pl.pallas_call(
        _mmk_res_kernel,
        grid=(m // tm, n // tn, k // tk),
        in_specs=[pl.BlockSpec((tm, tk), lambda i, j, kk: (i, kk)),
                  pl.BlockSpec((tk, tn), lambda i, j, kk: (kk, j)),
                  pl.BlockSpec((tm, tn), lambda i, j, kk: (i, j))],
        out_specs=pl.BlockSpec((tm, tn), lambda i, j, kk: (i, j)),
        out_shape=jax.ShapeDtypeStruct((m, n), F32),
        compiler_params=_params("parallel", "parallel", "arbitrary"),
        name="down_proj",
    )(a, b, res)


def _split3(a):
    hi = a.astype(BF16)
    r1 = a - hi.astype(F32)
    mid = r1.astype(BF16)
    lo = (r1 - mid.astype(F32)).astype(BF16)
    return hi, mid, lo


def _conv_silu(u_ref, tail_ref, ext_ref, w_ref, b_ref):
    chunk = u_ref.shape[0]
    u = u_ref[...]
    ext_ref[0:SUBLANES, :] = tail_ref[...]
    ext_ref[SUBLANES:SUBLANES + chunk, :] = u
    tail_ref[...] = u[chunk - SUBLANES:, :]
    w = 0.5 * w_ref[...]
    y = 0.5 * b_ref[...] + w[CONV_WIDTH - 1:CONV_WIDTH, :] * u
    for j in range(CONV_WIDTH - 1):
        start = SUBLANES - (CONV_WIDTH - 1) + j
        y = y + w[j:j + 1, :] * ext_ref[start:start + chunk, :]
    return y + y * jnp.tanh(y)


def _mlstm_kernel(qk_ref, v_ref, o_ref, g_ref, gb_ref, cw_ref, cb_ref, nrm_ref, y_ref, c_s, n_s, tail_s, ext_s):
    chunk = qk_ref.shape[0]
    dk, dv = MLSTM_DQK, MLSTM_DV

    @pl.when(pl.program_id(0) == 0)
    def _():
        c_s[...] = jnp.zeros_like(c_s)
        n_s[...] = jnp.zeros_like(n_s)
        tail_s[...] = jnp.zeros_like(tail_s)

    qk = _conv_silu(qk_ref, tail_s, ext_s, cw_ref, cb_ref)

    pre = GATE_SOFTCAP * jnp.tanh((g_ref[...] + gb_ref[...]) * (1.0 / GATE_SOFTCAP))
    logsig = jnp.minimum(pre, 0.0) - jnp.log(1.0 + jnp.exp(-jnp.abs(pre)))
    row = lax.broadcasted_iota(jnp.int32, (chunk, chunk), 0)
    col = lax.broadcasted_iota(jnp.int32, (chunk, chunk), 1)
    causal = row >= col
    tri = causal.astype(BF16)
    cum = sum(jnp.dot(tri, piece, preferred_element_type=F32) for piece in _split3(logsig))
    pre_t = pre.T
    cum_t = cum.T

    for hd in range(MLSTM_HEADS):
        g0 = 2 * hd
        q = qk[:, hd * dk:(hd + 1) * dk] * (dk ** -0.5)
        k = qk[:, (MLSTM_HEADS + hd) * dk:(MLSTM_HEADS + hd + 1) * dk]
        qb = q.astype(BF16)
        v = v_ref[:, hd * dv:(hd + 1) * dv].astype(BF16)
        li_col = pre[:, g0:g0 + 1]
        b_col = cum[:, g0 + 1:g0 + 2]
        b_last = cum[chunk - 1:chunk, g0 + 1:g0 + 2]
        r_col = li_col - b_col
        r_row = pre_t[g0:g0 + 1, :] - cum_t[g0 + 1:g0 + 2, :]

        dmat = jnp.where(causal, b_col + r_row, -jnp.inf)
        s_qk = lax.dot_general(qb, k.astype(BF16), (((1,), (1,)), ((), ())), preferred_element_type=F32)
        p = s_qk * jnp.exp(dmat)
        inter_w = jnp.exp(b_col)
        pv = jnp.dot(p.astype(BF16), jnp.concatenate([v, jnp.ones((chunk, LANES), BF16)], axis=1),
                     preferred_element_type=F32)
        num = pv[:, :dv] + inter_w * jnp.dot(qb, c_s[hd].astype(BF16), preferred_element_type=F32)
        den = pv[:, dv:dv + 1] + inter_w * jnp.sum(q * n_s[hd], axis=-1, keepdims=True)
        h = num / jnp.maximum(jnp.abs(den), 1.0)

        kw = k * jnp.exp(b_last + r_col)
        decay = jnp.exp(b_last)
        c_s[hd] = decay * c_s[hd] + lax.dot_general(kw.astype(BF16), v, (((0,), (0,)), ((), ())),
                                                    preferred_element_type=F32)
        n_s[hd] = decay * n_s[hd] + jnp.sum(kw, axis=0, keepdims=True)

        hn = h * lax.rsqrt(jnp.mean(h * h, axis=-1, keepdims=True) + EPS) * (0.5 * nrm_ref[:, hd * dv:(hd + 1) * dv])
        y_ref[:, hd * dv:(hd + 1) * dv] = (hn + hn * jnp.tanh(o_ref[:, hd * dv:(hd + 1) * dv])).astype(y_ref.dtype)


def _mlstm(proj, gates, gate_bias, conv_w, conv_b, mlstm_norm):
    s = proj.shape[0]
    L = MLSTM_CHUNK
    H = MLSTM_HEADS
    dk, dv = MLSTM_DQK, MLSTM_DV
    qkw = 2 * MLSTM_QK_WIDTH
    assert qkw == MLSTM_WIDTH
    return pl.pallas_call(
        _mlstm_kernel,
        grid=(s // L,),
        in_specs=[pl.BlockSpec((L, qkw), lambda c: (c, 0)),
                  pl.BlockSpec((L, MLSTM_WIDTH), lambda c: (c, 1)),
                  pl.BlockSpec((L, MLSTM_WIDTH), lambda c: (c, 2)),
                  pl.BlockSpec((L, LANES), lambda c: (c, 0)),
                  pl.BlockSpec((1, LANES), lambda c: (0, 0)),
                  pl.BlockSpec((CONV_WIDTH, qkw), lambda c: (0, 0)),
                  pl.BlockSpec((1, qkw), lambda c: (0, 0)),
                  pl.BlockSpec((1, MLSTM_WIDTH), lambda c: (0, 0))],
        out_specs=pl.BlockSpec((L, MLSTM_WIDTH), lambda c: (c, 0)),
        out_shape=jax.ShapeDtypeStruct((s, MLSTM_WIDTH), BF16),
        scratch_shapes=[pltpu.VMEM((H, dk, dv), F32),
                        pltpu.VMEM((H, 1, dk), F32),
                        pltpu.VMEM((SUBLANES, qkw), F32),
                        pltpu.VMEM((SUBLANES + L, qkw), F32)],
        compiler_params=_params("arbitrary"),
        name="mlstm",
    )(proj, proj, proj, gates, gate_bias, conv_w, conv_b, mlstm_norm)


def _band_softmax(q, k, v, bias):
    s = lax.dot_general(q, k, (((1,), (1,)), ((), ())), preferred_element_type=F32) + bias
    m = jnp.max(s, axis=-1, keepdims=True)
    p = jnp.exp2(s - m).astype(BF16)
    v1 = jnp.concatenate([v, jnp.ones_like(v)], axis=1)
    accl = jnp.dot(p, v1, preferred_element_type=F32)
    return accl[:, :v.shape[1]], m, accl[:, v.shape[1]:]


def _attn_kernel(q_ref, kp_ref, kc_ref, vp_ref, vc_ref, wsrc_ref, y_ref, wdst_ref,
                 q4_s, k4_s, v4_s, o_s, m_s, l_s, bias_s):
    blk = ATTN_BLOCK
    sup = q_ref.shape[0]
    nblk = sup // blk
    d1, d2 = DILATIONS[1], DILATIONS[2]
    sub = sup // d1
    sb = pl.program_id(1)
    cur = (sb & 1) * d1
    prv = d1 - cur

    wdst_ref[...] = wsrc_ref[...].astype(wdst_ref.dtype)

    @pl.when(sb == 0)
    def _():
        for r in range(d1):
            k4_s[prv + r] = jnp.zeros((sub, LANES), F32)
            v4_s[prv + r] = jnp.zeros((sub, LANES), F32)
        qi = lax.broadcasted_iota(jnp.int32, (blk, 2 * blk), 0)
        kj = lax.broadcasted_iota(jnp.int32, (blk, 2 * blk), 1)
        band = (kj >= qi) & (kj <= qi + blk)
        bias_s[0] = jnp.where(band, 0.0, -jnp.inf).astype(F32)
        bias_s[1] = jnp.where(band & (kj >= blk), 0.0, -jnp.inf).astype(F32)

    for r in range(d1):
        q4_s[r] = q_ref[pl.ds(r, sub, stride=d1), :]
        k4_s[cur + r] = kc_ref[pl.ds(r, sub, stride=d1), :]
        v4_s[cur + r] = vc_ref[pl.ds(r, sub, stride=d1), :]

    first_sb = (sb == 0).astype(jnp.int32)

    def run(p_idx, q, k, v, out_rows, first):
        bias = bias_s[first_sb if first else 0]
        acc, m, l = _band_softmax(q.astype(BF16), k.astype(BF16), v.astype(BF16), bias)
        o_s[p_idx, out_rows, :] = acc
        m_s[p_idx, out_rows, :] = jnp.broadcast_to(m, (blk, LANES))
        l_s[p_idx, out_rows, :] = jnp.broadcast_to(l, (blk, LANES))

    for t in range(nblk):
        q0 = t * blk
        k_prev = kp_ref[...] if t == 0 else kc_ref[q0 - blk:q0, :]
        v_prev = vp_ref[...] if t == 0 else vc_ref[q0 - blk:q0, :]
        run(0, q_ref[q0:q0 + blk, :],
            jnp.concatenate([k_prev, kc_ref[q0:q0 + blk, :]], axis=0),
            jnp.concatenate([v_prev, vc_ref[q0:q0 + blk, :]], axis=0),
            pl.ds(q0, blk), t == 0)

        sg, r = divmod(t, d1)
        s0 = sg * blk
        before = (prv + r, pl.ds(sub - blk, blk)) if sg == 0 else (cur + r, pl.ds(s0 - blk, blk))
        own = (cur + r, pl.ds(s0, blk))
        run(1, q4_s[r, s0:s0 + blk, :],
            jnp.concatenate([k4_s[before[0], before[1], :], k4_s[own[0], own[1], :]], axis=0),
            jnp.concatenate([v4_s[before[0], before[1], :], v4_s[own[0], own[1], :]], axis=0),
            pl.ds(sg * (d1 * blk) + r, blk, stride=d1), sg == 0)

        walk = pl.ds(sg, blk, stride=d2 // d1)
        run(2, q4_s[r, walk, :],
            jnp.concatenate([k4_s[prv + r, walk, :], k4_s[cur + r, walk, :]], axis=0),
            jnp.concatenate([v4_s[prv + r, walk, :], v4_s[cur + r, walk, :]], axis=0),
            pl.ds(t, blk, stride=d2), True)

    def combine(t, carry):
        rows = pl.ds(pl.multiple_of(t * blk, blk), blk)
        m0, m1, m2 = m_s[0, rows, :], m_s[1, rows, :], m_s[2, rows, :]
        mx = jnp.maximum(jnp.maximum(m0, m1), m2)
        w0, w1, w2 = jnp.exp2(m0 - mx), jnp.exp2(m1 - mx), jnp.exp2(m2 - mx)
        num = w0 * o_s[0, rows, :] + w1 * o_s[1, rows, :] + w2 * o_s[2, rows, :]
        den = w0 * l_s[0, rows, :] + w1 * l_s[1, rows, :] + w2 * l_s[2, rows, :]
        y_ref[rows, :] = (num / den).astype(y_ref.dtype)
        return carry

    lax.fori_loop(0, nblk, combine, 0, unroll=4)


def _dilated_attention(pf, col0, cast_w):
    s = pf.shape[0]
    sup = ATTN_SUPER
    hb = ATTN_HEADS
    d1 = DILATIONS[1]
    qb, kb, vb = (col0 // LANES + i * hb for i in range(3))
    blk = ATTN_BLOCK
    nblk = sup // blk
    tail = lambda b: jnp.maximum(b * nblk - 1, 0)
    grid = (hb, s // sup)
    cspec, cshape = _side_cast(cast_w, grid)
    return pl.pallas_call(
        _attn_kernel,
        grid=grid,
        in_specs=[pl.BlockSpec((sup, LANES), lambda h, b: (b, qb + h)),
                  pl.BlockSpec((blk, LANES), lambda h, b: (tail(b), kb + h)),
                  pl.BlockSpec((sup, LANES), lambda h, b: (b, kb + h)),
                  pl.BlockSpec((blk, LANES), lambda h, b: (tail(b), vb + h)),
                  pl.BlockSpec((sup, LANES), lambda h, b: (b, vb + h)),
                  cspec],
        out_specs=[pl.BlockSpec((sup, LANES), lambda h, b: (b, h)), cspec],
        out_shape=[jax.ShapeDtypeStruct((s, ATTN_WIDTH), BF16), cshape],
        scratch_shapes=[pltpu.VMEM((d1, sup // d1, LANES), F32),
                        pltpu.VMEM((2 * d1, sup // d1, LANES), F32),
                        pltpu.VMEM((2 * d1, sup // d1, LANES), F32),
                        pltpu.VMEM((len(DILATIONS), sup, LANES), F32),
                        pltpu.VMEM((len(DILATIONS), sup, LANES), F32),
                        pltpu.VMEM((len(DILATIONS), sup, LANES), F32),
                        pltpu.VMEM((2, blk, 2 * blk), F32)],
        compiler_params=_params("parallel", "arbitrary"),
        name="dilated_attn",
    )(pf, pf, pf, pf, pf, cast_w)


def _gate_weights(w_in_t, i_bias_l, f_bias_l):
    g0 = 2 * MLSTM_QK_WIDTH + 2 * MLSTM_WIDTH
    H = MLSTM_HEADS
    wg = jnp.stack([w_in_t[g0:g0 + H], w_in_t[g0 + H:g0 + 2 * H]], axis=1).reshape(2 * H, w_in_t.shape[1])
    wg = jnp.pad(wg, ((0, LANES - 2 * H), (0, 0)))
    bg = jnp.stack([i_bias_l, f_bias_l], axis=-1).reshape(1, 2 * H)
    bg = jnp.pad(bg, ((0, 0), (0, LANES - 2 * H)))
    return wg, bg.astype(F32)


def _layer(x, norm_mix, w_in, conv_w, conv_b, i_bias, f_bias, mlstm_norm, w_out, norm_mlp, w_up, w_down):
    qk_w = 2 * MLSTM_QK_WIDTH
    g0 = qk_w + 2 * MLSTM_WIDTH
    a0 = g0 + 2 * MLSTM_HEADS
    n_proj = w_in.shape[1] - 2 * MLSTM_HEADS
    assert a0 - g0 == SUBLANES
    w_in_t = w_in.T
    w_g, b_g = _gate_weights(w_in_t, i_bias, f_bias)

    h = _rmsnorm(x, norm_mix, BF16)
    o0 = qk_w + MLSTM_WIDTH
    col_scales = ((o0, g0, 0.5),
                  (g0, g0 + ATTN_WIDTH, ATTN_QSCALE))
    proj, w_up_b = _in_proj(h, w_in_t, g0, n_proj, col_scales, w_up, CAST_SLABS)
    gates = _gate_proj(h, w_g)
    y_a, w_out_b = _dilated_attention(proj, g0, w_out)
    y_m = _mlstm(proj, gates, b_g, conv_w, conv_b.reshape(1, -1), mlstm_norm.reshape(1, -1))
    x1, x1g, ssq = _out_proj(y_m, y_a, w_out_b, x, norm_mlp)
    u, w_down_b = _up_proj(x1g, ssq, w_up_b, w_down)
    return _matmul_ktiled_res(u, w_down_b, x1)


def kernel(x, norm_mix, w_in, conv_w, conv_b, i_bias, f_bias, mlstm_norm, w_out, norm_mlp, w_up, w_down, norm_final):
    b, s, d = x.shape
    outs = []
    for bi in range(b):
        xb = x[bi]
        for layer in range(norm_mix.shape[0]):
            xb = _layer(xb, norm_mix[layer], w_in[layer], conv_w[layer], conv_b[layer], i_bias[layer], f_bias[layer],
                        mlstm_norm[layer], w_out[layer], norm_mlp[layer], w_up[layer], w_down[layer])
        outs.append(_rmsnorm(xb, norm_final, x.dtype))
    return jnp.stack(outs, axis=0)
```

```python
import functools

import jax
import jax.numpy as jnp
from jax import lax
from jax.experimental import pallas as pl
from jax.experimental.pallas import tpu as pltpu

F32 = jnp.float32
BF16 = jnp.bfloat16

D_MODEL = 4096
MLSTM_HEADS = 4
MLSTM_DV = 512
MLSTM_DQK = 256
MLSTM_QK_WIDTH = MLSTM_HEADS * MLSTM_DQK
MLSTM_WIDTH = MLSTM_HEADS * MLSTM_DV
CONV_WIDTH = 4
GATE_SOFTCAP = 15.0
ATTN_HEAD_DIM = 128
ATTN_HEADS = 16
ATTN_WIDTH = ATTN_HEADS * ATTN_HEAD_DIM
ATTN_BLOCK = 128
DILATIONS = (1, 4, 16)
D_FF = 4 * D_MODEL
EPS = 1e-6

LANES = 128
SUBLANES = 8
VMEM_LIMIT_BYTES = 58 * 1024 * 1024

MM_TILE = 1024
MM_K_TILE = 4096
NORM_ROWS = 512
CAST_SLABS = 128
MLSTM_CHUNK = 256
ATTN_SUPER = DILATIONS[-1] * ATTN_BLOCK
ATTN_QSCALE = 1.4426950408889634 * ATTN_HEAD_DIM ** -0.5


def _params(*sem):
    return pltpu.CompilerParams(dimension_semantics=sem, vmem_limit_bytes=VMEM_LIMIT_BYTES)


def _rmsnorm_kernel(x_ref, g_ref, o_ref):
    x = x_ref[...]
    ms = jnp.mean(x * x, axis=-1, keepdims=True)
    o_ref[...] = (x * lax.rsqrt(ms + EPS) * g_ref[...]).astype(o_ref.dtype)


def _rmsnorm(x, g, out_dtype):
    s, d = x.shape
    return pl.pallas_call(
        _rmsnorm_kernel,
        grid=(s // NORM_ROWS,),
        in_specs=[pl.BlockSpec((NORM_ROWS, d), lambda i: (i, 0)),
                  pl.BlockSpec((1, d), lambda i: (0, 0))],
        out_specs=pl.BlockSpec((NORM_ROWS, d), lambda i: (i, 0)),
        out_shape=jax.ShapeDtypeStruct((s, d), out_dtype),
        compiler_params=_params("parallel"),
        name="rmsnorm",
    )(x, g.reshape(1, d))


def _side_cast(w, grid, slabs=None):
    steps = 1
    for g in grid:
        steps *= g
    slabs = steps if slabs is None else slabs
    rows = w.shape[0] // slabs
    assert slabs <= steps and rows * slabs == w.shape[0] and rows % (2 * SUBLANES) == 0

    def index_map(*ids):
        lin = ids[0]
        for g, i in zip(grid[1:], ids[1:]):
            lin = lin * g + i
        return (jnp.minimum(lin, slabs - 1), 0)

    spec = pl.BlockSpec((rows, w.shape[1]), index_map)
    return spec, jax.ShapeDtypeStruct(w.shape, BF16)


def _in_proj_kernel(a_ref, w_ref, nxt_ref, wsrc_ref, o_ref, wdst_ref, wb_s, *, first_shifted, tile_scales):
    wdst_ref[...] = wsrc_ref[...].astype(wdst_ref.dtype)
    j = pl.program_id(0)
    first_row_tile = pl.program_id(1) == 0

    @pl.when(first_row_tile & (j < first_shifted))
    def _():
        wb_s[...] = w_ref[...].astype(BF16)

    @pl.when(first_row_tile & (j >= first_shifted))
    def _():
        tn = w_ref.shape[0]
        body = tn - 2 * SUBLANES
        wb_s[0:body, :] = w_ref[SUBLANES:SUBLANES + body, :].astype(BF16)
        wb_s[body:tn, :] = jnp.concatenate([w_ref[tn - SUBLANES:tn, :], nxt_ref[...]], axis=0).astype(BF16)

    acc = lax.dot_general(a_ref[...], wb_s[...], (((1,), (1,)), ((), ())), preferred_element_type=F32)
    scale = 1.0
    for lo, hi, value in tile_scales:
        scale = jnp.where((j >= lo) & (j < hi), value, scale)
    o_ref[...] = acc * scale


def _in_proj(a, wt, gap_row, n_out, col_scales, cast_w, cast_slabs):
    m, k = a.shape
    tm, tn = MM_TILE, MM_TILE // 2
    assert gap_row % tn == 0 and n_out % tn == 0 and wt.shape[0] == n_out + SUBLANES
    assert all(lo % tn == 0 and hi % tn == 0 for lo, hi, _ in col_scales)
    per = tn // SUBLANES
    grid = (n_out // tn, m // tm)
    cspec, cshape = _side_cast(cast_w, grid, cast_slabs)
    return pl.pallas_call(
        functools.partial(_in_proj_kernel, first_shifted=gap_row // tn,
                          tile_scales=tuple((lo // tn, hi // tn, value) for lo, hi, value in col_scales)),
        grid=grid,
        in_specs=[pl.BlockSpec((tm, k), lambda j, i: (i, 0)),
                  pl.BlockSpec((tn, k), lambda j, i: (j, 0)),
                  pl.BlockSpec((SUBLANES, k), lambda j, i: ((j + 1) * per, 0)),
                  cspec],
        out_specs=[pl.BlockSpec((tm, tn), lambda j, i: (i, j)), cspec],
        out_shape=[jax.ShapeDtypeStruct((m, n_out), F32), cshape],
        scratch_shapes=[pltpu.VMEM((tn, k), BF16)],
        compiler_params=_params("arbitrary", "arbitrary"),
        name="in_proj",
    )(a, wt, wt, cast_w)


def _gate_proj_kernel(a_ref, wt_ref, o_ref):
    o_ref[...] = lax.dot_general(a_ref[...], wt_ref[...].astype(BF16), (((1,), (1,)), ((), ())),
                                 preferred_element_type=F32)


def _gate_proj(a, wt):
    m, k = a.shape
    n = wt.shape[0]
    tm = MM_TILE
    return pl.pallas_call(
        _gate_proj_kernel,
        grid=(m // tm,),
        in_specs=[pl.BlockSpec((tm, k), lambda i: (i, 0)),
                  pl.BlockSpec((n, k), lambda i: (0, 0))],
        out_specs=pl.BlockSpec((tm, n), lambda i: (i, 0)),
        out_shape=jax.ShapeDtypeStruct((m, n), F32),
        compiler_params=_params("parallel"),
        name="in_proj_gates",
    )(a, wt)


def _up_proj_kernel(a_ref, ssq_ref, b_ref, wsrc_ref, o_ref, wdst_ref, *, d_norm):
    wdst_ref[...] = wsrc_ref[...].astype(wdst_ref.dtype)
    acc = jnp.dot(a_ref[...], b_ref[...], preferred_element_type=F32)
    acc = acc * lax.rsqrt(ssq_ref[:, 0:1] * (1.0 / d_norm) + EPS)
    acc = jnp.maximum(acc, 0.0)
    o_ref[...] = (acc * acc).astype(o_ref.dtype)


def _up_proj(xg, ssq, w, cast_w):
    m, k = xg.shape
    n = w.shape[1]
    tm = tn = MM_TILE
    grid = (m // tm, n // tn)
    cspec, cshape = _side_cast(cast_w, grid)
    return pl.pallas_call(
        functools.partial(_up_proj_kernel, d_norm=k),
        grid=grid,
        in_specs=[pl.BlockSpec((tm, k), lambda i, j: (i, 0)),
                  pl.BlockSpec((tm, LANES), lambda i, j: (i, 0)),
                  pl.BlockSpec((k, tn), lambda i, j: (0, j)),
                  cspec],
        out_specs=[pl.BlockSpec((tm, tn), lambda i, j: (i, j)), cspec],
        out_shape=[jax.ShapeDtypeStruct((m, n), BF16), cshape],
        compiler_params=_params("parallel", "parallel"),
        name="up_proj",
    )(xg, ssq, w, cast_w)


def _out_proj_kernel(a0_ref, a1_ref, b0_ref, b1_ref, r_ref, g_ref, o_ref, xg_ref, ssq_ref):
    acc = jnp.dot(a0_ref[...], b0_ref[...], preferred_element_type=F32)
    acc = acc + jnp.dot(a1_ref[...], b1_ref[...], preferred_element_type=F32)
    x1 = r_ref[...] + acc
    o_ref[...] = x1
    xg_ref[...] = (x1 * g_ref[...]).astype(xg_ref.dtype)
    part = jnp.broadcast_to(jnp.sum(x1 * x1, axis=-1, keepdims=True), ssq_ref.shape)
    j = pl.program_id(1)

    @pl.when(j == 0)
    def _():
        ssq_ref[...] = part

    @pl.when(j > 0)
    def _():
        ssq_ref[...] += part


def _out_proj(y_m, y_a, w, res, gain):
    m, k0 = y_m.shape
    k1 = y_a.shape[1]
    assert k0 == k1
    n = w.shape[1]
    tm, tn = MM_TILE, MM_TILE // 2
    return pl.pallas_call(
        _out_proj_kernel,
        grid=(m // tm, n // tn),
        in_specs=[pl.BlockSpec((tm, k0), lambda i, j: (i, 0)),
                  pl.BlockSpec((tm, k1), lambda i, j: (i, 0)),
                  pl.BlockSpec((k0, tn), lambda i, j: (0, j)),
                  pl.BlockSpec((k1, tn), lambda i, j: (1, j)),
                  pl.BlockSpec((tm, tn), lambda i, j: (i, j)),
                  pl.BlockSpec((1, tn), lambda i, j: (0, j))],
        out_specs=[pl.BlockSpec((tm, tn), lambda i, j: (i, j)),
                   pl.BlockSpec((tm, tn), lambda i, j: (i, j)),
                   pl.BlockSpec((tm, LANES), lambda i, j: (i, 0))],
        out_shape=[jax.ShapeDtypeStruct((m, n), F32),
                   jax.ShapeDtypeStruct((m, n), BF16),
                   jax.ShapeDtypeStruct((m, LANES), F32)],
        compiler_params=_params("parallel", "arbitrary"),
        name="out_proj",
    )(y_m, y_a, w, w, res, gain.reshape(1, n))


def _mmk_res_kernel(a_ref, b_ref, r_ref, o_ref):
    @pl.when(pl.program_id(2) == 0)
    def _():
        o_ref[...] = r_ref[...]

    o_ref[...] += jnp.dot(a_ref[...], b_ref[...], preferred_element_type=F32)


def _matmul_ktiled_res(a, b, res):
    m, k = a.shape
    n = b.shape[1]
    tm = tn = MM_TILE
    tk = MM_K_TILE
    return pl.pallas_call(
        _mmk_res_kernel,
        grid=(m // tm, n // tn, k // tk),
        in_specs=[pl.BlockSpec((tm, tk), lambda i, j, kk: (i, kk)),
                  pl.BlockSpec((tk, tn), lambda i, j, kk: (kk, j)),
                  pl.BlockSpec((tm, tn), lambda i, j, kk: (i, j))],
        out_specs=pl.BlockSpec((tm, tn), lambda i, j, kk: (i, j)),
        out_shape=jax.ShapeDtypeStruct((m, n), F32),
        compiler_params=_params("parallel", "parallel", "arbitrary"),
        name="down_proj",
    )(a, b, res)


def _split3(a):
    hi = a.astype(BF16)
    r1 = a - hi.astype(F32)
    mid = r1.astype(BF16)
    lo = (r1 - mid.astype(F32)).astype(BF16)
    return hi, mid, lo


def _conv_silu(u_ref, tail_ref, ext_ref, w_ref, b_ref, cols):
    chunk = u_ref.shape[0]
    u = u_ref[:, cols]
    ext_ref[0:SUBLANES, cols] = tail_ref[:, cols]
    ext_ref[SUBLANES:SUBLANES + chunk, cols] = u
    tail_ref[:, cols] = u[chunk - SUBLANES:, :]
    w = 0.5 * w_ref[:, cols]
    y = 0.5 * b_ref[:, cols] + w[CONV_WIDTH - 1:CONV_WIDTH, :] * u
    for j in range(CONV_WIDTH - 1):
        start = SUBLANES - (CONV_WIDTH - 1) + j
        y = y + w[j:j + 1, :] * ext_ref[start:start + chunk, cols]
    return y + y * jnp.tanh(y)


def _mlstm_kernel(qk_ref, v_ref, o_ref, g_ref, gb_ref, cw_ref, cb_ref, nrm_ref, y_ref, c_s, n_s, tail_s, ext_s):
    chunk = qk_ref.shape[0]
    dk, dv = MLSTM_DQK, MLSTM_DV

    @pl.when(pl.program_id(0) == 0)
    def _():
        c_s[...] = jnp.zeros_like(c_s)
        n_s[...] = jnp.zeros_like(n_s)
        tail_s[...] = jnp.zeros_like(tail_s)


    pre = GATE_SOFTCAP * jnp.tanh((g_ref[...] + gb_ref[...]) * (1.0 / GATE_SOFTCAP))
    logsig = jnp.minimum(pre, 0.0) - jnp.log(1.0 + jnp.exp(-jnp.abs(pre)))
    row = lax.broadcasted_iota(jnp.int32, (chunk, chunk), 0)
    col = lax.broadcasted_iota(jnp.int32, (chunk, chunk), 1)
    causal = row >= col
    tri = causal.astype(BF16)
    cum = sum(jnp.dot(tri, piece, preferred_element_type=F32) for piece in _split3(logsig))
    pre_t = pre.T
    cum_t = cum.T

    for hd in range(MLSTM_HEADS):
        g0 = 2 * hd
        q_cols = slice(hd * dk, (hd + 1) * dk)
        k_cols = slice((MLSTM_HEADS + hd) * dk, (MLSTM_HEADS + hd + 1) * dk)
        q = _conv_silu(qk_ref, tail_s, ext_s, cw_ref, cb_ref, q_cols) * (dk ** -0.5)
        k = _conv_silu(qk_ref, tail_s, ext_s, cw_ref, cb_ref, k_cols)
        qb = q.astype(BF16)
        v = v_ref[:, hd * dv:(hd + 1) * dv].astype(BF16)
        li_col = pre[:, g0:g0 + 1]
        b_col = cum[:, g0 + 1:g0 + 2]
        b_last = cum[chunk - 1:chunk, g0 + 1:g0 + 2]
        r_col = li_col - b_col
        r_row = pre_t[g0:g0 + 1, :] - cum_t[g0 + 1:g0 + 2, :]

        dmat = jnp.where(causal, b_col + r_row, -jnp.inf)
        s_qk = lax.dot_general(qb, k.astype(BF16), (((1,), (1,)), ((), ())), preferred_element_type=F32)
        p = s_qk * jnp.exp(dmat)
        inter_w = jnp.exp(b_col)
        pv = jnp.dot(p.astype(BF16), jnp.concatenate([v, jnp.ones((chunk, LANES), BF16)], axis=1),
                     preferred_element_type=F32)
        num = pv[:, :dv] + inter_w * jnp.dot(qb, c_s[hd].astype(BF16), preferred_element_type=F32)
        den = pv[:, dv:dv + 1] + inter_w * jnp.sum(q * n_s[hd], axis=-1, keepdims=True)
        h = num / jnp.maximum(jnp.abs(den), 1.0)

        kw = k * jnp.exp(b_last + r_col)
        decay = jnp.exp(b_last)
        c_s[hd] = decay * c_s[hd] + lax.dot_general(kw.astype(BF16), v, (((0,), (0,)), ((), ())),
                                                    preferred_element_type=F32)
        n_s[hd] = decay * n_s[hd] + jnp.sum(kw, axis=0, keepdims=True)

        hn = h * lax.rsqrt(jnp.mean(h * h, axis=-1, keepdims=True) + EPS) * (0.5 * nrm_ref[:, hd * dv:(hd + 1) * dv])
        y_ref[:, hd * dv:(hd + 1) * dv] = (hn + hn * jnp.tanh(o_ref[:, hd * dv:(hd + 1) * dv])).astype(y_ref.dtype)


def _mlstm(proj, gates, gate_bias, conv_w, conv_b, mlstm_norm):
    s = proj.shape[0]
    L = MLSTM_CHUNK
    H = MLSTM_HEADS
    dk, dv = MLSTM_DQK, MLSTM_DV
    qkw = 2 * MLSTM_QK_WIDTH
    assert qkw == MLSTM_WIDTH
    return pl.pallas_call(
        _mlstm_kernel,
        grid=(s // L,),
        in_specs=[pl.BlockSpec((L, qkw), lambda c: (c, 0)),
                  pl.BlockSpec((L, MLSTM_WIDTH), lambda c: (c, 1)),
                  pl.BlockSpec((L, MLSTM_WIDTH), lambda c: (c, 2)),
                  pl.BlockSpec((L, LANES), lambda c: (c, 0)),
                  pl.BlockSpec((1, LANES), lambda c: (0, 0)),
                  pl.BlockSpec((CONV_WIDTH, qkw), lambda c: (0, 0)),
                  pl.BlockSpec((1, qkw), lambda c: (0, 0)),
                  pl.BlockSpec((1, MLSTM_WIDTH), lambda c: (0, 0))],
        out_specs=pl.BlockSpec((L, MLSTM_WIDTH), lambda c: (c, 0)),
        out_shape=jax.ShapeDtypeStruct((s, MLSTM_WIDTH), BF16),
        scratch_shapes=[pltpu.VMEM((H, dk, dv), F32),
                        pltpu.VMEM((H, 1, dk), F32),
                        pltpu.VMEM((SUBLANES, qkw), F32),
                        pltpu.VMEM((SUBLANES + L, qkw), F32)],
        compiler_params=_params("arbitrary"),
        name="mlstm",
    )(proj, proj, proj, gates, gate_bias, conv_w, conv_b, mlstm_norm)


def _band_softmax(q, k, v, bias):
    s = lax.dot_general(q, k, (((1,), (1,)), ((), ())), preferred_element_type=F32) + bias
    m = jnp.max(s, axis=-1, keepdims=True)
    p = jnp.exp2(s - m).astype(BF16)
    v1 = jnp.concatenate([v, jnp.ones_like(v)], axis=1)
    accl = jnp.dot(p, v1, preferred_element_type=F32)
    return accl[:, :v.shape[1]], m, accl[:, v.shape[1]:]


def _attn_kernel(q_ref, kp_ref, kc_ref, vp_ref, vc_ref, wsrc_ref, y_ref, wdst_ref,
                 q4_s, k4_s, v4_s, o_s, m_s, l_s, bias_s):
    blk = ATTN_BLOCK
    sup = q_ref.shape[0]
    nblk = sup // blk
    d1, d2 = DILATIONS[1], DILATIONS[2]
    sub = sup // d1
    sb = pl.program_id(1)
    cur = (sb & 1) * d1
    prv = d1 - cur

    wdst_ref[...] = wsrc_ref[...].astype(wdst_ref.dtype)

    @pl.when(sb == 0)
    def _():
        for r in range(d1):
            k4_s[prv + r] = jnp.zeros((sub, LANES), F32)
            v4_s[prv + r] = jnp.zeros((sub, LANES), F32)
        qi = lax.broadcasted_iota(jnp.int32, (blk, 2 * blk), 0)
        kj = lax.broadcasted_iota(jnp.int32, (blk, 2 * blk), 1)
        band = (kj >= qi) & (kj <= qi + blk)
        bias_s[0] = jnp.where(band, 0.0, -jnp.inf).astype(F32)
        bias_s[1] = jnp.where(band & (kj >= blk), 0.0, -jnp.inf).astype(F32)

    for r in range(d1):
        q4_s[r] = q_ref[pl.ds(r, sub, stride=d1), :]
        k4_s[cur + r] = kc_ref[pl.ds(r, sub, stride=d1), :]
        v4_s[cur + r] = vc_ref[pl.ds(r, sub, stride=d1), :]

    first_sb = (sb == 0).astype(jnp.int32)

    def run(p_idx, q, k, v, out_rows, first):
        bias = bias_s[first_sb if first else 0]
        acc, m, l = _band_softmax(q.astype(BF16), k.astype(BF16), v.astype(BF16), bias)
        o_s[p_idx, out_rows, :] = acc
        m_s[p_idx, out_rows, :] = jnp.broadcast_to(m, (blk, LANES))
        l_s[p_idx, out_rows, :] = jnp.broadcast_to(l, (blk, LANES))

    for t in range(nblk):
        q0 = t * blk
        k_prev = kp_ref[...] if t == 0 else kc_ref[q0 - blk:q0, :]
        v_prev = vp_ref[...] if t == 0 else vc_ref[q0 - blk:q0, :]
        run(0, q_ref[q0:q0 + blk, :],
            jnp.concatenate([k_prev, kc_ref[q0:q0 + blk, :]], axis=0),
            jnp.concatenate([v_prev, vc_ref[q0:q0 + blk, :]], axis=0),
            pl.ds(q0, blk), t == 0)

        sg, r = divmod(t, d1)
        s0 = sg * blk
        before = (prv + r, pl.ds(sub - blk, blk)) if sg == 0 else (cur + r, pl.ds(s0 - blk, blk))
        own = (cur + r, pl.ds(s0, blk))
        run(1, q4_s[r, s0:s0 + blk, :],
            jnp.concatenate([k4_s[before[0], before[1], :], k4_s[own[0], own[1], :]], axis=0),
            jnp.concatenate([v4_s[before[0], before[1], :], v4_s[own[0], own[1], :]], axis=0),
            pl.ds(sg * (d1 * blk) + r, blk, stride=d1), sg == 0)

        walk = pl.ds(sg, blk, stride=d2 // d1)
        run(2, q4_s[r, walk, :],
            jnp.concatenate([k4_s[prv + r, walk, :], k4_s[cur + r, walk, :]], axis=0),
            jnp.concatenate([v4_s[prv + r, walk, :], v4_s[cur + r, walk, :]], axis=0),
            pl.ds(t, blk, stride=d2), True)

    def combine(t, carry):
        rows = pl.ds(pl.multiple_of(t * blk, blk), blk)
        m0, m1, m2 = m_s[0, rows, :], m_s[1, rows, :], m_s[2, rows, :]
        mx = jnp.maximum(jnp.maximum(m0, m1), m2)
        w0, w1, w2 = jnp.exp2(m0 - mx), jnp.exp2(m1 - mx), jnp.exp2(m2 - mx)
        num = w0 * o_s[0, rows, :] + w1 * o_s[1, rows, :] + w2 * o_s[2, rows, :]
        den = w0 * l_s[0, rows, :] + w1 * l_s[1, rows, :] + w2 * l_s[2, rows, :]
        y_ref[rows, :] = (num / den).astype(y_ref.dtype)
        return carry

    lax.fori_loop(0, nblk, combine, 0, unroll=4)


def _dilated_attention(pf, col0, cast_w):
    s = pf.shape[0]
    sup = ATTN_SUPER
    hb = ATTN_HEADS
    d1 = DILATIONS[1]
    qb, kb, vb = (col0 // LANES + i * hb for i in range(3))
    blk = ATTN_BLOCK
    nblk = sup // blk
    tail = lambda b: jnp.maximum(b * nblk - 1, 0)
    grid = (hb, s // sup)
    cspec, cshape = _side_cast(cast_w, grid)
    return pl.pallas_call(
        _attn_kernel,
        grid=grid,
        in_specs=[pl.BlockSpec((sup, LANES), lambda h, b: (b, qb + h)),
                  pl.BlockSpec((blk, LANES), lambda h, b: (tail(b), kb + h)),
                  pl.BlockSpec((sup, LANES), lambda h, b: (b, kb + h)),
                  pl.BlockSpec((blk, LANES), lambda h, b: (tail(b), vb + h)),
                  pl.BlockSpec((sup, LANES), lambda h, b: (b, vb + h)),
                  cspec],
        out_specs=[pl.BlockSpec((sup, LANES), lambda h, b: (b, h)), cspec],
        out_shape=[jax.ShapeDtypeStruct((s, ATTN_WIDTH), BF16), cshape],
        scratch_shapes=[pltpu.VMEM((d1, sup // d1, LANES), F32),
                        pltpu.VMEM((2 * d1, sup // d1, LANES), F32),
                        pltpu.VMEM((2 * d1, sup // d1, LANES), F32),
                        pltpu.VMEM((len(DILATIONS), sup, LANES), F32),
                        pltpu.VMEM((len(DILATIONS), sup, LANES), F32),
                        pltpu.VMEM((len(DILATIONS), sup, LANES), F32),
                        pltpu.VMEM((2, blk, 2 * blk), F32)],
        compiler_params=_params("parallel", "arbitrary"),
        name="dilated_attn",
    )(pf, pf, pf, pf, pf, cast_w)


def _gate_weights(w_in_t, i_bias_l, f_bias_l):
    g0 = 2 * MLSTM_QK_WIDTH + 2 * MLSTM_WIDTH
    H = MLSTM_HEADS
    wg = jnp.stack([w_in_t[g0:g0 + H], w_in_t[g0 + H:g0 + 2 * H]], axis=1).reshape(2 * H, w_in_t.shape[1])
    wg = jnp.pad(wg, ((0, LANES - 2 * H), (0, 0)))
    bg = jnp.stack([i_bias_l, f_bias_l], axis=-1).reshape(1, 2 * H)
    bg = jnp.pad(bg, ((0, 0), (0, LANES - 2 * H)))
    return wg, bg.astype(F32)


def _layer(x, norm_mix, w_in, conv_w, conv_b, i_bias, f_bias, mlstm_norm, w_out, norm_mlp, w_up, w_down):
    qk_w = 2 * MLSTM_QK_WIDTH
    g0 = qk_w + 2 * MLSTM_WIDTH
    a0 = g0 + 2 * MLSTM_HEADS
    n_proj = w_in.shape[1] - 2 * MLSTM_HEADS
    assert a0 - g0 == SUBLANES
    w_in_t = w_in.T
    w_g, b_g = _gate_weights(w_in_t, i_bias, f_bias)

    h = _rmsnorm(x, norm_mix, BF16)
    o0 = qk_w + MLSTM_WIDTH
    col_scales = ((o0, g0, 0.5),
                  (g0, g0 + ATTN_WIDTH, ATTN_QSCALE))
    proj, w_up_b = _in_proj(h, w_in_t, g0, n_proj, col_scales, w_up, CAST_SLABS)
    gates = _gate_proj(h, w_g)
    y_a, w_out_b = _dilated_attention(proj, g0, w_out)
    y_m = _mlstm(proj, gates, b_g, conv_w, conv_b.reshape(1, -1), mlstm_norm.reshape(1, -1))
    x1, x1g, ssq = _out_proj(y_m, y_a, w_out_b, x, norm_mlp)
    u, w_down_b = _up_proj(x1g, ssq, w_up_b, w_down)
    return _matmul_ktiled_res(u, w_down_b, x1)


def kernel(x, norm_mix, w_in, conv_w, conv_b, i_bias, f_bias, mlstm_norm, w_out, norm_mlp, w_up, w_down, norm_final):
    b, s, d = x.shape
    outs = []
    for bi in range(b):
        xb = x[bi]
        for layer in range(norm_mix.shape[0]):
            xb = _layer(xb, norm_mix[layer], w_in[layer], conv_w[layer], conv_b[layer], i_bias[layer], f_bias[layer],
                        mlstm_norm[layer], w_out[layer], norm_mlp[layer], w_up[layer], w_down[layer])
        outs.append(_rmsnorm(xb, norm_final, x.dtype))
    return jnp.stack(outs, axis=0)
```
